```python
import jax, jax.numpy as jnp
from jax import lax
import numpy as np

D_MODEL = 1024
BATCH = 8
SEQ = 2048
DEPTH = 1
DEC_BATCH = 8
DEC_SEQ = 32
PAST_LEN = 2048

CHUNK = 64
LEFT_CHUNKS = 8
BAND = (LEFT_CHUNKS + 1) * CHUNK
ATTN_WIDTH = D_MODEL // 2
N_HEADS = 8
HEAD_DIM = ATTN_WIDTH // N_HEADS
CONV_WIDTH = D_MODEL - ATTN_WIDTH
CONV_KERNEL = 31
REL_CLIP = 128
N_REL = 2 * REL_CLIP + 1
D_FF = -(-8 * D_MODEL // (3 * 256)) * 256
D_IN = 3 * ATTN_WIDTH + 2 * CONV_WIDTH
RMS_EPS = 1e-6
LN_EPS = 1e-5
NEG_INF = -1e30

kernel_name = "hymba_style_streaming_conformer_step"


def rmsnorm(x, g):
    xf = x.astype(jnp.float32)
    y = xf * lax.rsqrt(jnp.mean(xf * xf, axis=-1, keepdims=True) + RMS_EPS)
    return (y * g.astype(jnp.float32)).astype(x.dtype)


def layernorm(x, g, b):
    xf = x.astype(jnp.float32)
    mu = jnp.mean(xf, axis=-1, keepdims=True)
    xc = xf - mu
    y = xc * lax.rsqrt(jnp.mean(xc * xc, axis=-1, keepdims=True) + LN_EPS)
    return (y * g.astype(jnp.float32) + b.astype(jnp.float32)).astype(x.dtype)


def rel_bias(table, q_idx, k_idx, offset):
    rel = q_idx[:, None] - k_idx[None, :] + offset
    return table[:, jnp.clip(rel, -REL_CLIP, REL_CLIP) + REL_CLIP]


def chunk_attend(q, k, v, bias, key_valid):
    s = jnp.einsum('bnqhd,bnkhd->bnhqk', q, k).astype(jnp.float32) * (HEAD_DIM ** -0.5)
    s = s + bias.astype(jnp.float32)[None, None]
    if key_valid is not None:
        s = jnp.where(key_valid[None, :, None, None, :], s, NEG_INF)
    p = jax.nn.softmax(s, axis=-1).astype(v.dtype)
    return jnp.einsum('bnhqk,bnkhd->bnqhd', p, v)


def band_attention_prompt(q, k, v, table):
    B, S, H, Dh = q.shape
    nc = S // CHUNK
    pad = LEFT_CHUNKS * CHUNK
    kp = jnp.pad(k, ((0, 0), (pad, 0), (0, 0), (0, 0)))
    vp = jnp.pad(v, ((0, 0), (pad, 0), (0, 0), (0, 0)))
    idx = (jnp.arange(nc) * CHUNK)[:, None] + jnp.arange(BAND)[None, :]
    kb = kp[:, idx]
    vb = vp[:, idx]
    bias = rel_bias(table, jnp.arange(CHUNK), jnp.arange(BAND), pad)
    key_valid = idx >= pad
    o = chunk_attend(q.reshape(B, nc, CHUNK, H, Dh), kb, vb, bias, key_valid)
    return o.reshape(B, S, H * Dh)


def band_attention_sample(q, k, v, cache_k, cache_v, table):
    B, T, H, Dh = q.shape
    W = cache_k.shape[1]
    k_all = jnp.concatenate([cache_k, k], axis=1)
    v_all = jnp.concatenate([cache_v, v], axis=1)
    bias = rel_bias(table, jnp.arange(T), jnp.arange(W + T), W)
    o = chunk_attend(q[:, None], k_all[:, None], v_all[:, None], bias, None)
    return o.reshape(B, T, H * Dh), k_all[:, -W:], v_all[:, -W:]


def conformer_conv(u_glu, conv_past, w_dw, b_dw, ln_g, ln_b):
    a, g = jnp.split(u_glu, 2, axis=-1)
    u = a * jax.nn.sigmoid(g)
    buf = jnp.concatenate([conv_past.astype(u.dtype), u], axis=1)
    y = lax.conv_general_dilated(buf, w_dw[:, None, :].astype(u.dtype), (1,), 'VALID',
                                 dimension_numbers=('NWC', 'WIO', 'NWC'),
                                 feature_group_count=CONV_WIDTH) + b_dw
    y = jax.nn.silu(layernorm(y, ln_g, ln_b))
    return y, buf[:, -(CONV_KERNEL - 1):]


def trunk_layer(x, kv_past, conv_past, g_mix, w_in, rel_table, w_dw, b_dw, ln_g, ln_b,
                w_out, g_ffn, w_gate, w_up, w_down):
    B, T, _ = x.shape
    z = rmsnorm(x, g_mix) @ w_in
    q, k, v, u = jnp.split(z, [ATTN_WIDTH, 2 * ATTN_WIDTH, 3 * ATTN_WIDTH], axis=-1)
    q = q.reshape(B, T, N_HEADS, HEAD_DIM)
    k = k.reshape(B, T, N_HEADS, HEAD_DIM)
    v = v.reshape(B, T, N_HEADS, HEAD_DIM)
    if kv_past is None:
        o_attn = band_attention_prompt(q, k, v, rel_table)
        w_keep = min(LEFT_CHUNKS * CHUNK, T)
        new_k, new_v = k[:, -w_keep:], v[:, -w_keep:]
        conv_past = jnp.zeros((B, CONV_KERNEL - 1, CONV_WIDTH), u.dtype)
    else:
        o_attn, new_k, new_v = band_attention_sample(q, k, v, kv_past[0], kv_past[1], rel_table)
    o_conv, new_conv = conformer_conv(u, conv_past, w_dw, b_dw, ln_g, ln_b)
    x = x + jnp.concatenate([o_attn, o_conv], axis=-1) @ w_out
    h = rmsnorm(x, g_ffn)
    x = x + (jax.nn.silu(h @ w_gate) * (h @ w_up)) @ w_down
    return x, new_k, new_v, new_conv


def setup_inputs(seed: int = 0) -> dict:
    key = jax.random.key(seed)
    ks = jax.random.split(key, 20)
    f32 = jnp.float32
    nrm = lambda k, shape, scale: jax.random.normal(k, shape, f32) * scale
    w_past = min(LEFT_CHUNKS * CHUNK, PAST_LEN)
    return {
        "x_prompt": nrm(ks[0], (BATCH, SEQ, D_MODEL), 1.0),
        "x_sample": nrm(ks[1], (DEC_BATCH, DEC_SEQ, D_MODEL), 1.0),
        "cache_k": nrm(ks[2], (DEPTH, DEC_BATCH, w_past, N_HEADS, HEAD_DIM), 1.0),
        "cache_v": nrm(ks[3], (DEPTH, DEC_BATCH, w_past, N_HEADS, HEAD_DIM), 1.0),
        "state_conv": nrm(ks[4], (DEPTH, DEC_BATCH, CONV_KERNEL - 1, CONV_WIDTH), 1.0),
        "g_mix": 1.0 + nrm(ks[5], (DEPTH, D_MODEL), 0.02),
        "w_in": nrm(ks[6], (DEPTH, D_MODEL, D_IN), D_MODEL ** -0.5),
        "rel_table": nrm(ks[7], (DEPTH, N_HEADS, N_REL), 0.1),
        "w_dw": nrm(ks[8], (DEPTH, CONV_KERNEL, CONV_WIDTH), CONV_KERNEL ** -0.5),
        "b_dw": nrm(ks[9], (DEPTH, CONV_WIDTH), 0.02),
        "ln_g": 1.0 + nrm(ks[10], (DEPTH, CONV_WIDTH), 0.02),
        "ln_b": nrm(ks[11], (DEPTH, CONV_WIDTH), 0.02),
        "w_out": nrm(ks[12], (DEPTH, D_MODEL, D_MODEL), D_MODEL ** -0.5),
        "g_ffn": 1.0 + nrm(ks[13], (DEPTH, D_MODEL), 0.02),
        "w_gate": nrm(ks[14], (DEPTH, D_MODEL, D_FF), D_MODEL ** -0.5),
        "w_up": nrm(ks[15], (DEPTH, D_MODEL, D_FF), D_MODEL ** -0.5),
        "w_down": nrm(ks[16], (DEPTH, D_FF, D_MODEL), D_FF ** -0.5),
        "g_final": 1.0 + nrm(ks[17], (D_MODEL,), 0.02),
    }


def reference(x_prompt, x_sample, cache_k, cache_v, state_conv, g_mix, w_in, rel_table,
              w_dw, b_dw, ln_g, ln_b, w_out, g_ffn, w_gate, w_up, w_down, g_final):
    xp, xs = x_prompt, x_sample
    kp_l, vp_l, cp_l, ks_l, vs_l, cs_l = [], [], [], [], [], []
    for l in range(DEPTH):
        params = (g_mix[l], w_in[l], rel_table[l], w_dw[l], b_dw[l], ln_g[l], ln_b[l],
                  w_out[l], g_ffn[l], w_gate[l], w_up[l], w_down[l])
        xp, kp, vp, cp = trunk_layer(xp, None, None, *params)
        xs, kn, vn, cn = trunk_layer(xs, (cache_k[l], cache_v[l]), state_conv[l], *params)
        kp_l.append(kp); vp_l.append(vp); cp_l.append(cp)
        ks_l.append(kn); vs_l.append(vn); cs_l.append(cn)
    y_prompt = rmsnorm(xp, g_final)
    y_sample = rmsnorm(xs, g_final)
    return (y_prompt, y_sample, jnp.stack(kp_l), jnp.stack(vp_l), jnp.stack(cp_l),
            jnp.stack(ks_l), jnp.stack(vs_l), jnp.stack(cs_l))
```

```python
import functools

import jax
import jax.numpy as jnp
from jax import lax
from jax.experimental import pallas as pl
from jax.experimental.pallas import tpu as pltpu

D_MODEL = 1024
CHUNK = 64
LEFT_CHUNKS = 8
W_BAND = LEFT_CHUNKS * CHUNK
ATTN_WIDTH = 512
N_HEADS = 8
HEAD_DIM = 64
CONV_WIDTH = 512
CONV_KERNEL = 31
CONV_PAST = CONV_KERNEL - 1
REL_CLIP = 128
D_FF = 2816
RMS_EPS = 1e-6
LN_EPS = 1e-5
NEG_INF = -1e30
SCALE = HEAD_DIM ** -0.5

LANES = 128
N_PAIRS = ATTN_WIDTH // LANES
TM_PROJ = 512
TQ = 256
N_KBLK = W_BAND // TQ + 1
TM_FFN = 512
CONV_ROWS = 64
VMEM_LIMIT = 56 * 1024 * 1024

f32 = jnp.float32
bf16 = jnp.bfloat16


def _rmsnorm(x, g):
    return (x * lax.rsqrt(jnp.mean(x * x, axis=-1, keepdims=True) + RMS_EPS)) * g


def _dot(a, b):
    return jnp.dot(a, b, preferred_element_type=f32)


def _dot_nt(a, b):
    return lax.dot_general(a, b, (((1,), (1,)), ((), ())), preferred_element_type=f32)


def _const_spec(shape):
    nd = len(shape)
    return pl.BlockSpec(shape, lambda *_: (0,) * nd, pipeline_mode=pl.Buffered(1))


def _project(x, g_ref, w_ref):
    h = _rmsnorm(x, g_ref[...]).astype(bf16)
    proj = lambda c: _dot(h, w_ref[:, c * ATTN_WIDTH:(c + 1) * ATTN_WIDTH])
    q = proj(0) * SCALE
    lane = lax.broadcasted_iota(jnp.int32, q.shape, 1)
    even = (lane & HEAD_DIM) == 0
    qe = jnp.where(even, q, 0.0).astype(bf16)
    qo = jnp.where(even, 0.0, q).astype(bf16)
    k = proj(1)
    v = proj(2)
    u = proj(3) * jax.nn.sigmoid(proj(4))
    return qe, qo, k, v, u


def _inproj_prompt_kernel(x_ref, g_ref, w_ref, qe_ref, qo_ref, k_ref, v_ref, u_ref,
                          kf_ref, vf_ref, ct_ref):
    qe, qo, k, v, u = _project(x_ref[0], g_ref, w_ref)
    qe_ref[0] = qe
    qo_ref[0] = qo
    k_ref[0] = k.astype(bf16)
    v_ref[0] = v.astype(bf16)
    u_ref[0] = u

    @pl.when(pl.program_id(1) == pl.num_programs(1) - 1)
    def _():
        kf_ref[0] = k
        vf_ref[0] = v
        ct_ref[0] = u_ref[0, TM_PROJ - CONV_PAST:TM_PROJ, :]


def _inproj_prompt(x, g, w):
    B, S, D = x.shape
    assert S % TM_PROJ == 0 and TM_PROJ == W_BAND
    tile = lambda n: pl.BlockSpec((1, TM_PROJ, n), lambda b, i: (b, i, 0))
    per_b = lambda r, n: pl.BlockSpec((1, r, n), lambda b, i: (b, 0, 0))
    act = lambda dt: jax.ShapeDtypeStruct((B, S, ATTN_WIDTH), dt)
    return pl.pallas_call(
        _inproj_prompt_kernel,
        grid=(B, S // TM_PROJ),
        in_specs=[tile(D), _const_spec((1, D)), _const_spec(w.shape)],
        out_specs=[tile(ATTN_WIDTH)] * 5 + [per_b(W_BAND, ATTN_WIDTH)] * 2
                  + [per_b(CONV_PAST, CONV_WIDTH)],
        out_shape=[act(bf16)] * 4 + [act(f32)]
                  + [jax.ShapeDtypeStruct((B, W_BAND, ATTN_WIDTH), f32)] * 2
                  + [jax.ShapeDtypeStruct((B, CONV_PAST, CONV_WIDTH), f32)],
        compiler_params=pltpu.CompilerParams(
            dimension_semantics=("arbitrary", "arbitrary"), vmem_limit_bytes=VMEM_LIMIT),
        name="inproj_prompt",
    )(x, g, w)


def _inproj_sample_kernel(x_ref, g_ref, w_ref, qe_ref, qo_ref, k_ref, v_ref, u_ref):
    qe, qo, k, v, u = _project(x_ref[...], g_ref, w_ref)
    qe_ref[...] = qe
    qo_ref[...] = qo
    k_ref[...] = k
    v_ref[...] = v
    u_ref[...] = u


def _inproj_sample(x, g, w):
    N, D = x.shape
    act = lambda dt: jax.ShapeDtypeStruct((N, ATTN_WIDTH), dt)
    return pl.pallas_call(
        _inproj_sample_kernel,
        grid=(1,),
        in_specs=[_const_spec((N, D)), _const_spec((1, D)), _const_spec(w.shape)],
        out_specs=[_const_spec((N, ATTN_WIDTH))] * 5,
        out_shape=[act(bf16)] * 2 + [act(f32)] * 3,
        compiler_params=pltpu.CompilerParams(vmem_limit_bytes=VMEM_LIMIT),
        name="inproj_sample",
    )(x, g, w)


def _conv_module(cbuf_ref, y_ref, rows, wdw_ref, bdw_ref, lng_ref, lnb_ref):
    rb = min(CONV_ROWS, rows)
    for c in range(CONV_WIDTH // LANES):
        cs = slice(c * LANES, (c + 1) * LANES)
        for r0 in range(0, rows, rb):
            acc = jnp.zeros((rb, LANES), f32)
            for j in range(CONV_KERNEL):
                acc = acc + wdw_ref[j:j + 1, cs] * cbuf_ref[r0 + j + 2:r0 + j + 2 + rb, cs]
            y_ref[r0:r0 + rb, cs] = acc + bdw_ref[:, cs]
    y = y_ref[...]
    mu = jnp.mean(y, axis=-1, keepdims=True)
    yc = y - mu
    yn = yc * lax.rsqrt(jnp.mean(yc * yc, axis=-1, keepdims=True) + LN_EPS)
    yn = yn * lng_ref[...] + lnb_ref[...]
    return yn * jax.nn.sigmoid(yn)


def _softmax_parts(s_parts):
    m = functools.reduce(jnp.maximum, [jnp.max(s, axis=-1, keepdims=True) for s in s_parts])
    e_parts = [jnp.exp(s - m) for s in s_parts]
    l = functools.reduce(jnp.add, [jnp.sum(e, axis=-1, keepdims=True) for e in e_parts])
    return e_parts, l


def _mix_prompt_kernel(qe_ref, qo_ref, k0_ref, k1_ref, k2_ref, v0_ref, v1_ref, v2_ref,
                       u_ref, up_ref, x_ref, bias_ref, wdw_ref, bdw_ref, lng_ref, lnb_ref,
                       wout_ref, o_ref, cbuf_ref, y_ref, cat_ref):
    i = pl.program_id(1)
    k_refs = (k0_ref, k1_ref, k2_ref)
    v_refs = (v0_ref, v1_ref, v2_ref)
    nk = N_KBLK * TQ

    qi = lax.broadcasted_iota(jnp.int32, (TQ, nk), 0)
    kj = lax.broadcasted_iota(jnp.int32, (TQ, nk), 1)
    dc = kj // CHUNK - qi // CHUNK
    valid = (dc >= 0) & (dc <= LEFT_CHUNKS) & (kj >= W_BAND - TQ * i)
    lane = lax.broadcasted_iota(jnp.int32, (TQ, LANES), 1)
    even = lane < HEAD_DIM

    for p in range(N_PAIRS):
        ps = slice(p * LANES, (p + 1) * LANES)
        outs = []
        for par, q_ref in enumerate((qe_ref, qo_ref)):
            q2 = q_ref[0, :, ps]
            s = jnp.concatenate([_dot_nt(q2, kr[0, :, ps]) for kr in k_refs], axis=1)
            s = jnp.where(valid, s + bias_ref[2 * p + par], NEG_INF)
            (e,), l = _softmax_parts([s])
            pb = e.astype(bf16)
            o = functools.reduce(jnp.add, [
                _dot(pb[:, n * TQ:(n + 1) * TQ], vr[0, :, ps]) for n, vr in enumerate(v_refs)])
            outs.append(o / l)
        cat_ref[:, ps] = jnp.where(even, outs[0], outs[1]).astype(bf16)

    cbuf_ref[0:32, :] = jnp.where(i == 0, 0.0, up_ref[0])
    cbuf_ref[32:32 + TQ, :] = u_ref[0]
    o_conv = _conv_module(cbuf_ref, y_ref, TQ, wdw_ref, bdw_ref, lng_ref, lnb_ref)
    cat_ref[:, ATTN_WIDTH:] = o_conv.astype(bf16)

    o_ref[0] = x_ref[0] + _dot(cat_ref[...], wout_ref[...])


def _mix_prompt(qe, qo, k, v, u, x, bias, wdw, bdw, lng, lnb, wout):
    B, S, D = x.shape
    assert S % TQ == 0 and W_BAND % TQ == 0
    qspec = pl.BlockSpec((1, TQ, ATTN_WIDTH), lambda b, i: (b, i, 0))
    kspec = lambda back: pl.BlockSpec(
        (1, TQ, ATTN_WIDTH), lambda b, i: (b, jnp.maximum(i - back, 0), 0))
    kspecs = [kspec(N_KBLK - 1 - n) for n in range(N_KBLK)]
    upspec = pl.BlockSpec((1, 32, CONV_WIDTH),
                          lambda b, i: (b, jnp.maximum(i * (TQ // 32) - 1, 0), 0))
    xspec = pl.BlockSpec((1, TQ, D), lambda b, i: (b, i, 0))
    return pl.pallas_call(
        _mix_prompt_kernel,
        grid=(B, S // TQ),
        in_specs=[qspec, qspec] + kspecs + kspecs + [qspec, upspec, xspec,
                  _const_spec(bias.shape), _const_spec(wdw.shape), _const_spec(bdw.shape),
                  _const_spec(lng.shape), _const_spec(lnb.shape), _const_spec(wout.shape)],
        out_specs=xspec,
        out_shape=jax.ShapeDtypeStruct((B, S, D), f32),
        scratch_shapes=[pltpu.VMEM((32 + TQ, CONV_WIDTH), f32),
                        pltpu.VMEM((TQ, CONV_WIDTH), f32),
                        pltpu.VMEM((TQ, D), bf16)],
        compiler_params=pltpu.CompilerParams(
            dimension_semantics=("arbitrary", "arbitrary"), vmem_limit_bytes=VMEM_LIMIT),
        name="mix_prompt",
    )(qe, qo, k, k, k, v, v, v, u, u, x, bias, wdw, bdw, lng, lnb, wout)


def _mix_sample_kernel(qe_ref, qo_ref, kn_ref, vn_ref, ck_ref, cv_ref, u_ref, st_ref, x_ref,
                       biasc_ref, biasn_ref, wdw_ref, bdw_ref, lng_ref, lnb_ref, wout_ref,
                       o_ref, nk_ref, nv_ref, nc_ref, cbuf_ref, y_ref, cat_ref):
    T = u_ref.shape[1]
    W = ck_ref.shape[1]
    nk_ref[0, 0:W - T, :] = ck_ref[0, T:W, :]
    nk_ref[0, W - T:W, :] = kn_ref[0]
    nv_ref[0, 0:W - T, :] = cv_ref[0, T:W, :]
    nv_ref[0, W - T:W, :] = vn_ref[0]
    nc_ref[0] = u_ref[0, T - CONV_PAST:T, :]

    lane = lax.broadcasted_iota(jnp.int32, (T, LANES), 1)
    even = lane < HEAD_DIM
    for p in range(N_PAIRS):
        ps = slice(p * LANES, (p + 1) * LANES)
        kc = ck_ref[0, :, ps].astype(bf16)
        vc = cv_ref[0, :, ps].astype(bf16)
        kn = kn_ref[0, :, ps].astype(bf16)
        vn = vn_ref[0, :, ps].astype(bf16)
        outs = []
        for par, q_ref in enumerate((qe_ref, qo_ref)):
            q2 = q_ref[0, :, ps]
            h = 2 * p + par
            (ec, en), l = _softmax_parts([_dot_nt(q2, kc) + biasc_ref[h],
                                          _dot_nt(q2, kn) + biasn_ref[h]])
            o = _dot(ec.astype(bf16), vc) + _dot(en.astype(bf16), vn)
            outs.append(o / l)
        cat_ref[:, ps] = jnp.where(even, outs[0], outs[1]).astype(bf16)

    cbuf_ref[0:8, :] = jnp.zeros((8, CONV_WIDTH), f32)
    cbuf_ref[32 - CONV_PAST:32, :] = st_ref[0]
    cbuf_ref[32:32 + T, :] = u_ref[0]
    o_conv = _conv_module(cbuf_ref, y_ref, T, wdw_ref, bdw_ref, lng_ref, lnb_ref)
    cat_ref[:, ATTN_WIDTH:] = o_conv.astype(bf16)

    o_ref[0] = x_ref[0] + _dot(cat_ref[...], wout_ref[...])


def _mix_sample(qe, qo, kn, vn, ck, cv, u, st, x, biasc, biasn, wdw, bdw, lng, lnb, wout):
    B, T, D = x.shape
    W = ck.shape[1]
    assert T >= CONV_PAST and T % 8 == 0
    row = lambda r, n: pl.BlockSpec((1, r, n), lambda b: (b, 0, 0))
    return pl.pallas_call(
        _mix_sample_kernel,
        grid=(B,),
        in_specs=[row(T, ATTN_WIDTH)] * 4 + [row(W, ATTN_WIDTH)] * 2
                 + [row(T, CONV_WIDTH), row(CONV_PAST, CONV_WIDTH), row(T, D)]
                 + [_const_spec(a.shape) for a in (biasc, biasn, wdw, bdw, lng, lnb, wout)],
        out_specs=[row(T, D), row(W, ATTN_WIDTH), row(W, ATTN_WIDTH), row(CONV_PAST, CONV_WIDTH)],
        out_shape=[jax.ShapeDtypeStruct((B, T, D), f32),
                   jax.ShapeDtypeStruct((B, W, ATTN_WIDTH), f32),
                   jax.ShapeDtypeStruct((B, W, ATTN_WIDTH), f32),
                   jax.ShapeDtypeStruct((B, CONV_PAST, CONV_WIDTH), f32)],
        scratch_shapes=[pltpu.VMEM((32 + T, CONV_WIDTH), f32),
                        pltpu.VMEM((T, CONV_WIDTH), f32),
                        pltpu.VMEM((T, D), bf16)],
        compiler_params=pltpu.CompilerParams(
            dimension_semantics=("arbitrary",), vmem_limit_bytes=VMEM_LIMIT),
        name="mix_sample",
    )(qe, qo, kn, vn, ck, cv, u, st, x, biasc, biasn, wdw, bdw, lng, lnb, wout)


def _ffn_kernel(x_ref, g_ref, wg_ref, wu_ref, wd_ref, gf_ref, o_ref):
    x = x_ref[...]
    h = _rmsnorm(x, g_ref[...]).astype(bf16)
    half = D_FF // 2
    acc = x
    for c in range(2):
        cs = slice(c * half, (c + 1) * half)
        a = jax.nn.silu(_dot(h, wg_ref[:, cs])) * _dot(h, wu_ref[:, cs])
        acc = acc + _dot(a.astype(bf16), wd_ref[cs, :])
    o_ref[...] = _rmsnorm(acc, gf_ref[...])


def _ffn(x, g, wg, wu, wd, gf, tm):
    N, D = x.shape
    assert N % tm == 0
    tile = pl.BlockSpec((tm, D), lambda i: (i, 0))
    return pl.pallas_call(
        _ffn_kernel,
        grid=(N // tm,),
        in_specs=[tile, _const_spec((1, D)), _const_spec(wg.shape), _const_spec(wu.shape),
                  _const_spec(wd.shape), _const_spec((1, D))],
        out_specs=tile,
        out_shape=jax.ShapeDtypeStruct((N, D), f32),
        compiler_params=pltpu.CompilerParams(
            dimension_semantics=("arbitrary",), vmem_limit_bytes=VMEM_LIMIT),
        name="ffn",
    )(x, g, wg, wu, wd, gf)


def _rel_bias(table, n_q, n_k, offset):
    rel = jnp.arange(n_q)[:, None] - jnp.arange(n_k)[None, :] + offset
    return table[:, jnp.clip(rel, -REL_CLIP, REL_CLIP) + REL_CLIP]


def kernel(x_prompt, x_sample, cache_k, cache_v, state_conv, g_mix, w_in, rel_table, w_dw, b_dw,
           ln_g, ln_b, w_out, g_ffn, w_gate, w_up, w_down, g_final):
    assert g_mix.shape[0] == 1, "single-layer trunk: the final RMSNorm is fused into the FFN kernel"
    B, S, D = x_prompt.shape
    Bs, T, _ = x_sample.shape
    W = cache_k.shape[2]
    gm, gff, gf = g_mix[0].reshape(1, D), g_ffn[0].reshape(1, D), g_final.reshape(1, D)
    win, wout = w_in[0].astype(bf16), w_out[0].astype(bf16)
    wg, wu, wd = w_gate[0].astype(bf16), w_up[0].astype(bf16), w_down[0].astype(bf16)
    wdw = w_dw[0]
    bdw, lng, lnb = (a[0].reshape(1, CONV_WIDTH) for a in (b_dw, ln_g, ln_b))
    heads = lambda a: a.reshape(1, a.shape[0], a.shape[1], N_HEADS, HEAD_DIM)

    bias_p = _rel_bias(rel_table[0], TQ, N_KBLK * TQ, W_BAND)
    qe, qo, k, v, u, kf, vf, ct = _inproj_prompt(x_prompt, gm, win)
    x1 = _mix_prompt(qe, qo, k, v, u, x_prompt, bias_p, wdw, bdw, lng, lnb, wout)
    y_prompt = _ffn(x1.reshape(B * S, D), gff, wg, wu, wd, gf, TM_FFN).reshape(B, S, D)

    bias_s = _rel_bias(rel_table[0], T, W + T, W)
    qe, qo, kn, vn, u = _inproj_sample(x_sample.reshape(Bs * T, D), gm, win)
    r3 = lambda a: a.reshape(Bs, T, ATTN_WIDTH)
    x1, nk, nv, nc = _mix_sample(
        r3(qe), r3(qo), r3(kn), r3(vn),
        cache_k[0].reshape(Bs, W, ATTN_WIDTH), cache_v[0].reshape(Bs, W, ATTN_WIDTH),
        r3(u), state_conv[0], x_sample, bias_s[:, :, :W], bias_s[:, :, W:],
        wdw, bdw, lng, lnb, wout)
    y_sample = _ffn(x1.reshape(Bs * T, D), gff, wg, wu, wd, gf, Bs * T).reshape(Bs, T, D)

    return (y_prompt, y_sample, heads(kf), heads(vf), ct[None],
            heads(nk), heads(nv), nc[None])
```

```python
import functools

import jax
import jax.numpy as jnp
from jax import lax
from jax.experimental import pallas as pl
from jax.experimental.pallas import tpu as pltpu

D_MODEL = 1024
CHUNK = 64
LEFT_CHUNKS = 8
W_BAND = LEFT_CHUNKS * CHUNK
ATTN_WIDTH = 512
N_HEADS = 8
HEAD_DIM = 64
CONV_WIDTH = 512
CONV_KERNEL = 31
CONV_PAST = CONV_KERNEL - 1
REL_CLIP = 128
D_FF = 2816
RMS_EPS = 1e-6
LN_EPS = 1e-5
NEG_INF = -1e30
SCALE = HEAD_DIM ** -0.5

LANES = 128
N_PAIRS = ATTN_WIDTH // LANES
TM_PROJ = 512
TQ = 256
N_KBLK = W_BAND // TQ + 1
F_LEN = 1024
TM_FFN = 512
CONV_ROWS = 64
VMEM_LIMIT = 56 * 1024 * 1024

f32 = jnp.float32
bf16 = jnp.bfloat16


def _rmsnorm(x, g):
    return (x * lax.rsqrt(jnp.mean(x * x, axis=-1, keepdims=True) + RMS_EPS)) * g


def _dot(a, b):
    return jnp.dot(a, b, preferred_element_type=f32)


def _dot_nt(a, b):
    return lax.dot_general(a, b, (((1,), (1,)), ((), ())), preferred_element_type=f32)


def _const_spec(shape):
    nd = len(shape)
    return pl.BlockSpec(shape, lambda *_: (0,) * nd, pipeline_mode=pl.Buffered(1))


def _project(x, g_ref, w_ref):
    h = _rmsnorm(x, g_ref[...]).astype(bf16)
    proj = lambda c: _dot(h, w_ref[:, c * ATTN_WIDTH:(c + 1) * ATTN_WIDTH])
    q = proj(0) * SCALE
    lane = lax.broadcasted_iota(jnp.int32, q.shape, 1)
    even = (lane & HEAD_DIM) == 0
    qe = jnp.where(even, q, 0.0).astype(bf16)
    qo = jnp.where(even, 0.0, q).astype(bf16)
    k = proj(1)
    v = proj(2)
    u = proj(3) * jax.nn.sigmoid(proj(4))
    return qe, qo, k, v, u


def _inproj_prompt_kernel(x_ref, g_ref, w_ref, qe_ref, qo_ref, k_ref, v_ref, u_ref,
                          kf_ref, vf_ref, ct_ref):
    qe, qo, k, v, u = _project(x_ref[0], g_ref, w_ref)
    qe_ref[0] = qe
    qo_ref[0] = qo
    k_ref[0] = k.astype(bf16)
    v_ref[0] = v.astype(bf16)
    u_ref[0] = u

    @pl.when(pl.program_id(1) == pl.num_programs(1) - 1)
    def _():
        kf_ref[0] = k
        vf_ref[0] = v
        ct_ref[0] = u_ref[0, TM_PROJ - CONV_PAST:TM_PROJ, :]


def _inproj_prompt(x, g, w):
    B, S, D = x.shape
    assert S % TM_PROJ == 0 and TM_PROJ == W_BAND
    tile = lambda n: pl.BlockSpec((1, TM_PROJ, n), lambda b, i: (b, i, 0))
    per_b = lambda r, n: pl.BlockSpec((1, r, n), lambda b, i: (b, 0, 0))
    act = lambda dt: jax.ShapeDtypeStruct((B, S, ATTN_WIDTH), dt)
    return pl.pallas_call(
        _inproj_prompt_kernel,
        grid=(B, S // TM_PROJ),
        in_specs=[tile(D), _const_spec((1, D)), _const_spec(w.shape)],
        out_specs=[tile(ATTN_WIDTH)] * 5 + [per_b(W_BAND, ATTN_WIDTH)] * 2
                  + [per_b(CONV_PAST, CONV_WIDTH)],
        out_shape=[act(bf16)] * 4 + [act(f32)]
                  + [jax.ShapeDtypeStruct((B, W_BAND, ATTN_WIDTH), f32)] * 2
                  + [jax.ShapeDtypeStruct((B, CONV_PAST, CONV_WIDTH), f32)],
        compiler_params=pltpu.CompilerParams(
            dimension_semantics=("arbitrary", "arbitrary"), vmem_limit_bytes=VMEM_LIMIT),
        name="inproj_prompt",
    )(x, g, w)


def _inproj_sample_kernel(x_ref, g_ref, w_ref, qe_ref, qo_ref, k_ref, v_ref, u_ref):
    qe, qo, k, v, u = _project(x_ref[...], g_ref, w_ref)
    qe_ref[...] = qe
    qo_ref[...] = qo
    k_ref[...] = k
    v_ref[...] = v
    u_ref[...] = u


def _inproj_sample(x, g, w):
    N, D = x.shape
    act = lambda dt: jax.ShapeDtypeStruct((N, ATTN_WIDTH), dt)
    return pl.pallas_call(
        _inproj_sample_kernel,
        grid=(1,),
        in_specs=[_const_spec((N, D)), _const_spec((1, D)), _const_spec(w.shape)],
        out_specs=[_const_spec((N, ATTN_WIDTH))] * 5,
        out_shape=[act(bf16)] * 2 + [act(f32)] * 3,
        compiler_params=pltpu.CompilerParams(vmem_limit_bytes=VMEM_LIMIT),
        name="inproj_sample",
    )(x, g, w)


def _conv_module(cbuf_ref, y_ref, rows, wdw_ref, bdw_ref, lng_ref, lnb_ref):
    rb = min(CONV_ROWS, rows)
    for c in range(CONV_WIDTH // LANES):
        cs = slice(c * LANES, (c + 1) * LANES)
        for r0 in range(0, rows, rb):
            acc = jnp.zeros((rb, LANES), f32)
            for j in range(CONV_KERNEL):
                acc = acc + wdw_ref[j:j + 1, cs] * cbuf_ref[r0 + j + 2:r0 + j + 2 + rb, cs]
            y_ref[r0:r0 + rb, cs] = acc + bdw_ref[:, cs]
    y = y_ref[...]
    mu = jnp.mean(y, axis=-1, keepdims=True)
    yc = y - mu
    yn = yc * lax.rsqrt(jnp.mean(yc * yc, axis=-1, keepdims=True) + LN_EPS)
    yn = yn * lng_ref[...] + lnb_ref[...]
    return yn * jax.nn.sigmoid(yn)


def _toeplitz(frow, rows):
    return pltpu.roll(jnp.broadcast_to(frow, (rows, F_LEN)), 0, 1, stride=1, stride_axis=0)


def _softmax_parts(s_parts):
    m = functools.reduce(jnp.maximum, [jnp.max(s, axis=-1, keepdims=True) for s in s_parts])
    e_parts = [jnp.exp(s - m) for s in s_parts]
    l = functools.reduce(jnp.add, [jnp.sum(e, axis=-1, keepdims=True) for e in e_parts])
    return e_parts, l


def _mix_prompt_kernel(qe_ref, qo_ref, k0_ref, k1_ref, k2_ref, v0_ref, v1_ref, v2_ref,
                       u_ref, up_ref, x_ref, f_ref, wdw_ref, bdw_ref, lng_ref, lnb_ref,
                       wout_ref, o_ref, bias_ref, band_ref, cbuf_ref, y_ref, cat_ref):
    i = pl.program_id(1)
    k_refs = (k0_ref, k1_ref, k2_ref)
    v_refs = (v0_ref, v1_ref, v2_ref)
    nk = N_KBLK * TQ

    @pl.when((pl.program_id(0) == 0) & (i == 0))
    def _():
        for h in range(N_HEADS):
            bias_ref[h] = _toeplitz(f_ref[h:h + 1, :], TQ)[:, :nk]
        qi = lax.broadcasted_iota(jnp.int32, (TQ, nk), 0)
        kj = lax.broadcasted_iota(jnp.int32, (TQ, nk), 1)
        dc = kj // CHUNK - qi // CHUNK
        band_ref[...] = jnp.where((dc >= 0) & (dc <= LEFT_CHUNKS), kj, -1)

    valid = band_ref[...] >= jnp.maximum(W_BAND - TQ * i, 0)
    lane = lax.broadcasted_iota(jnp.int32, (TQ, LANES), 1)
    even = lane < HEAD_DIM

    for p in range(N_PAIRS):
        ps = slice(p * LANES, (p + 1) * LANES)
        outs = []
        for par, q_ref in enumerate((qe_ref, qo_ref)):
            q2 = q_ref[0, :, ps]
            s = jnp.concatenate([_dot_nt(q2, kr[0, :, ps]) for kr in k_refs], axis=1)
            s = jnp.where(valid, s + bias_ref[2 * p + par], NEG_INF)
            (e,), l = _softmax_parts([s])
            pb = e.astype(bf16)
            o = functools.reduce(jnp.add, [
                _dot(pb[:, n * TQ:(n + 1) * TQ], vr[0, :, ps]) for n, vr in enumerate(v_refs)])
            outs.append(o / l)
        cat_ref[:, ps] = jnp.where(even, outs[0], outs[1]).astype(bf16)

    cbuf_ref[0:32, :] = jnp.where(i == 0, 0.0, up_ref[0])
    cbuf_ref[32:32 + TQ, :] = u_ref[0]
    o_conv = _conv_module(cbuf_ref, y_ref, TQ, wdw_ref, bdw_ref, lng_ref, lnb_ref)
    cat_ref[:, ATTN_WIDTH:] = o_conv.astype(bf16)

    o_ref[0] = x_ref[0] + _dot(cat_ref[...], wout_ref[...])


def _mix_prompt(qe, qo, k, v, u, x, fvec, wdw, bdw, lng, lnb, wout):
    B, S, D = x.shape
    assert S % TQ == 0 and W_BAND % TQ == 0
    qspec = pl.BlockSpec((1, TQ, ATTN_WIDTH), lambda b, i: (b, i, 0))
    kspec = lambda back: pl.BlockSpec(
        (1, TQ, ATTN_WIDTH), lambda b, i: (b, jnp.maximum(i - back, 0), 0))
    kspecs = [kspec(N_KBLK - 1 - n) for n in range(N_KBLK)]
    upspec = pl.BlockSpec((1, 32, CONV_WIDTH),
                          lambda b, i: (b, jnp.maximum(i * (TQ // 32) - 1, 0), 0))
    xspec = pl.BlockSpec((1, TQ, D), lambda b, i: (b, i, 0))
    return pl.pallas_call(
        _mix_prompt_kernel,
        grid=(B, S // TQ),
        in_specs=[qspec, qspec] + kspecs + kspecs + [qspec, upspec, xspec,
                  _const_spec(fvec.shape), _const_spec(wdw.shape), _const_spec(bdw.shape),
                  _const_spec(lng.shape), _const_spec(lnb.shape), _const_spec(wout.shape)],
        out_specs=xspec,
        out_shape=jax.ShapeDtypeStruct((B, S, D), f32),
        scratch_shapes=[pltpu.VMEM((N_HEADS, TQ, N_KBLK * TQ), f32),
                        pltpu.VMEM((TQ, N_KBLK * TQ), jnp.int32),
                        pltpu.VMEM((32 + TQ, CONV_WIDTH), f32),
                        pltpu.VMEM((TQ, CONV_WIDTH), f32),
                        pltpu.VMEM((TQ, D), bf16)],
        compiler_params=pltpu.CompilerParams(
            dimension_semantics=("arbitrary", "arbitrary"), vmem_limit_bytes=VMEM_LIMIT),
        name="mix_prompt",
    )(qe, qo, k, k, k, v, v, v, u, u, x, fvec, wdw, bdw, lng, lnb, wout)


def _mix_sample_kernel(qe_ref, qo_ref, kn_ref, vn_ref, ck_ref, cv_ref, u_ref, st_ref, x_ref,
                       f_ref, wdw_ref, bdw_ref, lng_ref, lnb_ref, wout_ref,
                       o_ref, nk_ref, nv_ref, nc_ref, bias_ref, cbuf_ref, y_ref, cat_ref):
    T = u_ref.shape[1]
    W = ck_ref.shape[1]

    @pl.when(pl.program_id(0) == 0)
    def _():
        for h in range(N_HEADS):
            bias_ref[h] = _toeplitz(f_ref[h:h + 1, :], T)

    nk_ref[0, 0:W - T, :] = ck_ref[0, T:W, :]
    nk_ref[0, W - T:W, :] = kn_ref[0]
    nv_ref[0, 0:W - T, :] = cv_ref[0, T:W, :]
    nv_ref[0, W - T:W, :] = vn_ref[0]
    nc_ref[0] = u_ref[0, T - CONV_PAST:T, :]

    lane = lax.broadcasted_iota(jnp.int32, (T, LANES), 1)
    even = lane < HEAD_DIM
    for p in range(N_PAIRS):
        ps = slice(p * LANES, (p + 1) * LANES)
        kc = ck_ref[0, :, ps].astype(bf16)
        vc = cv_ref[0, :, ps].astype(bf16)
        kn = kn_ref[0, :, ps].astype(bf16)
        vn = vn_ref[0, :, ps].astype(bf16)
        outs = []
        for par, q_ref in enumerate((qe_ref, qo_ref)):
            q2 = q_ref[0, :, ps]
            h = 2 * p + par
            (ec, en), l = _softmax_parts([_dot_nt(q2, kc) + bias_ref[h, :, 0:W],
                                          _dot_nt(q2, kn) + bias_ref[h, :, W:W + T]])
            o = _dot(ec.astype(bf16), vc) + _dot(en.astype(bf16), vn)
            outs.append(o / l)
        cat_ref[:, ps] = jnp.where(even, outs[0], outs[1]).astype(bf16)

    cbuf_ref[0:8, :] = jnp.zeros((8, CONV_WIDTH), f32)
    cbuf_ref[32 - CONV_PAST:32, :] = st_ref[0]
    cbuf_ref[32:32 + T, :] = u_ref[0]
    o_conv = _conv_module(cbuf_ref, y_ref, T, wdw_ref, bdw_ref, lng_ref, lnb_ref)
    cat_ref[:, ATTN_WIDTH:] = o_conv.astype(bf16)

    o_ref[0] = x_ref[0] + _dot(cat_ref[...], wout_ref[...])


def _mix_sample(qe, qo, kn, vn, ck, cv, u, st, x, fvec, wdw, bdw, lng, lnb, wout):
    B, T, D = x.shape
    W = ck.shape[1]
    assert T >= CONV_PAST and T % 8 == 0 and W == W_BAND
    row = lambda r, n: pl.BlockSpec((1, r, n), lambda b: (b, 0, 0))
    return pl.pallas_call(
        _mix_sample_kernel,
        grid=(B,),
        in_specs=[row(T, ATTN_WIDTH)] * 4 + [row(W, ATTN_WIDTH)] * 2
                 + [row(T, CONV_WIDTH), row(CONV_PAST, CONV_WIDTH), row(T, D)]
                 + [_const_spec(a.shape) for a in (fvec, wdw, bdw, lng, lnb, wout)],
        out_specs=[row(T, D), row(W, ATTN_WIDTH), row(W, ATTN_WIDTH), row(CONV_PAST, CONV_WIDTH)],
        out_shape=[jax.ShapeDtypeStruct((B, T, D), f32),
                   jax.ShapeDtypeStruct((B, W, ATTN_WIDTH), f32),
                   jax.ShapeDtypeStruct((B, W, ATTN_WIDTH), f32),
                   jax.ShapeDtypeStruct((B, CONV_PAST, CONV_WIDTH), f32)],
        scratch_shapes=[pltpu.VMEM((N_HEADS, T, F_LEN), f32),
                        pltpu.VMEM((32 + T, CONV_WIDTH), f32),
                        pltpu.VMEM((T, CONV_WIDTH), f32),
                        pltpu.VMEM((T, D), bf16)],
        compiler_params=pltpu.CompilerParams(
            dimension_semantics=("arbitrary",), vmem_limit_bytes=VMEM_LIMIT),
        name="mix_sample",
    )(qe, qo, kn, vn, ck, cv, u, st, x, fvec, wdw, bdw, lng, lnb, wout)


def _ffn_kernel(x_ref, g_ref, wg_ref, wu_ref, wd_ref, gf_ref, o_ref):
    x = x_ref[...]
    h = _rmsnorm(x, g_ref[...]).astype(bf16)
    half = D_FF // 2
    acc = x
    for c in range(2):
        cs = slice(c * half, (c + 1) * half)
        a = jax.nn.silu(_dot(h, wg_ref[:, cs])) * _dot(h, wu_ref[:, cs])
        acc = acc + _dot(a.astype(bf16), wd_ref[cs, :])
    o_ref[...] = _rmsnorm(acc, gf_ref[...])


def _ffn(x, g, wg, wu, wd, gf, tm):
    N, D = x.shape
    assert N % tm == 0
    tile = pl.BlockSpec((tm, D), lambda i: (i, 0))
    return pl.pallas_call(
        _ffn_kernel,
        grid=(N // tm,),
        in_specs=[tile, _const_spec((1, D)), _const_spec(wg.shape), _const_spec(wu.shape),
                  _const_spec(wd.shape), _const_spec((1, D))],
        out_specs=tile,
        out_shape=jax.ShapeDtypeStruct((N, D), f32),
        compiler_params=pltpu.CompilerParams(
            dimension_semantics=("arbitrary",), vmem_limit_bytes=VMEM_LIMIT),
        name="ffn",
    )(x, g, wg, wu, wd, gf)


def _rel_bias_row(table):
    n_rel = table.shape[1]
    far = W_BAND - REL_CLIP
    assert n_rel == 2 * REL_CLIP + 1 and far + n_rel + TQ <= F_LEN
    rep = lambda col, n: jnp.broadcast_to(table[:, col:col + 1], (table.shape[0], n))
    return jnp.concatenate([rep(n_rel - 1, far), table[:, ::-1],
                            rep(0, F_LEN - TQ - far - n_rel), rep(n_rel - 1, TQ)], axis=1)


def kernel(x_prompt, x_sample, cache_k, cache_v, state_conv, g_mix, w_in, rel_table, w_dw, b_dw,
           ln_g, ln_b, w_out, g_ffn, w_gate, w_up, w_down, g_final):
    assert g_mix.shape[0] == 1, "single-layer trunk: the final RMSNorm is fused into the FFN kernel"
    B, S, D = x_prompt.shape
    Bs, T, _ = x_sample.shape
    W = cache_k.shape[2]
    gm, gff, gf = g_mix[0].reshape(1, D), g_ffn[0].reshape(1, D), g_final.reshape(1, D)
    win, wout = w_in[0].astype(bf16), w_out[0].astype(bf16)
    wg, wu, wd = w_gate[0].astype(bf16), w_up[0].astype(bf16), w_down[0].astype(bf16)
    wdw = w_dw[0]
    bdw, lng, lnb = (a[0].reshape(1, CONV_WIDTH) for a in (b_dw, ln_g, ln_b))
    heads = lambda a: a.reshape(1, a.shape[0], a.shape[1], N_HEADS, HEAD_DIM)
    fvec = _rel_bias_row(rel_table[0])

    qe, qo, k, v, u, kf, vf, ct = _inproj_prompt(x_prompt, gm, win)
    x1 = _mix_prompt(qe, qo, k, v, u, x_prompt, fvec, wdw, bdw, lng, lnb, wout)
    y_prompt = _ffn(x1.reshape(B * S, D), gff, wg, wu, wd, gf, TM_FFN).reshape(B, S, D)

    qe, qo, kn, vn, u = _inproj_sample(x_sample.reshape(Bs * T, D), gm, win)
    r3 = lambda a: a.reshape(Bs, T, ATTN_WIDTH)
    x1, nk, nv, nc = _mix_sample(
        r3(qe), r3(qo), r3(kn), r3(vn),
        cache_k[0].reshape(Bs, W, ATTN_WIDTH), cache_v[0].reshape(Bs, W, ATTN_WIDTH),
        r3(u), state_conv[0], x_sample, fvec, wdw, bdw, lng, lnb, wout)
    y_sample = _ffn(x1.reshape(Bs * T, D), gff, wg, wu, wd, gf, Bs * T).reshape(Bs, T, D)

    return (y_prompt, y_sample, heads(kf), heads(vf), ct[None],
            heads(nk), heads(nv), nc[None])
```

```python
import functools

import jax
import jax.numpy as jnp
from jax import lax
from jax.experimental import pallas as pl
from jax.experimental.pallas import tpu as pltpu

D_MODEL = 1024
CHUNK = 64
LEFT_CHUNKS = 8
W_BAND = LEFT_CHUNKS * CHUNK
ATTN_WIDTH = 512
N_HEADS = 8
HEAD_DIM = 64
CONV_WIDTH = 512
CONV_KERNEL = 31
CONV_PAST = CONV_KERNEL - 1
REL_CLIP = 128
D_FF = 2816
RMS_EPS = 1e-6
LN_EPS = 1e-5
NEG_INF = -1e30
SCALE = HEAD_DIM ** -0.5

LANES = 128
SUBLANES = 8
N_PAIRS = ATTN_WIDTH // LANES
TM_PROJ = 512
PROJ_COLS = 256
TQ = 256
N_KBLK = W_BAND // TQ + 1
SUB = 2 * CHUNK
WIN = W_BAND + SUB
F_LEN = 1024
TM_FFN = 512
CONV_ROWS = 64
CONV_LEAD = 32
VMEM_LIMIT = 56 * 1024 * 1024

f32 = jnp.float32
bf16 = jnp.bfloat16


def _rmsnorm(x, g):
    return (x * lax.rsqrt(jnp.mean(x * x, axis=-1, keepdims=True) + RMS_EPS)) * g


def _dot(a, b):
    return jnp.dot(a, b, preferred_element_type=f32)


def _dot_nt(a, b):
    return lax.dot_general(a, b, (((1,), (1,)), ((), ())), preferred_element_type=f32)


def _const_spec(shape):
    nd = len(shape)
    return pl.BlockSpec(shape, lambda *_: (0,) * nd, pipeline_mode=pl.Buffered(1))


def _project_stages(h, w_ref, qe_store, qo_store, k_store, v_store):
    def q_stage(cs):
        q = _dot(h, w_ref[:, cs]) * SCALE
        lane = lax.broadcasted_iota(jnp.int32, q.shape, 1)
        even = (lane & HEAD_DIM) == 0
        qe_store(cs, jnp.where(even, q, 0.0).astype(bf16))
        qo_store(cs, jnp.where(even, 0.0, q).astype(bf16))

    def kv_stage(base, store, cs):
        store(cs, _dot(h, w_ref[:, base + cs.start:base + cs.stop]))

    stages = []
    for c0 in range(0, ATTN_WIDTH, PROJ_COLS):
        stages.append(functools.partial(q_stage, slice(c0, c0 + PROJ_COLS)))
    for base, store in ((ATTN_WIDTH, k_store), (2 * ATTN_WIDTH, v_store)):
        for c0 in range(0, ATTN_WIDTH, PROJ_COLS):
            stages.append(functools.partial(kv_stage, base, store, slice(c0, c0 + PROJ_COLS)))
    return stages


def _glu(h, w_ref):
    base = 3 * ATTN_WIDTH
    a = _dot(h, w_ref[:, base:base + CONV_WIDTH])
    return a * jax.nn.sigmoid(_dot(h, w_ref[:, base + CONV_WIDTH:base + 2 * CONV_WIDTH]))


def _conv_stages(cbuf_ref, sh_ref, y_ref, rows, wdw_ref, bdw_ref):
    lead = CONV_LEAD - CONV_PAST
    span = rows + CONV_LEAD - SUBLANES
    rb = min(CONV_ROWS, rows)

    def shift_stage(m):
        sh_ref[m - 1, 0:span, :] = cbuf_ref[m:m + span, :]

    def tap_stage(cs, r0):
        acc = jnp.zeros((rb, LANES), f32)
        for j in range(CONV_KERNEL):
            a, m = divmod(j + lead, SUBLANES)
            src = cbuf_ref if m == 0 else sh_ref.at[m - 1]
            lo = r0 + SUBLANES * a
            acc = acc + wdw_ref[j:j + 1, cs] * src[lo:lo + rb, cs]
        y_ref[r0:r0 + rb, cs] = acc + bdw_ref[:, cs]

    shifts = [functools.partial(shift_stage, m) for m in range(1, SUBLANES)]
    taps = [functools.partial(tap_stage, slice(c * LANES, (c + 1) * LANES), r0)
            for c in range(CONV_WIDTH // LANES) for r0 in range(0, rows, rb)]
    return shifts, taps


def _ln_swish(y, lng_ref, lnb_ref):
    mu = jnp.mean(y, axis=-1, keepdims=True)
    yc = y - mu
    yn = yc * lax.rsqrt(jnp.mean(yc * yc, axis=-1, keepdims=True) + LN_EPS)
    yn = yn * lng_ref[...] + lnb_ref[...]
    return yn * jax.nn.sigmoid(yn)


def _interleave(major, minor):
    done = 0
    for n, stage in enumerate(major):
        stage()
        upto = (n + 1) * len(minor) // len(major)
        for other in minor[done:upto]:
            other()
        done = upto


def _inproj_prompt_kernel(x_ref, g_ref, w_ref, wdw_ref, bdw_ref, lng_ref, lnb_ref,
                          qe_ref, qo_ref, k_ref, v_ref, oc_ref, kf_ref, vf_ref, ct_ref,
                          cbuf_ref, sh_ref, y_ref):
    i = pl.program_id(1)

    @pl.when(i == 0)
    def _():
        cbuf_ref[0:CONV_LEAD, :] = jnp.zeros((CONV_LEAD, CONV_WIDTH), f32)

    h = _rmsnorm(x_ref[0], g_ref[...]).astype(bf16)
    cbuf_ref[CONV_LEAD:CONV_LEAD + TM_PROJ, :] = _glu(h, w_ref)

    def store(ref):
        def put(cs, val):
            ref[0, :, cs] = val
        return put

    def kv_store(ref, full_ref):
        def put(cs, val):
            ref[0, :, cs] = val.astype(bf16)
            full_ref[0, :, cs] = val
        return put

    proj = _project_stages(h, w_ref, store(qe_ref), store(qo_ref),
                           kv_store(k_ref, kf_ref), kv_store(v_ref, vf_ref))
    shifts, taps = _conv_stages(cbuf_ref, sh_ref, y_ref, TM_PROJ, wdw_ref, bdw_ref)
    _interleave(proj, shifts + taps)
    oc_ref[0] = _ln_swish(y_ref[...], lng_ref, lnb_ref).astype(bf16)
    ct_ref[0] = cbuf_ref[CONV_LEAD + TM_PROJ - CONV_PAST:CONV_LEAD + TM_PROJ, :]

    cbuf_ref[0:CONV_LEAD, :] = cbuf_ref[TM_PROJ:TM_PROJ + CONV_LEAD, :]


def _inproj_prompt(x, g, w, wdw, bdw, lng, lnb):
    B, S, D = x.shape
    assert S % TM_PROJ == 0 and TM_PROJ == W_BAND
    tile = lambda n: pl.BlockSpec((1, TM_PROJ, n), lambda b, i: (b, i, 0))
    per_b = lambda r, n: pl.BlockSpec((1, r, n), lambda b, i: (b, 0, 0))
    act = jax.ShapeDtypeStruct((B, S, ATTN_WIDTH), bf16)
    return pl.pallas_call(
        _inproj_prompt_kernel,
        grid=(B, S // TM_PROJ),
        in_specs=[tile(D)] + [_const_spec(a.shape) for a in (g, w, wdw, bdw, lng, lnb)],
        out_specs=[tile(ATTN_WIDTH)] * 5 + [per_b(W_BAND, ATTN_WIDTH)] * 2
                  + [per_b(CONV_PAST, CONV_WIDTH)],
        out_shape=[act] * 5 + [jax.ShapeDtypeStruct((B, W_BAND, ATTN_WIDTH), f32)] * 2
                  + [jax.ShapeDtypeStruct((B, CONV_PAST, CONV_WIDTH), f32)],
        scratch_shapes=[pltpu.VMEM((CONV_LEAD + TM_PROJ, CONV_WIDTH), f32),
                        pltpu.VMEM((SUBLANES - 1, CONV_LEAD + TM_PROJ - SUBLANES, CONV_WIDTH), f32),
                        pltpu.VMEM((TM_PROJ, CONV_WIDTH), f32)],
        compiler_params=pltpu.CompilerParams(
            dimension_semantics=("arbitrary", "arbitrary"), vmem_limit_bytes=VMEM_LIMIT),
        name="inproj_prompt",
    )(x, g, w, wdw, bdw, lng, lnb)


def _inproj_sample_kernel(x_ref, st_ref, g_ref, w_ref, wdw_ref, bdw_ref, lng_ref, lnb_ref,
                          qe_ref, qo_ref, k_ref, v_ref, oc_ref, nc_ref, cbuf_ref, sh_ref, y_ref):
    B, T, _ = oc_ref.shape
    h = _rmsnorm(x_ref[...], g_ref[...]).astype(bf16)
    u = _glu(h, w_ref)

    def store(ref):
        def put(cs, val):
            ref[:, cs] = val
        return put

    proj = _project_stages(h, w_ref, store(qe_ref), store(qo_ref), store(k_ref), store(v_ref))
    lead = CONV_LEAD - CONV_PAST

    def conv_stage(b):
        cbuf_ref[0:SUBLANES, :] = jnp.zeros((SUBLANES, CONV_WIDTH), f32)
        cbuf_ref[lead:CONV_LEAD, :] = st_ref[b]
        cbuf_ref[CONV_LEAD:CONV_LEAD + T, :] = u[b * T:(b + 1) * T]
        nc_ref[b] = cbuf_ref[CONV_LEAD + T - CONV_PAST:CONV_LEAD + T, :]
        shifts, taps = _conv_stages(cbuf_ref, sh_ref, y_ref, T, wdw_ref, bdw_ref)
        for stage in shifts + taps:
            stage()
        oc_ref[b] = _ln_swish(y_ref[...], lng_ref, lnb_ref).astype(bf16)

    _interleave([functools.partial(conv_stage, b) for b in range(B)], proj)


def _inproj_sample(x, st, g, w, wdw, bdw, lng, lnb):
    N, D = x.shape
    B = st.shape[0]
    T = N // B
    assert T >= CONV_PAST and T % SUBLANES == 0
    act = lambda dt: jax.ShapeDtypeStruct((N, ATTN_WIDTH), dt)
    return pl.pallas_call(
        _inproj_sample_kernel,
        grid=(1,),
        in_specs=[_const_spec(a.shape) for a in (x, st, g, w, wdw, bdw, lng, lnb)],
        out_specs=[_const_spec((N, ATTN_WIDTH))] * 4
                  + [_const_spec((B, T, CONV_WIDTH)), _const_spec((B, CONV_PAST, CONV_WIDTH))],
        out_shape=[act(bf16)] * 2 + [act(f32)] * 2
                  + [jax.ShapeDtypeStruct((B, T, CONV_WIDTH), bf16),
                     jax.ShapeDtypeStruct((B, CONV_PAST, CONV_WIDTH), f32)],
        scratch_shapes=[pltpu.VMEM((CONV_LEAD + T, CONV_WIDTH), f32),
                        pltpu.VMEM((SUBLANES - 1, CONV_LEAD + T - SUBLANES, CONV_WIDTH), f32),
                        pltpu.VMEM((T, CONV_WIDTH), f32)],
        compiler_params=pltpu.CompilerParams(vmem_limit_bytes=VMEM_LIMIT),
        name="inproj_sample",
    )(x, st, g, w, wdw, bdw, lng, lnb)


def _toeplitz(frow, rows):
    return pltpu.roll(jnp.broadcast_to(frow, (rows, F_LEN)), 0, 1, stride=1, stride_axis=0)


def _softmax_parts(s_parts):
    m = functools.reduce(jnp.maximum, [jnp.max(s, axis=-1, keepdims=True) for s in s_parts])
    e_parts = [jnp.exp(s - m) for s in s_parts]
    l = functools.reduce(jnp.add, [jnp.sum(e, axis=-1, keepdims=True) for e in e_parts])
    return e_parts, l


def _mix_prompt_kernel(qe_ref, qo_ref, k0_ref, k1_ref, k2_ref, v0_ref, v1_ref, v2_ref,
                       oc_ref, x_ref, f_ref, wout_ref, o_ref, bias_ref, cat_ref):
    i = pl.program_id(1)
    k_refs = (k0_ref, k1_ref, k2_ref)
    v_refs = (v0_ref, v1_ref, v2_ref)

    @pl.when((pl.program_id(0) == 0) & (i == 0))
    def _():
        for h in range(N_HEADS):
            p, par = divmod(h, 2)
            bias_ref[p, par * SUB:(par + 1) * SUB, :] = _toeplitz(f_ref[h:h + 1, :], SUB)[:, :WIN]

    row = lax.broadcasted_iota(jnp.int32, (2 * SUB, LANES), 0)
    lane = lax.broadcasted_iota(jnp.int32, (2 * SUB, LANES), 1)
    odd_chunk = (row & CHUNK) != 0
    edge_ok = {0: jnp.logical_not(odd_chunk & (lane < CHUNK)),
               WIN // LANES - 1: odd_chunk | (lane < CHUNK)}
    blk_ok = (i >= 2, i >= 1)
    even_head = lax.broadcasted_iota(jnp.int32, (SUB, LANES), 1) < HEAD_DIM

    def pieces(r):
        out = []
        for b in range(N_KBLK):
            lo, hi = max(SUB * r, TQ * b), min(SUB * r + WIN, TQ * (b + 1))
            if lo < hi:
                out.append((b, lo - TQ * b, hi - TQ * b))
        return out

    def scores(r, p):
        rs, ps = slice(r * SUB, (r + 1) * SUB), slice(p * LANES, (p + 1) * LANES)
        q2 = jnp.concatenate([qe_ref[0, rs, ps], qo_ref[0, rs, ps]], axis=0)
        return jnp.concatenate(
            [_dot_nt(q2, k_refs[b][0, a:z, ps]) for b, a, z in pieces(r)], axis=1)

    def finish(r, p, s):
        rs, ps = slice(r * SUB, (r + 1) * SUB), slice(p * LANES, (p + 1) * LANES)
        s = s + bias_ref[p]
        tiles = []
        for t in range(WIN // LANES):
            st = s[:, t * LANES:(t + 1) * LANES]
            b = (SUB * r + LANES * t) // TQ
            ok = edge_ok.get(t)
            if b < N_KBLK - 1:
                ok = blk_ok[b] if ok is None else ok & blk_ok[b]
            tiles.append(st if ok is None else jnp.where(ok, st, NEG_INF))
        (e,), l = _softmax_parts([jnp.concatenate(tiles, axis=1)])
        pb = e.astype(bf16)
        o, c0 = None, 0
        for b, a, z in pieces(r):
            ob = _dot(pb[:, c0:c0 + z - a], v_refs[b][0, a:z, ps])
            o = ob if o is None else o + ob
            c0 += z - a
        o = o / l
        cat_ref[rs, ps] = jnp.where(even_head, o[:SUB], o[SUB:]).astype(bf16)

    units = [(r, p) for r in range(TQ // SUB) for p in range(N_PAIRS)]
    s_next = scores(*units[0])
    for n, unit in enumerate(units):
        s_cur = s_next
        if n + 1 < len(units):
            s_next = scores(*units[n + 1])
        finish(*unit, s_cur)

    o_ref[0] = (x_ref[0] + _dot(cat_ref[...], wout_ref[0:ATTN_WIDTH, :])
                + _dot(oc_ref[0], wout_ref[ATTN_WIDTH:, :]))


def _mix_prompt(qe, qo, k, v, oc, x, fvec, wout):
    B, S, D = x.shape
    assert S % TQ == 0 and W_BAND % TQ == 0 and TQ % SUB == 0 and WIN % LANES == 0
    qspec = pl.BlockSpec((1, TQ, ATTN_WIDTH), lambda b, i: (b, i, 0))
    kspec = lambda back: pl.BlockSpec(
        (1, TQ, ATTN_WIDTH), lambda b, i: (b, jnp.maximum(i - back, 0), 0))
    kspecs = [kspec(N_KBLK - 1 - n) for n in range(N_KBLK)]
    xspec = pl.BlockSpec((1, TQ, D), lambda b, i: (b, i, 0))
    return pl.pallas_call(
        _mix_prompt_kernel,
        grid=(B, S // TQ),
        in_specs=[qspec, qspec] + kspecs + kspecs + [qspec, xspec,
                  _const_spec(fvec.shape), _const_spec(wout.shape)],
        out_specs=xspec,
        out_shape=jax.ShapeDtypeStruct((B, S, D), f32),
        scratch_shapes=[pltpu.VMEM((N_PAIRS, 2 * SUB, WIN), f32),
                        pltpu.VMEM((TQ, ATTN_WIDTH), bf16)],
        compiler_params=pltpu.CompilerParams(
            dimension_semantics=("arbitrary", "arbitrary"), vmem_limit_bytes=VMEM_LIMIT),
        name="mix_prompt",
    )(qe, qo, k, k, k, v, v, v, oc, x, fvec, wout)


def _mix_sample_kernel(qe_ref, qo_ref, kn_ref, vn_ref, ck_ref, cv_ref, oc_ref, x_ref,
                       f_ref, wout_ref, o_ref, nk_ref, nv_ref, bias_ref, cat_ref):
    T = oc_ref.shape[1]
    W = ck_ref.shape[1]

    @pl.when(pl.program_id(0) == 0)
    def _():
        for h in range(N_HEADS):
            bias_ref[h] = _toeplitz(f_ref[h:h + 1, :], T)

    nk_ref[0, 0:W - T, :] = ck_ref[0, T:W, :]
    nk_ref[0, W - T:W, :] = kn_ref[0]
    nv_ref[0, 0:W - T, :] = cv_ref[0, T:W, :]
    nv_ref[0, W - T:W, :] = vn_ref[0]

    lane = lax.broadcasted_iota(jnp.int32, (T, LANES), 1)
    even_head = lane < HEAD_DIM
    for p in range(N_PAIRS):
        ps = slice(p * LANES, (p + 1) * LANES)
        kc = ck_ref[0, :, ps].astype(bf16)
        vc = cv_ref[0, :, ps].astype(bf16)
        kn = kn_ref[0, :, ps].astype(bf16)
        vn = vn_ref[0, :, ps].astype(bf16)
        outs = []
        for par, q_ref in enumerate((qe_ref, qo_ref)):
            q2 = q_ref[0, :, ps]
            h = 2 * p + par
            (ec, en), l = _softmax_parts([_dot_nt(q2, kc) + bias_ref[h, :, 0:W],
                                          _dot_nt(q2, kn) + bias_ref[h, :, W:W + T]])
            o = _dot(ec.astype(bf16), vc) + _dot(en.astype(bf16), vn)
            outs.append(o / l)
        cat_ref[:, ps] = jnp.where(even_head, outs[0], outs[1]).astype(bf16)

    o_ref[0] = (x_ref[0] + _dot(cat_ref[...], wout_ref[0:ATTN_WIDTH, :])
                + _dot(oc_ref[0], wout_ref[ATTN_WIDTH:, :]))


def _mix_sample(qe, qo, kn, vn, ck, cv, oc, x, fvec, wout):
    B, T, D = x.shape
    W = ck.shape[1]
    assert W == W_BAND and W + T <= F_LEN - T
    row = lambda r, n: pl.BlockSpec((1, r, n), lambda b: (b, 0, 0))
    return pl.pallas_call(
        _mix_sample_kernel,
        grid=(B,),
        in_specs=[row(T, ATTN_WIDTH)] * 4 + [row(W, ATTN_WIDTH)] * 2
                 + [row(T, CONV_WIDTH), row(T, D), _const_spec(fvec.shape), _const_spec(wout.shape)],
        out_specs=[row(T, D), row(W, ATTN_WIDTH), row(W, ATTN_WIDTH)],
        out_shape=[jax.ShapeDtypeStruct((B, T, D), f32),
                   jax.ShapeDtypeStruct((B, W, ATTN_WIDTH), f32),
                   jax.ShapeDtypeStruct((B, W, ATTN_WIDTH), f32)],
        scratch_shapes=[pltpu.VMEM((N_HEADS, T, F_LEN), f32),
                        pltpu.VMEM((T, ATTN_WIDTH), bf16)],
        compiler_params=pltpu.CompilerParams(
            dimension_semantics=("arbitrary",), vmem_limit_bytes=VMEM_LIMIT),
        name="mix_sample",
    )(qe, qo, kn, vn, ck, cv, oc, x, fvec, wout)


def _ffn_kernel(x_ref, g_ref, wg_ref, wu_ref, wd_ref, gf_ref, o_ref):
    x = x_ref[...]
    h = _rmsnorm(x, g_ref[...]).astype(bf16)
    half = D_FF // 2
    acc = x
    for c in range(2):
        cs = slice(c * half, (c + 1) * half)
        a = jax.nn.silu(_dot(h, wg_ref[:, cs])) * _dot(h, wu_ref[:, cs])
        acc = acc + _dot(a.astype(bf16), wd_ref[cs, :])
    o_ref[...] = _rmsnorm(acc, gf_ref[...])


def _ffn(x, g, wg, wu, wd, gf, tm):
    N, D = x.shape
    assert N % tm == 0
    tile = pl.BlockSpec((tm, D), lambda i: (i, 0))
    return pl.pallas_call(
        _ffn_kernel,
        grid=(N // tm,),
        in_specs=[tile, _const_spec((1, D)), _const_spec(wg.shape), _const_spec(wu.shape),
                  _const_spec(wd.shape), _const_spec((1, D))],
        out_specs=tile,
        out_shape=jax.ShapeDtypeStruct((N, D), f32),
        compiler_params=pltpu.CompilerParams(
            dimension_semantics=("arbitrary",), vmem_limit_bytes=VMEM_LIMIT),
        name="ffn",
    )(x, g, wg, wu, wd, gf)


def _rel_bias_row(table):
    n_rel = table.shape[1]
    far = W_BAND - REL_CLIP
    assert n_rel == 2 * REL_CLIP + 1 and far + n_rel + SUB <= F_LEN
    rep = lambda col, n: jnp.broadcast_to(table[:, col:col + 1], (table.shape[0], n))
    return jnp.concatenate([rep(n_rel - 1, far), table[:, ::-1],
                            rep(0, F_LEN - SUB - far - n_rel), rep(n_rel - 1, SUB)], axis=1)


def kernel(x_prompt, x_sample, cache_k, cache_v, state_conv, g_mix, w_in, rel_table, w_dw, b_dw,
           ln_g, ln_b, w_out, g_ffn, w_gate, w_up, w_down, g_final):
    assert g_mix.shape[0] == 1, "single-layer trunk: the final RMSNorm is fused into the FFN kernel"
    B, S, D = x_prompt.shape
    Bs, T, _ = x_sample.shape
    W = cache_k.shape[2]
    gm, gff, gf = g_mix[0].reshape(1, D), g_ffn[0].reshape(1, D), g_final.reshape(1, D)
    win, wout = w_in[0].astype(bf16), w_out[0].astype(bf16)
    wg, wu, wd = w_gate[0].astype(bf16), w_up[0].astype(bf16), w_down[0].astype(bf16)
    conv_params = (w_dw[0],) + tuple(a[0].reshape(1, CONV_WIDTH) for a in (b_dw, ln_g, ln_b))
    heads = lambda a: a.reshape(1, a.shape[0], a.shape[1], N_HEADS, HEAD_DIM)
    fvec = _rel_bias_row(rel_table[0])

    qe, qo, k, v, oc, kf, vf, ct = _inproj_prompt(x_prompt, gm, win, *conv_params)
    x1 = _mix_prompt(qe, qo, k, v, oc, x_prompt, fvec, wout)
    y_prompt = _ffn(x1.reshape(B * S, D), gff, wg, wu, wd, gf, TM_FFN).reshape(B, S, D)

    qe, qo, kn, vn, oc, nc = _inproj_sample(
        x_sample.reshape(Bs * T, D), state_conv[0], gm, win, *conv_params)
    r3 = lambda a: a.reshape(Bs, T, ATTN_WIDTH)
    x1, nk, nv = _mix_sample(
        r3(qe), r3(qo), r3(kn), r3(vn),
        cache_k[0].reshape(Bs, W, ATTN_WIDTH), cache_v[0].reshape(Bs, W, ATTN_WIDTH),
        oc, x_sample, fvec, wout)
    y_sample = _ffn(x1.reshape(Bs * T, D), gff, wg, wu, wd, gf, Bs * T).reshape(Bs, T, D)

    return (y_prompt, y_sample, heads(kf), heads(vf), ct[None],
            heads(nk), heads(nv), nc[None])
```

```python
import functools

import jax
import jax.numpy as jnp
from jax import lax
from jax.experimental import pallas as pl
from jax.experimental.pallas import tpu as pltpu

D_MODEL = 1024
CHUNK = 64
LEFT_CHUNKS = 8
W_BAND = LEFT_CHUNKS * CHUNK
ATTN_WIDTH = 512
N_HEADS = 8
HEAD_DIM = 64
CONV_WIDTH = 512
CONV_KERNEL = 31
CONV_PAST = CONV_KERNEL - 1
REL_CLIP = 128
D_FF = 2816
RMS_EPS = 1e-6
LN_EPS = 1e-5
NEG_INF = -1e30
SCALE = HEAD_DIM ** -0.5

LANES = 128
SUBLANES = 8
N_PAIRS = ATTN_WIDTH // LANES
TM_PROJ = 512
PROJ_COLS = 256
TQ = 256
N_KBLK = W_BAND // TQ + 1
SUB = 2 * CHUNK
WIN = W_BAND + SUB
F_LEN = 1024
TM_FFN = 512
MXU_TILE = 256
FFN_COLS = 4 * MXU_TILE
CONV_ROWS = 64
CONV_LEAD = 32
VMEM_LIMIT = 56 * 1024 * 1024

f32 = jnp.float32
bf16 = jnp.bfloat16


def _rmsnorm(x, g):
    return (x * lax.rsqrt(jnp.mean(x * x, axis=-1, keepdims=True) + RMS_EPS)) * g


def _dot(a, b):
    return jnp.dot(a, b, preferred_element_type=f32)


def _dot_nt(a, b):
    return lax.dot_general(a, b, (((1,), (1,)), ((), ())), preferred_element_type=f32)


def _const_spec(shape):
    nd = len(shape)
    return pl.BlockSpec(shape, lambda *_: (0,) * nd, pipeline_mode=pl.Buffered(1))


def _project_stages(h_ref, w_ref, qe_store, qo_store, k_store, v_store):
    def issue(base, cs):
        return _dot(h_ref[...], w_ref[:, base + cs.start:base + cs.stop])

    def q_finish(cs, z):
        q = z * SCALE
        lane = lax.broadcasted_iota(jnp.int32, q.shape, 1)
        even = (lane & HEAD_DIM) == 0
        qe_store(cs, jnp.where(even, q, 0.0).astype(bf16))
        qo_store(cs, jnp.where(even, 0.0, q).astype(bf16))

    stages = []
    for base, finish in ((0, q_finish), (ATTN_WIDTH, k_store), (2 * ATTN_WIDTH, v_store)):
        for c0 in range(0, ATTN_WIDTH, PROJ_COLS):
            cs = slice(c0, c0 + PROJ_COLS)
            stages.append((functools.partial(issue, base, cs), functools.partial(finish, cs)))
    return stages


def _glu(h_ref, w_ref):
    base = 3 * ATTN_WIDTH
    a = _dot(h_ref[...], w_ref[:, base:base + CONV_WIDTH])
    g = _dot(h_ref[...], w_ref[:, base + CONV_WIDTH:base + 2 * CONV_WIDTH])
    return a * jax.nn.sigmoid(g)


def _conv_stages(cbuf_ref, sh_ref, y_ref, rows, wdw_ref, bdw_ref):
    lead = CONV_LEAD - CONV_PAST
    span = rows + CONV_LEAD - SUBLANES
    rb = min(CONV_ROWS, rows)

    def shift_stage(m):
        sh_ref[m - 1, 0:span, :] = cbuf_ref[m:m + span, :]

    def tap_stage(cs, r0):
        acc = jnp.zeros((rb, LANES), f32)
        for j in range(CONV_KERNEL):
            a, m = divmod(j + lead, SUBLANES)
            src = cbuf_ref if m == 0 else sh_ref.at[m - 1]
            lo = r0 + SUBLANES * a
            acc = acc + wdw_ref[j:j + 1, cs] * src[lo:lo + rb, cs]
        y_ref[r0:r0 + rb, cs] = acc + bdw_ref[:, cs]

    shifts = [functools.partial(shift_stage, m) for m in range(1, SUBLANES)]
    taps = [functools.partial(tap_stage, slice(c * LANES, (c + 1) * LANES), r0)
            for c in range(CONV_WIDTH // LANES) for r0 in range(0, rows, rb)]
    return shifts, taps


def _ln_swish(y, lng_ref, lnb_ref):
    mu = jnp.mean(y, axis=-1, keepdims=True)
    yc = y - mu
    yn = yc * lax.rsqrt(jnp.mean(yc * yc, axis=-1, keepdims=True) + LN_EPS)
    yn = yn * lng_ref[...] + lnb_ref[...]
    return yn * jax.nn.sigmoid(yn)


def _pipeline(mxu_stages, valu_stages):
    pending = mxu_stages[0][0]()
    done = 0
    for n, (_, consume) in enumerate(mxu_stages):
        cur = pending
        if n + 1 < len(mxu_stages):
            pending = mxu_stages[n + 1][0]()
        upto = (n + 1) * len(valu_stages) // len(mxu_stages)
        for stage in valu_stages[done:upto]:
            stage()
        done = upto
        consume(cur)


def _inproj_prompt_kernel(x_ref, g_ref, w_ref, wdw_ref, bdw_ref, lng_ref, lnb_ref,
                          qe_ref, qo_ref, k_ref, v_ref, oc_ref, kf_ref, vf_ref, ct_ref,
                          h_ref, cbuf_ref, sh_ref, y_ref):
    i = pl.program_id(1)

    @pl.when(i == 0)
    def _():
        cbuf_ref[0:CONV_LEAD, :] = jnp.zeros((CONV_LEAD, CONV_WIDTH), f32)

    h_ref[...] = _rmsnorm(x_ref[0], g_ref[...]).astype(bf16)
    cbuf_ref[CONV_LEAD:CONV_LEAD + TM_PROJ, :] = _glu(h_ref, w_ref)

    def store(ref):
        def put(cs, val):
            ref[0, :, cs] = val
        return put

    def kv_store(ref, full_ref):
        def put(cs, val):
            ref[0, :, cs] = val.astype(bf16)
            full_ref[0, :, cs] = val
        return put

    proj = _project_stages(h_ref, w_ref, store(qe_ref), store(qo_ref),
                           kv_store(k_ref, kf_ref), kv_store(v_ref, vf_ref))
    shifts, taps = _conv_stages(cbuf_ref, sh_ref, y_ref, TM_PROJ, wdw_ref, bdw_ref)
    _pipeline(proj, shifts + taps)
    oc_ref[0] = _ln_swish(y_ref[...], lng_ref, lnb_ref).astype(bf16)
    ct_ref[0] = cbuf_ref[CONV_LEAD + TM_PROJ - CONV_PAST:CONV_LEAD + TM_PROJ, :]

    cbuf_ref[0:CONV_LEAD, :] = cbuf_ref[TM_PROJ:TM_PROJ + CONV_LEAD, :]


def _inproj_prompt(x, g, w, wdw, bdw, lng, lnb):
    B, S, D = x.shape
    assert S % TM_PROJ == 0 and TM_PROJ == W_BAND
    tile = lambda n: pl.BlockSpec((1, TM_PROJ, n), lambda b, i: (b, i, 0))
    per_b = lambda r, n: pl.BlockSpec((1, r, n), lambda b, i: (b, 0, 0))
    act = jax.ShapeDtypeStruct((B, S, ATTN_WIDTH), bf16)
    return pl.pallas_call(
        _inproj_prompt_kernel,
        grid=(B, S // TM_PROJ),
        in_specs=[tile(D)] + [_const_spec(a.shape) for a in (g, w, wdw, bdw, lng, lnb)],
        out_specs=[tile(ATTN_WIDTH)] * 5 + [per_b(W_BAND, ATTN_WIDTH)] * 2
                  + [per_b(CONV_PAST, CONV_WIDTH)],
        out_shape=[act] * 5 + [jax.ShapeDtypeStruct((B, W_BAND, ATTN_WIDTH), f32)] * 2
                  + [jax.ShapeDtypeStruct((B, CONV_PAST, CONV_WIDTH), f32)],
        scratch_shapes=[pltpu.VMEM((TM_PROJ, D), bf16),
                        pltpu.VMEM((CONV_LEAD + TM_PROJ, CONV_WIDTH), f32),
                        pltpu.VMEM((SUBLANES - 1, CONV_LEAD + TM_PROJ - SUBLANES, CONV_WIDTH), f32),
                        pltpu.VMEM((TM_PROJ, CONV_WIDTH), f32)],
        compiler_params=pltpu.CompilerParams(
            dimension_semantics=("arbitrary", "arbitrary"), vmem_limit_bytes=VMEM_LIMIT),
        name="inproj_prompt",
    )(x, g, w, wdw, bdw, lng, lnb)


def _inproj_sample_kernel(x_ref, st_ref, g_ref, w_ref, wdw_ref, bdw_ref, lng_ref, lnb_ref,
                          qe_ref, qo_ref, k_ref, v_ref, oc_ref, nc_ref,
                          h_ref, cbuf_ref, sh_ref, y_ref):
    B, T, _ = oc_ref.shape
    h_ref[...] = _rmsnorm(x_ref[...], g_ref[...]).astype(bf16)
    u = _glu(h_ref, w_ref)

    def store(ref):
        def put(cs, val):
            ref[:, cs] = val
        return put

    proj = _project_stages(h_ref, w_ref, store(qe_ref), store(qo_ref), store(k_ref), store(v_ref))
    lead = CONV_LEAD - CONV_PAST

    def conv_stage(b):
        cbuf_ref[0:SUBLANES, :] = jnp.zeros((SUBLANES, CONV_WIDTH), f32)
        cbuf_ref[lead:CONV_LEAD, :] = st_ref[b]
        cbuf_ref[CONV_LEAD:CONV_LEAD + T, :] = u[b * T:(b + 1) * T]
        nc_ref[b] = cbuf_ref[CONV_LEAD + T - CONV_PAST:CONV_LEAD + T, :]
        shifts, taps = _conv_stages(cbuf_ref, sh_ref, y_ref, T, wdw_ref, bdw_ref)
        for stage in shifts + taps:
            stage()
        oc_ref[b] = _ln_swish(y_ref[...], lng_ref, lnb_ref).astype(bf16)

    _pipeline(proj, [functools.partial(conv_stage, b) for b in range(B)])


def _inproj_sample(x, st, g, w, wdw, bdw, lng, lnb):
    N, D = x.shape
    B = st.shape[0]
    T = N // B
    assert T >= CONV_PAST and T % SUBLANES == 0
    act = lambda dt: jax.ShapeDtypeStruct((N, ATTN_WIDTH), dt)
    return pl.pallas_call(
        _inproj_sample_kernel,
        grid=(1,),
        in_specs=[_const_spec(a.shape) for a in (x, st, g, w, wdw, bdw, lng, lnb)],
        out_specs=[_const_spec((N, ATTN_WIDTH))] * 4
                  + [_const_spec((B, T, CONV_WIDTH)), _const_spec((B, CONV_PAST, CONV_WIDTH))],
        out_shape=[act(bf16)] * 2 + [act(f32)] * 2
                  + [jax.ShapeDtypeStruct((B, T, CONV_WIDTH), bf16),
                     jax.ShapeDtypeStruct((B, CONV_PAST, CONV_WIDTH), f32)],
        scratch_shapes=[pltpu.VMEM((N, D), bf16),
                        pltpu.VMEM((CONV_LEAD + T, CONV_WIDTH), f32),
                        pltpu.VMEM((SUBLANES - 1, CONV_LEAD + T - SUBLANES, CONV_WIDTH), f32),
                        pltpu.VMEM((T, CONV_WIDTH), f32)],
        compiler_params=pltpu.CompilerParams(vmem_limit_bytes=VMEM_LIMIT),
        name="inproj_sample",
    )(x, st, g, w, wdw, bdw, lng, lnb)


def _toeplitz(frow, rows):
    return pltpu.roll(jnp.broadcast_to(frow, (rows, F_LEN)), 0, 1, stride=1, stride_axis=0)


def _softmax_parts(s_parts):
    m = functools.reduce(jnp.maximum, [jnp.max(s, axis=-1, keepdims=True) for s in s_parts])
    e_parts = [jnp.exp(s - m) for s in s_parts]
    l = functools.reduce(jnp.add, [jnp.sum(e, axis=-1, keepdims=True) for e in e_parts])
    return e_parts, l


def _mix_prompt_kernel(qe_ref, qo_ref, k0_ref, k1_ref, k2_ref, v0_ref, v1_ref, v2_ref,
                       oc_ref, x_ref, f_ref, wout_ref, o_ref, bias_ref, cat_ref):
    i = pl.program_id(1)
    k_refs = (k0_ref, k1_ref, k2_ref)
    v_refs = (v0_ref, v1_ref, v2_ref)

    @pl.when((pl.program_id(0) == 0) & (i == 0))
    def _():
        for h in range(N_HEADS):
            p, par = divmod(h, 2)
            bias_ref[p, par * SUB:(par + 1) * SUB, :] = _toeplitz(f_ref[h:h + 1, :], SUB)[:, :WIN]

    row = lax.broadcasted_iota(jnp.int32, (2 * SUB, LANES), 0)
    lane = lax.broadcasted_iota(jnp.int32, (2 * SUB, LANES), 1)
    odd_chunk = (row & CHUNK) != 0
    edge_ok = {0: jnp.logical_not(odd_chunk & (lane < CHUNK)),
               WIN // LANES - 1: odd_chunk | (lane < CHUNK)}
    blk_ok = (i >= 2, i >= 1)
    even_head = lax.broadcasted_iota(jnp.int32, (SUB, LANES), 1) < HEAD_DIM

    def pieces(r):
        out = []
        for b in range(N_KBLK):
            lo, hi = max(SUB * r, TQ * b), min(SUB * r + WIN, TQ * (b + 1))
            if lo < hi:
                out.append((b, lo - TQ * b, hi - TQ * b))
        return out

    def scores(r, p):
        rs, ps = slice(r * SUB, (r + 1) * SUB), slice(p * LANES, (p + 1) * LANES)
        q2 = jnp.concatenate([qe_ref[0, rs, ps], qo_ref[0, rs, ps]], axis=0)
        return jnp.concatenate(
            [_dot_nt(q2, k_refs[b][0, a:z, ps]) for b, a, z in pieces(r)], axis=1)

    def finish(r, p, s):
        rs, ps = slice(r * SUB, (r + 1) * SUB), slice(p * LANES, (p + 1) * LANES)
        s = s + bias_ref[p]
        tiles = []
        for t in range(WIN // LANES):
            st = s[:, t * LANES:(t + 1) * LANES]
            b = (SUB * r + LANES * t) // TQ
            ok = edge_ok.get(t)
            if b < N_KBLK - 1:
                ok = blk_ok[b] if ok is None else ok & blk_ok[b]
            tiles.append(st if ok is None else jnp.where(ok, st, NEG_INF))
        (e,), l = _softmax_parts([jnp.concatenate(tiles, axis=1)])
        pb = e.astype(bf16)
        o, c0 = None, 0
        for b, a, z in pieces(r):
            ob = _dot(pb[:, c0:c0 + z - a], v_refs[b][0, a:z, ps])
            o = ob if o is None else o + ob
            c0 += z - a
        o = o / l
        cat_ref[rs, ps] = jnp.where(even_head, o[:SUB], o[SUB:]).astype(bf16)

    units = [(r, p) for r in range(TQ // SUB) for p in range(N_PAIRS)]
    s_next = scores(*units[0])
    for n, unit in enumerate(units):
        s_cur = s_next
        if n + 1 < len(units):
            s_next = scores(*units[n + 1])
        finish(*unit, s_cur)

    o_ref[0] = (x_ref[0] + _dot(cat_ref[...], wout_ref[0:ATTN_WIDTH, :])
                + _dot(oc_ref[0], wout_ref[ATTN_WIDTH:, :]))


def _mix_prompt(qe, qo, k, v, oc, x, fvec, wout):
    B, S, D = x.shape
    assert S % TQ == 0 and W_BAND % TQ == 0 and TQ % SUB == 0 and WIN % LANES == 0
    qspec = pl.BlockSpec((1, TQ, ATTN_WIDTH), lambda b, i: (b, i, 0))
    kspec = lambda back: pl.BlockSpec(
        (1, TQ, ATTN_WIDTH), lambda b, i: (b, jnp.maximum(i - back, 0), 0))
    kspecs = [kspec(N_KBLK - 1 - n) for n in range(N_KBLK)]
    xspec = pl.BlockSpec((1, TQ, D), lambda b, i: (b, i, 0))
    return pl.pallas_call(
        _mix_prompt_kernel,
        grid=(B, S // TQ),
        in_specs=[qspec, qspec] + kspecs + kspecs + [qspec, xspec,
                  _const_spec(fvec.shape), _const_spec(wout.shape)],
        out_specs=xspec,
        out_shape=jax.ShapeDtypeStruct((B, S, D), f32),
        scratch_shapes=[pltpu.VMEM((N_PAIRS, 2 * SUB, WIN), f32),
                        pltpu.VMEM((TQ, ATTN_WIDTH), bf16)],
        compiler_params=pltpu.CompilerParams(
            dimension_semantics=("arbitrary", "arbitrary"), vmem_limit_bytes=VMEM_LIMIT),
        name="mix_prompt",
    )(qe, qo, k, k, k, v, v, v, oc, x, fvec, wout)


def _mix_sample_kernel(qe_ref, qo_ref, kn_ref, vn_ref, ck_ref, cv_ref, oc_ref, x_ref,
                       f_ref, wout_ref, o_ref, nk_ref, nv_ref, bias_ref, cat_ref):
    T = oc_ref.shape[1]
    W = ck_ref.shape[1]

    @pl.when(pl.program_id(0) == 0)
    def _():
        for h in range(N_HEADS):
            p, par = divmod(h, 2)
            bias_ref[p, par * T:(par + 1) * T, :] = _toeplitz(f_ref[h:h + 1, :], T)

    nk_ref[0, 0:W - T, :] = ck_ref[0, T:W, :]
    nk_ref[0, W - T:W, :] = kn_ref[0]
    nv_ref[0, 0:W - T, :] = cv_ref[0, T:W, :]
    nv_ref[0, W - T:W, :] = vn_ref[0]

    lane = lax.broadcasted_iota(jnp.int32, (T, LANES), 1)
    even_head = lane < HEAD_DIM
    for p in range(N_PAIRS):
        ps = slice(p * LANES, (p + 1) * LANES)
        kc = ck_ref[0, :, ps].astype(bf16)
        vc = cv_ref[0, :, ps].astype(bf16)
        kn = kn_ref[0, :, ps].astype(bf16)
        vn = vn_ref[0, :, ps].astype(bf16)
        q2 = jnp.concatenate([qe_ref[0, :, ps], qo_ref[0, :, ps]], axis=0)
        (ec, en), l = _softmax_parts([_dot_nt(q2, kc) + bias_ref[p, :, 0:W],
                                      _dot_nt(q2, kn) + bias_ref[p, :, W:W + T]])
        o = (_dot(ec.astype(bf16), vc) + _dot(en.astype(bf16), vn)) / l
        cat_ref[:, ps] = jnp.where(even_head, o[:T], o[T:]).astype(bf16)

    o_ref[0] = (x_ref[0] + _dot(cat_ref[...], wout_ref[0:ATTN_WIDTH, :])
                + _dot(oc_ref[0], wout_ref[ATTN_WIDTH:, :]))


def _mix_sample(qe, qo, kn, vn, ck, cv, oc, x, fvec, wout):
    B, T, D = x.shape
    W = ck.shape[1]
    assert W == W_BAND and W + T <= F_LEN - T
    row = lambda r, n: pl.BlockSpec((1, r, n), lambda b: (b, 0, 0))
    return pl.pallas_call(
        _mix_sample_kernel,
        grid=(B,),
        in_specs=[row(T, ATTN_WIDTH)] * 4 + [row(W, ATTN_WIDTH)] * 2
                 + [row(T, CONV_WIDTH), row(T, D), _const_spec(fvec.shape), _const_spec(wout.shape)],
        out_specs=[row(T, D), row(W, ATTN_WIDTH), row(W, ATTN_WIDTH)],
        out_shape=[jax.ShapeDtypeStruct((B, T, D), f32),
                   jax.ShapeDtypeStruct((B, W, ATTN_WIDTH), f32),
                   jax.ShapeDtypeStruct((B, W, ATTN_WIDTH), f32)],
        scratch_shapes=[pltpu.VMEM((N_PAIRS, 2 * T, F_LEN), f32),
                        pltpu.VMEM((T, ATTN_WIDTH), bf16)],
        compiler_params=pltpu.CompilerParams(
            dimension_semantics=("arbitrary",), vmem_limit_bytes=VMEM_LIMIT),
        name="mix_sample",
    )(qe, qo, kn, vn, ck, cv, oc, x, fvec, wout)


def _ffn_kernel(x_ref, g_ref, wg_ref, wu_ref, wd_ref, gf_ref, o_ref):
    x = x_ref[...]
    h = _rmsnorm(x, g_ref[...]).astype(bf16)
    acc = x
    for c0 in range(0, D_FF, FFN_COLS):
        cs = slice(c0, min(c0 + FFN_COLS, D_FF))
        a = jax.nn.silu(_dot(h, wg_ref[:, cs])) * _dot(h, wu_ref[:, cs])
        acc = acc + _dot(a.astype(bf16), wd_ref[cs, :])
    o_ref[...] = _rmsnorm(acc, gf_ref[...])


def _ffn(x, g, wg, wu, wd, gf, tm):
    N, D = x.shape
    assert N % tm == 0
    tile = pl.BlockSpec((tm, D), lambda i: (i, 0))
    return pl.pallas_call(
        _ffn_kernel,
        grid=(N // tm,),
        in_specs=[tile, _const_spec((1, D)), _const_spec(wg.shape), _const_spec(wu.shape),
                  _const_spec(wd.shape), _const_spec((1, D))],
        out_specs=tile,
        out_shape=jax.ShapeDtypeStruct((N, D), f32),
        compiler_params=pltpu.CompilerParams(
            dimension_semantics=("arbitrary",), vmem_limit_bytes=VMEM_LIMIT),
        name="ffn",
    )(x, g, wg, wu, wd, gf)


def _rel_bias_row(table):
    n_rel = table.shape[1]
    far = W_BAND - REL_CLIP
    assert n_rel == 2 * REL_CLIP + 1 and far + n_rel + SUB <= F_LEN
    rep = lambda col, n: jnp.broadcast_to(table[:, col:col + 1], (table.shape[0], n))
    return jnp.concatenate([rep(n_rel - 1, far), table[:, ::-1],
                            rep(0, F_LEN - SUB - far - n_rel), rep(n_rel - 1, SUB)], axis=1)


def kernel(x_prompt, x_sample, cache_k, cache_v, state_conv, g_mix, w_in, rel_table, w_dw, b_dw,
           ln_g, ln_b, w_out, g_ffn, w_gate, w_up, w_down, g_final):
    assert g_mix.shape[0] == 1, "single-layer trunk: the final RMSNorm is fused into the FFN kernel"
    B, S, D = x_prompt.shape
    Bs, T, _ = x_sample.shape
    W = cache_k.shape[2]
    gm, gff, gf = g_mix[0].reshape(1, D), g_ffn[0].reshape(1, D), g_final.reshape(1, D)
    win, wout = w_in[0].astype(bf16), w_out[0].astype(bf16)
    wg, wu, wd = w_gate[0].astype(bf16), w_up[0].astype(bf16), w_down[0].astype(bf16)
    conv_params = (w_dw[0],) + tuple(a[0].reshape(1, CONV_WIDTH) for a in (b_dw, ln_g, ln_b))
    heads = lambda a: a.reshape(1, a.shape[0], a.shape[1], N_HEADS, HEAD_DIM)
    fvec = _rel_bias_row(rel_table[0])

    qe, qo, k, v, oc, kf, vf, ct = _inproj_prompt(x_prompt, gm, win, *conv_params)
    x1 = _mix_prompt(qe, qo, k, v, oc, x_prompt, fvec, wout)
    y_prompt = _ffn(x1.reshape(B * S, D), gff, wg, wu, wd, gf, TM_FFN).reshape(B, S, D)

    qe, qo, kn, vn, oc, nc = _inproj_sample(
        x_sample.reshape(Bs * T, D), state_conv[0], gm, win, *conv_params)
    r3 = lambda a: a.reshape(Bs, T, ATTN_WIDTH)
    x1, nk, nv = _mix_sample(
        r3(qe), r3(qo), r3(kn), r3(vn),
        cache_k[0].reshape(Bs, W, ATTN_WIDTH), cache_v[0].reshape(Bs, W, ATTN_WIDTH),
        oc, x_sample, fvec, wout)
    y_sample = _ffn(x1.reshape(Bs * T, D), gff, wg, wu, wd, gf, Bs * T).reshape(Bs, T, D)

    return (y_prompt, y_sample, heads(kf), heads(vf), ct[None],
            heads(nk), heads(nv), nc[None])
```

```python
import functools

import jax
import jax.numpy as jnp
from jax import lax
from jax.experimental import pallas as pl
from jax.experimental.pallas import tpu as pltpu

D_MODEL = 1024
CHUNK = 64
LEFT_CHUNKS = 8
W_BAND = LEFT_CHUNKS * CHUNK
ATTN_WIDTH = 512
N_HEADS = 8
HEAD_DIM = 64
CONV_WIDTH = 512
CONV_KERNEL = 31
CONV_PAST = CONV_KERNEL - 1
REL_CLIP = 128
D_FF = 2816
RMS_EPS = 1e-6
LN_EPS = 1e-5
NEG_INF = -1e30
SCALE = HEAD_DIM ** -0.5

LANES = 128
SUBLANES = 8
N_PAIRS = ATTN_WIDTH // LANES
TM_PROJ = 512
PROJ_COLS = 256
TQ = 512
N_KBLK = W_BAND // TQ + 1
SUB = 2 * CHUNK
WIN = W_BAND + SUB
F_LEN = 1024
TM_FFN = 512
MXU_TILE = 256
FFN_COLS = 4 * MXU_TILE
CONV_ROWS = 64
CONV_LEAD = 32
VMEM_LIMIT = 56 * 1024 * 1024

f32 = jnp.float32
bf16 = jnp.bfloat16


def _rmsnorm(x, g):
    return (x * lax.rsqrt(jnp.mean(x * x, axis=-1, keepdims=True) + RMS_EPS)) * g


def _dot(a, b):
    return jnp.dot(a, b, preferred_element_type=f32)


def _dot_nt(a, b):
    return lax.dot_general(a, b, (((1,), (1,)), ((), ())), preferred_element_type=f32)


def _const_spec(shape):
    nd = len(shape)
    return pl.BlockSpec(shape, lambda *_: (0,) * nd, pipeline_mode=pl.Buffered(1))


def _project_stages(h_ref, w_ref, qe_store, qo_store, k_store, v_store):
    def issue(base, cs):
        return _dot(h_ref[...], w_ref[:, base + cs.start:base + cs.stop])

    def q_finish(cs, z):
        q = z * SCALE
        lane = lax.broadcasted_iota(jnp.int32, q.shape, 1)
        even = (lane & HEAD_DIM) == 0
        qe_store(cs, jnp.where(even, q, 0.0).astype(bf16))
        qo_store(cs, jnp.where(even, 0.0, q).astype(bf16))

    stages = []
    for base, finish in ((0, q_finish), (ATTN_WIDTH, k_store), (2 * ATTN_WIDTH, v_store)):
        for c0 in range(0, ATTN_WIDTH, PROJ_COLS):
            cs = slice(c0, c0 + PROJ_COLS)
            stages.append((functools.partial(issue, base, cs), functools.partial(finish, cs)))
    return stages


def _cast_weight(w32_ref, w_ref):
    for c0 in range(0, w_ref.shape[1], MXU_TILE):
        w_ref[:, c0:c0 + MXU_TILE] = w32_ref[:, c0:c0 + MXU_TILE].astype(bf16)


def _glu(h_ref, w_ref):
    base = 3 * ATTN_WIDTH
    a = _dot(h_ref[...], w_ref[:, base:base + CONV_WIDTH])
    g = _dot(h_ref[...], w_ref[:, base + CONV_WIDTH:base + 2 * CONV_WIDTH])
    return a * jax.nn.sigmoid(g)


def _conv_stages(cbuf_ref, sh_ref, y_ref, rows, wdw_ref, bdw_ref):
    lead = CONV_LEAD - CONV_PAST
    span = rows + CONV_LEAD - SUBLANES
    rb = min(CONV_ROWS, rows)

    def shift_stage(m):
        sh_ref[m - 1, 0:span, :] = cbuf_ref[m:m + span, :]

    def tap_stage(cs, r0):
        acc = jnp.zeros((rb, LANES), f32)
        for j in range(CONV_KERNEL):
            a, m = divmod(j + lead, SUBLANES)
            src = cbuf_ref if m == 0 else sh_ref.at[m - 1]
            lo = r0 + SUBLANES * a
            acc = acc + wdw_ref[j:j + 1, cs] * src[lo:lo + rb, cs]
        y_ref[r0:r0 + rb, cs] = acc + bdw_ref[:, cs]

    shifts = [functools.partial(shift_stage, m) for m in range(1, SUBLANES)]
    taps = [functools.partial(tap_stage, slice(c * LANES, (c + 1) * LANES), r0)
            for c in range(CONV_WIDTH // LANES) for r0 in range(0, rows, rb)]
    return shifts, taps


def _ln_swish(y, lng_ref, lnb_ref):
    mu = jnp.mean(y, axis=-1, keepdims=True)
    yc = y - mu
    yn = yc * lax.rsqrt(jnp.mean(yc * yc, axis=-1, keepdims=True) + LN_EPS)
    yn = yn * lng_ref[...] + lnb_ref[...]
    return yn * jax.nn.sigmoid(yn)


def _pipeline(mxu_stages, valu_stages):
    pending = mxu_stages[0][0]()
    done = 0
    for n, (_, consume) in enumerate(mxu_stages):
        cur = pending
        if n + 1 < len(mxu_stages):
            pending = mxu_stages[n + 1][0]()
        upto = (n + 1) * len(valu_stages) // len(mxu_stages)
        for stage in valu_stages[done:upto]:
            stage()
        done = upto
        consume(cur)


def _inproj_prompt_kernel(x_ref, g_ref, w32_ref, wdw_ref, bdw_ref, lng_ref, lnb_ref,
                          qe_ref, qo_ref, k_ref, v_ref, oc_ref, kf_ref, vf_ref, ct_ref,
                          w_ref, h_ref, cbuf_ref, sh_ref, y_ref):
    i = pl.program_id(1)

    @pl.when((pl.program_id(0) == 0) & (i == 0))
    def _():
        _cast_weight(w32_ref, w_ref)

    @pl.when(i == 0)
    def _():
        cbuf_ref[0:CONV_LEAD, :] = jnp.zeros((CONV_LEAD, CONV_WIDTH), f32)

    h_ref[...] = _rmsnorm(x_ref[0], g_ref[...]).astype(bf16)
    cbuf_ref[CONV_LEAD:CONV_LEAD + TM_PROJ, :] = _glu(h_ref, w_ref)

    def store(ref):
        def put(cs, val):
            ref[0, :, cs] = val
        return put

    def kv_store(ref, full_ref):
        def put(cs, val):
            ref[0, :, cs] = val.astype(bf16)
            full_ref[0, :, cs] = val
        return put

    proj = _project_stages(h_ref, w_ref, store(qe_ref), store(qo_ref),
                           kv_store(k_ref, kf_ref), kv_store(v_ref, vf_ref))
    shifts, taps = _conv_stages(cbuf_ref, sh_ref, y_ref, TM_PROJ, wdw_ref, bdw_ref)
    _pipeline(proj, shifts + taps)
    oc_ref[0] = _ln_swish(y_ref[...], lng_ref, lnb_ref).astype(bf16)
    ct_ref[0] = cbuf_ref[CONV_LEAD + TM_PROJ - CONV_PAST:CONV_LEAD + TM_PROJ, :]

    cbuf_ref[0:CONV_LEAD, :] = cbuf_ref[TM_PROJ:TM_PROJ + CONV_LEAD, :]


def _inproj_prompt(x, g, w, wdw, bdw, lng, lnb):
    B, S, D = x.shape
    assert S % TM_PROJ == 0 and TM_PROJ == W_BAND
    tile = lambda n: pl.BlockSpec((1, TM_PROJ, n), lambda b, i: (b, i, 0))
    per_b = lambda r, n: pl.BlockSpec((1, r, n), lambda b, i: (b, 0, 0))
    act = jax.ShapeDtypeStruct((B, S, ATTN_WIDTH), bf16)
    return pl.pallas_call(
        _inproj_prompt_kernel,
        grid=(B, S // TM_PROJ),
        in_specs=[tile(D)] + [_const_spec(a.shape) for a in (g, w, wdw, bdw, lng, lnb)],
        out_specs=[tile(ATTN_WIDTH)] * 5 + [per_b(W_BAND, ATTN_WIDTH)] * 2
                  + [per_b(CONV_PAST, CONV_WIDTH)],
        out_shape=[act] * 5 + [jax.ShapeDtypeStruct((B, W_BAND, ATTN_WIDTH), f32)] * 2
                  + [jax.ShapeDtypeStruct((B, CONV_PAST, CONV_WIDTH), f32)],
        scratch_shapes=[pltpu.VMEM(w.shape, bf16),
                        pltpu.VMEM((TM_PROJ, D), bf16),
                        pltpu.VMEM((CONV_LEAD + TM_PROJ, CONV_WIDTH), f32),
                        pltpu.VMEM((SUBLANES - 1, CONV_LEAD + TM_PROJ - SUBLANES, CONV_WIDTH), f32),
                        pltpu.VMEM((TM_PROJ, CONV_WIDTH), f32)],
        compiler_params=pltpu.CompilerParams(
            dimension_semantics=("arbitrary", "arbitrary"), vmem_limit_bytes=VMEM_LIMIT),
        name="inproj_prompt",
    )(x, g, w, wdw, bdw, lng, lnb)


def _inproj_sample_kernel(x_ref, st_ref, g_ref, w32_ref, wdw_ref, bdw_ref, lng_ref, lnb_ref,
                          qe_ref, qo_ref, k_ref, v_ref, oc_ref, nc_ref,
                          w_ref, h_ref, cbuf_ref, sh_ref, y_ref):
    B, T, _ = oc_ref.shape
    _cast_weight(w32_ref, w_ref)
    h_ref[...] = _rmsnorm(x_ref[...], g_ref[...]).astype(bf16)
    u = _glu(h_ref, w_ref)

    def store(ref):
        def put(cs, val):
            ref[:, cs] = val
        return put

    proj = _project_stages(h_ref, w_ref, store(qe_ref), store(qo_ref), store(k_ref), store(v_ref))
    lead = CONV_LEAD - CONV_PAST

    def conv_stage(b):
        cbuf_ref[0:SUBLANES, :] = jnp.zeros((SUBLANES, CONV_WIDTH), f32)
        cbuf_ref[lead:CONV_LEAD, :] = st_ref[b]
        cbuf_ref[CONV_LEAD:CONV_LEAD + T, :] = u[b * T:(b + 1) * T]
        nc_ref[b] = cbuf_ref[CONV_LEAD + T - CONV_PAST:CONV_LEAD + T, :]
        shifts, taps = _conv_stages(cbuf_ref, sh_ref, y_ref, T, wdw_ref, bdw_ref)
        for stage in shifts + taps:
            stage()
        oc_ref[b] = _ln_swish(y_ref[...], lng_ref, lnb_ref).astype(bf16)

    _pipeline(proj, [functools.partial(conv_stage, b) for b in range(B)])


def _inproj_sample(x, st, g, w, wdw, bdw, lng, lnb):
    N, D = x.shape
    B = st.shape[0]
    T = N // B
    assert T >= CONV_PAST and T % SUBLANES == 0
    act = lambda dt: jax.ShapeDtypeStruct((N, ATTN_WIDTH), dt)
    return pl.pallas_call(
        _inproj_sample_kernel,
        grid=(1,),
        in_specs=[_const_spec(a.shape) for a in (x, st, g, w, wdw, bdw, lng, lnb)],
        out_specs=[_const_spec((N, ATTN_WIDTH))] * 4
                  + [_const_spec((B, T, CONV_WIDTH)), _const_spec((B, CONV_PAST, CONV_WIDTH))],
        out_shape=[act(bf16)] * 2 + [act(f32)] * 2
                  + [jax.ShapeDtypeStruct((B, T, CONV_WIDTH), bf16),
                     jax.ShapeDtypeStruct((B, CONV_PAST, CONV_WIDTH), f32)],
        scratch_shapes=[pltpu.VMEM(w.shape, bf16),
                        pltpu.VMEM((N, D), bf16),
                        pltpu.VMEM((CONV_LEAD + T, CONV_WIDTH), f32),
                        pltpu.VMEM((SUBLANES - 1, CONV_LEAD + T - SUBLANES, CONV_WIDTH), f32),
                        pltpu.VMEM((T, CONV_WIDTH), f32)],
        compiler_params=pltpu.CompilerParams(vmem_limit_bytes=VMEM_LIMIT),
        name="inproj_sample",
    )(x, st, g, w, wdw, bdw, lng, lnb)


def _toeplitz(frow, rows):
    return pltpu.roll(jnp.broadcast_to(frow, (rows, F_LEN)), 0, 1, stride=1, stride_axis=0)


def _softmax_parts(s_parts):
    m = functools.reduce(jnp.maximum, [jnp.max(s, axis=-1, keepdims=True) for s in s_parts])
    e_parts = [jnp.exp(s - m) for s in s_parts]
    l = functools.reduce(jnp.add, [jnp.sum(e, axis=-1, keepdims=True) for e in e_parts])
    return e_parts, l


def _mix_prompt_kernel(qe_ref, qo_ref, *refs):
    k_refs, v_refs = refs[:N_KBLK], refs[N_KBLK:2 * N_KBLK]
    oc_ref, x_ref, f_ref, wout_ref, o_ref, bias_ref, cat_ref = refs[2 * N_KBLK:]
    i = pl.program_id(1)

    @pl.when((pl.program_id(0) == 0) & (i == 0))
    def _():
        for h in range(N_HEADS):
            p, par = divmod(h, 2)
            bias_ref[p, par * SUB:(par + 1) * SUB, :] = _toeplitz(f_ref[h:h + 1, :], SUB)[:, :WIN]

    row = lax.broadcasted_iota(jnp.int32, (2 * SUB, LANES), 0)
    lane = lax.broadcasted_iota(jnp.int32, (2 * SUB, LANES), 1)
    odd_chunk = (row & CHUNK) != 0
    edge_ok = {0: jnp.logical_not(odd_chunk & (lane < CHUNK)),
               WIN // LANES - 1: odd_chunk | (lane < CHUNK)}
    blk_ok = [i >= N_KBLK - 1 - b for b in range(N_KBLK - 1)]
    even_head = lax.broadcasted_iota(jnp.int32, (SUB, LANES), 1) < HEAD_DIM

    def pieces(r):
        out = []
        for b in range(N_KBLK):
            lo, hi = max(SUB * r, TQ * b), min(SUB * r + WIN, TQ * (b + 1))
            if lo < hi:
                out.append((b, lo - TQ * b, hi - TQ * b))
        return out

    def scores(r, p):
        rs, ps = slice(r * SUB, (r + 1) * SUB), slice(p * LANES, (p + 1) * LANES)
        q2 = jnp.concatenate([qe_ref[0, rs, ps], qo_ref[0, rs, ps]], axis=0)
        return jnp.concatenate(
            [_dot_nt(q2, k_refs[b][0, a:z, ps]) for b, a, z in pieces(r)], axis=1)

    def finish(r, p, s):
        rs, ps = slice(r * SUB, (r + 1) * SUB), slice(p * LANES, (p + 1) * LANES)
        s = s + bias_ref[p]
        tiles = []
        for t in range(WIN // LANES):
            st = s[:, t * LANES:(t + 1) * LANES]
            b = (SUB * r + LANES * t) // TQ
            ok = edge_ok.get(t)
            if b < N_KBLK - 1:
                ok = blk_ok[b] if ok is None else ok & blk_ok[b]
            tiles.append(st if ok is None else jnp.where(ok, st, NEG_INF))
        (e,), l = _softmax_parts([jnp.concatenate(tiles, axis=1)])
        pb = e.astype(bf16)
        o, c0 = None, 0
        for b, a, z in pieces(r):
            ob = _dot(pb[:, c0:c0 + z - a], v_refs[b][0, a:z, ps])
            o = ob if o is None else o + ob
            c0 += z - a
        o = o / l
        cat_ref[rs, ps] = jnp.where(even_head, o[:SUB], o[SUB:]).astype(bf16)

    units = [(r, p) for r in range(TQ // SUB) for p in range(N_PAIRS)]
    s_next = scores(*units[0])
    for n, unit in enumerate(units):
        s_cur = s_next
        if n + 1 < len(units):
            s_next = scores(*units[n + 1])
        finish(*unit, s_cur)

    o_ref[0] = (x_ref[0] + _dot(cat_ref[...], wout_ref[0:ATTN_WIDTH, :])
                + _dot(oc_ref[0], wout_ref[ATTN_WIDTH:, :]))


def _mix_prompt(qe, qo, k, v, oc, x, fvec, wout):
    B, S, D = x.shape
    assert S % TQ == 0 and W_BAND % TQ == 0 and TQ % SUB == 0 and WIN % LANES == 0
    qspec = pl.BlockSpec((1, TQ, ATTN_WIDTH), lambda b, i: (b, i, 0))
    kspec = lambda back: pl.BlockSpec(
        (1, TQ, ATTN_WIDTH), lambda b, i: (b, jnp.maximum(i - back, 0), 0))
    kspecs = [kspec(N_KBLK - 1 - n) for n in range(N_KBLK)]
    xspec = pl.BlockSpec((1, TQ, D), lambda b, i: (b, i, 0))
    return pl.pallas_call(
        _mix_prompt_kernel,
        grid=(B, S // TQ),
        in_specs=[qspec, qspec] + kspecs + kspecs + [qspec, xspec,
                  _const_spec(fvec.shape), _const_spec(wout.shape)],
        out_specs=xspec,
        out_shape=jax.ShapeDtypeStruct((B, S, D), f32),
        scratch_shapes=[pltpu.VMEM((N_PAIRS, 2 * SUB, WIN), f32),
                        pltpu.VMEM((TQ, ATTN_WIDTH), bf16)],
        compiler_params=pltpu.CompilerParams(
            dimension_semantics=("arbitrary", "arbitrary"), vmem_limit_bytes=VMEM_LIMIT),
        name="mix_prompt",
    )(qe, qo, *([k] * N_KBLK), *([v] * N_KBLK), oc, x, fvec, wout)


def _mix_sample_kernel(qe_ref, qo_ref, kn_ref, vn_ref, ck_ref, cv_ref, oc_ref, x_ref,
                       f_ref, wout_ref, o_ref, nk_ref, nv_ref, bias_ref, cat_ref):
    T = oc_ref.shape[1]
    W = ck_ref.shape[1]

    @pl.when(pl.program_id(0) == 0)
    def _():
        for h in range(N_HEADS):
            p, par = divmod(h, 2)
            bias_ref[p, par * T:(par + 1) * T, :] = _toeplitz(f_ref[h:h + 1, :], T)

    nk_ref[0, 0:W - T, :] = ck_ref[0, T:W, :]
    nk_ref[0, W - T:W, :] = kn_ref[0]
    nv_ref[0, 0:W - T, :] = cv_ref[0, T:W, :]
    nv_ref[0, W - T:W, :] = vn_ref[0]

    lane = lax.broadcasted_iota(jnp.int32, (T, LANES), 1)
    even_head = lane < HEAD_DIM
    for p in range(N_PAIRS):
        ps = slice(p * LANES, (p + 1) * LANES)
        kc = ck_ref[0, :, ps].astype(bf16)
        vc = cv_ref[0, :, ps].astype(bf16)
        kn = kn_ref[0, :, ps].astype(bf16)
        vn = vn_ref[0, :, ps].astype(bf16)
        q2 = jnp.concatenate([qe_ref[0, :, ps], qo_ref[0, :, ps]], axis=0)
        (ec, en), l = _softmax_parts([_dot_nt(q2, kc) + bias_ref[p, :, 0:W],
                                      _dot_nt(q2, kn) + bias_ref[p, :, W:W + T]])
        o = (_dot(ec.astype(bf16), vc) + _dot(en.astype(bf16), vn)) / l
        cat_ref[:, ps] = jnp.where(even_head, o[:T], o[T:]).astype(bf16)

    o_ref[0] = (x_ref[0] + _dot(cat_ref[...], wout_ref[0:ATTN_WIDTH, :])
                + _dot(oc_ref[0], wout_ref[ATTN_WIDTH:, :]))


def _mix_sample(qe, qo, kn, vn, ck, cv, oc, x, fvec, wout):
    B, T, D = x.shape
    W = ck.shape[1]
    assert W == W_BAND and W + T <= F_LEN - T
    row = lambda r, n: pl.BlockSpec((1, r, n), lambda b: (b, 0, 0))
    return pl.pallas_call(
        _mix_sample_kernel,
        grid=(B,),
        in_specs=[row(T, ATTN_WIDTH)] * 4 + [row(W, ATTN_WIDTH)] * 2
                 + [row(T, CONV_WIDTH), row(T, D), _const_spec(fvec.shape), _const_spec(wout.shape)],
        out_specs=[row(T, D), row(W, ATTN_WIDTH), row(W, ATTN_WIDTH)],
        out_shape=[jax.ShapeDtypeStruct((B, T, D), f32),
                   jax.ShapeDtypeStruct((B, W, ATTN_WIDTH), f32),
                   jax.ShapeDtypeStruct((B, W, ATTN_WIDTH), f32)],
        scratch_shapes=[pltpu.VMEM((N_PAIRS, 2 * T, F_LEN), f32),
                        pltpu.VMEM((T, ATTN_WIDTH), bf16)],
        compiler_params=pltpu.CompilerParams(
            dimension_semantics=("arbitrary",), vmem_limit_bytes=VMEM_LIMIT),
        name="mix_sample",
    )(qe, qo, kn, vn, ck, cv, oc, x, fvec, wout)


def _ffn_kernel(x_ref, g_ref, wg_ref, wu_ref, wd_ref, gf_ref, o_ref):
    x = x_ref[...]
    h = _rmsnorm(x, g_ref[...]).astype(bf16)
    acc = x
    for c0 in range(0, D_FF, FFN_COLS):
        cs = slice(c0, min(c0 + FFN_COLS, D_FF))
        a = jax.nn.silu(_dot(h, wg_ref[:, cs])) * _dot(h, wu_ref[:, cs])
        acc = acc + _dot(a.astype(bf16), wd_ref[cs, :])
    o_ref[...] = _rmsnorm(acc, gf_ref[...])


def _ffn(x, g, wg, wu, wd, gf, tm):
    N, D = x.shape
    assert N % tm == 0
    tile = pl.BlockSpec((tm, D), lambda i: (i, 0))
    return pl.pallas_call(
        _ffn_kernel,
        grid=(N // tm,),
        in_specs=[tile, _const_spec((1, D)), _const_spec(wg.shape), _const_spec(wu.shape),
                  _const_spec(wd.shape), _const_spec((1, D))],
        out_specs=tile,
        out_shape=jax.ShapeDtypeStruct((N, D), f32),
        compiler_params=pltpu.CompilerParams(
            dimension_semantics=("arbitrary",), vmem_limit_bytes=VMEM_LIMIT),
        name="ffn",
    )(x, g, wg, wu, wd, gf)


def _rel_bias_row(table):
    n_rel = table.shape[1]
    far = W_BAND - REL_CLIP
    assert n_rel == 2 * REL_CLIP + 1 and far + n_rel + SUB <= F_LEN
    rep = lambda col, n: jnp.broadcast_to(table[:, col:col + 1], (table.shape[0], n))
    return jnp.concatenate([rep(n_rel - 1, far), table[:, ::-1],
                            rep(0, F_LEN - SUB - far - n_rel), rep(n_rel - 1, SUB)], axis=1)


def kernel(x_prompt, x_sample, cache_k, cache_v, state_conv, g_mix, w_in, rel_table, w_dw, b_dw,
           ln_g, ln_b, w_out, g_ffn, w_gate, w_up, w_down, g_final):
    assert g_mix.shape[0] == 1, "single-layer trunk: the final RMSNorm is fused into the FFN kernel"
    B, S, D = x_prompt.shape
    Bs, T, _ = x_sample.shape
    W = cache_k.shape[2]
    gm, gff, gf = g_mix[0].reshape(1, D), g_ffn[0].reshape(1, D), g_final.reshape(1, D)
    win, wout = w_in[0], w_out[0].astype(bf16)
    wg, wu, wd = w_gate[0].astype(bf16), w_up[0].astype(bf16), w_down[0].astype(bf16)
    conv_params = (w_dw[0],) + tuple(a[0].reshape(1, CONV_WIDTH) for a in (b_dw, ln_g, ln_b))
    heads = lambda a: a.reshape(1, a.shape[0], a.shape[1], N_HEADS, HEAD_DIM)
    fvec = _rel_bias_row(rel_table[0])

    qe, qo, k, v, oc, kf, vf, ct = _inproj_prompt(x_prompt, gm, win, *conv_params)
    x1 = _mix_prompt(qe, qo, k, v, oc, x_prompt, fvec, wout)
    y_prompt = _ffn(x1.reshape(B * S, D), gff, wg, wu, wd, gf, TM_FFN).reshape(B, S, D)

    qe, qo, kn, vn, oc, nc = _inproj_sample(
        x_sample.reshape(Bs * T, D), state_conv[0], gm, win, *conv_params)
    r3 = lambda a: a.reshape(Bs, T, ATTN_WIDTH)
    x1, nk, nv = _mix_sample(
        r3(qe), r3(qo), r3(kn), r3(vn),
        cache_k[0].reshape(Bs, W, ATTN_WIDTH), cache_v[0].reshape(Bs, W, ATTN_WIDTH),
        oc, x_sample, fvec, wout)
    y_sample = _ffn(x1.reshape(Bs * T, D), gff, wg, wu, wd, gf, Bs * T).reshape(Bs, T, D)

    return (y_prompt, y_sample, heads(kf), heads(vf), ct[None],
            heads(nk), heads(nv), nc[None])
```

```python
import functools

import jax
import jax.numpy as jnp
from jax import lax
from jax.experimental import pallas as pl
from jax.experimental.pallas import tpu as pltpu

D_MODEL = 1024
CHUNK = 64
LEFT_CHUNKS = 8
W_BAND = LEFT_CHUNKS * CHUNK
ATTN_WIDTH = 512
N_HEADS = 8
HEAD_DIM = 64
CONV_WIDTH = 512
CONV_KERNEL = 31
CONV_PAST = CONV_KERNEL - 1
REL_CLIP = 128
D_FF = 2816
RMS_EPS = 1e-6
LN_EPS = 1e-5
NEG_INF = -1e30
SCALE = HEAD_DIM ** -0.5

LANES = 128
SUBLANES = 8
N_PAIRS = ATTN_WIDTH // LANES
TT_PROJ = 64
Z_STRIDE = TT_PROJ + SUBLANES
PROJ_COLS = 256
TQ = 512
N_KBLK = W_BAND // TQ + 1
SUB = 2 * CHUNK
WIN = W_BAND + SUB
F_LEN = 1024
TM_FFN = 512
MXU_TILE = 256
FFN_COLS = 4 * MXU_TILE
CONV_ROWS = 64
CONV_STEPS = 16
CONV_LEAD = 32
VMEM_LIMIT = 56 * 1024 * 1024

f32 = jnp.float32
bf16 = jnp.bfloat16


def _rmsnorm(x, g):
    return (x * lax.rsqrt(jnp.mean(x * x, axis=-1, keepdims=True) + RMS_EPS)) * g


def _dot(a, b):
    return jnp.dot(a, b, preferred_element_type=f32)


def _dot_nt(a, b):
    return lax.dot_general(a, b, (((1,), (1,)), ((), ())), preferred_element_type=f32)


def _const_spec(shape):
    nd = len(shape)
    return pl.BlockSpec(shape, lambda *_: (0,) * nd, pipeline_mode=pl.Buffered(1))


def _project_stages(h_ref, w_ref, qe_store, qo_store, k_store, v_store):
    def issue(base, cs):
        return _dot(h_ref[...], w_ref[:, base + cs.start:base + cs.stop])

    def q_finish(cs, z):
        q = z * SCALE
        lane = lax.broadcasted_iota(jnp.int32, q.shape, 1)
        even = (lane & HEAD_DIM) == 0
        qe_store(cs, jnp.where(even, q, 0.0).astype(bf16))
        qo_store(cs, jnp.where(even, 0.0, q).astype(bf16))

    stages = []
    for base, finish in ((0, q_finish), (ATTN_WIDTH, k_store), (2 * ATTN_WIDTH, v_store)):
        for c0 in range(0, ATTN_WIDTH, PROJ_COLS):
            cs = slice(c0, c0 + PROJ_COLS)
            stages.append((functools.partial(issue, base, cs), functools.partial(finish, cs)))
    return stages


def _cast_weight(w32_ref, w_ref):
    for c0 in range(0, w_ref.shape[1], MXU_TILE):
        w_ref[:, c0:c0 + MXU_TILE] = w32_ref[:, c0:c0 + MXU_TILE].astype(bf16)


def _glu(h_ref, w_ref):
    base = 3 * ATTN_WIDTH
    a = _dot(h_ref[...], w_ref[:, base:base + CONV_WIDTH])
    g = _dot(h_ref[...], w_ref[:, base + CONV_WIDTH:base + 2 * CONV_WIDTH])
    return a * jax.nn.sigmoid(g)


def _conv_stages(cbuf_ref, sh_ref, y_ref, rows, wdw_ref, bdw_ref):
    lead = CONV_LEAD - CONV_PAST
    span = rows + CONV_LEAD - SUBLANES
    rb = min(CONV_ROWS, rows)

    def shift_stage(m):
        sh_ref[m - 1, 0:span, :] = cbuf_ref[m:m + span, :]

    def tap_stage(cs, r0):
        acc = jnp.zeros((rb, LANES), f32)
        for j in range(CONV_KERNEL):
            a, m = divmod(j + lead, SUBLANES)
            src = cbuf_ref if m == 0 else sh_ref.at[m - 1]
            lo = r0 + SUBLANES * a
            acc = acc + wdw_ref[j:j + 1, cs] * src[lo:lo + rb, cs]
        y_ref[r0:r0 + rb, cs] = acc + bdw_ref[:, cs]

    shifts = [functools.partial(shift_stage, m) for m in range(1, SUBLANES)]
    taps = [functools.partial(tap_stage, slice(c * LANES, (c + 1) * LANES), r0)
            for c in range(CONV_WIDTH // LANES) for r0 in range(0, rows, rb)]
    return shifts, taps


def _conv_time_major(ut_ref, y_ref, steps, wdw_ref, bdw_ref):
    for c in range(CONV_WIDTH // LANES):
        cs = slice(c * LANES, (c + 1) * LANES)
        for t0 in range(0, steps, CONV_STEPS):
            acc = jnp.zeros((CONV_STEPS, ut_ref.shape[2], LANES), f32)
            for j in range(CONV_KERNEL):
                acc = acc + wdw_ref[j:j + 1, cs] * ut_ref[c, t0 + j:t0 + j + CONV_STEPS]
            y_ref[c, t0:t0 + CONV_STEPS] = acc + bdw_ref[:, cs]


def _ln_swish_tiles(y_tiles, lng_ref, lnb_ref):
    n = len(y_tiles) * LANES
    mu = sum(jnp.sum(y, axis=-1, keepdims=True) for y in y_tiles) / n
    yc = [y - mu for y in y_tiles]
    var = sum(jnp.sum(y * y, axis=-1, keepdims=True) for y in yc) / n
    inv = lax.rsqrt(var + LN_EPS)
    out = []
    for c, y in enumerate(yc):
        cs = slice(c * LANES, (c + 1) * LANES)
        yn = y * inv * lng_ref[:, cs] + lnb_ref[:, cs]
        out.append(yn * jax.nn.sigmoid(yn))
    return out


def _ln_swish(y, lng_ref, lnb_ref):
    mu = jnp.mean(y, axis=-1, keepdims=True)
    yc = y - mu
    yn = yc * lax.rsqrt(jnp.mean(yc * yc, axis=-1, keepdims=True) + LN_EPS)
    yn = yn * lng_ref[...] + lnb_ref[...]
    return yn * jax.nn.sigmoid(yn)


def _pipeline(mxu_stages, valu_stages):
    pending = mxu_stages[0][0]()
    done = 0
    for n, (_, consume) in enumerate(mxu_stages):
        cur = pending
        if n + 1 < len(mxu_stages):
            pending = mxu_stages[n + 1][0]()
        upto = (n + 1) * len(valu_stages) // len(mxu_stages)
        for stage in valu_stages[done:upto]:
            stage()
        done = upto
        consume(cur)


def _inproj_prompt_kernel(x_ref, g_ref, w32_ref, wdw_ref, bdw_ref, lng_ref, lnb_ref,
                          qe_ref, qo_ref, k_ref, v_ref, oc_ref, kf_ref, vf_ref, ct_ref,
                          w_ref, h_ref, z_ref, ut_ref, y_ref):
    nb = x_ref.shape[0]
    rows = nb * TT_PROJ

    @pl.when(pl.program_id(0) == 0)
    def _():
        _cast_weight(w32_ref, w_ref)
        ut_ref[:, 0:CONV_PAST] = jnp.zeros((CONV_WIDTH // LANES, CONV_PAST, nb, LANES), f32)

    h_ref[...] = _rmsnorm(x_ref[...].reshape(rows, D_MODEL), g_ref[...]).astype(bf16)

    def store(ref):
        def put(cs, val):
            ref[:, :, cs] = val.reshape(nb, TT_PROJ, val.shape[-1])
        return put

    def kv_store(ref, full_ref):
        def put(cs, val):
            val = val.reshape(nb, TT_PROJ, val.shape[-1])
            ref[:, :, cs] = val.astype(bf16)
            full_ref[:, :, cs] = val
        return put

    u = _glu(h_ref, w_ref)
    for issue, consume in _project_stages(h_ref, w_ref, store(qe_ref), store(qo_ref),
                                          kv_store(k_ref, kf_ref), kv_store(v_ref, vf_ref)):
        consume(issue())
    lane_tiles = [slice(c * LANES, (c + 1) * LANES) for c in range(CONV_WIDTH // LANES)]
    for c, cs in enumerate(lane_tiles):
        for b in range(nb):
            z_ref[c, Z_STRIDE * b:Z_STRIDE * b + TT_PROJ, :] = u[b * TT_PROJ:(b + 1) * TT_PROJ, cs]
            ct_ref[b, :, cs] = z_ref[c, Z_STRIDE * b + TT_PROJ - CONV_PAST:Z_STRIDE * b + TT_PROJ, :]
        for t in range(TT_PROJ):
            ut_ref[c, CONV_PAST + t] = z_ref[c, pl.ds(t, nb, stride=Z_STRIDE), :]
    _conv_time_major(ut_ref, y_ref, TT_PROJ, wdw_ref, bdw_ref)
    oc = _ln_swish_tiles([y_ref[c] for c in range(len(lane_tiles))], lng_ref, lnb_ref)
    for c, cs in enumerate(lane_tiles):
        for t in range(TT_PROJ):
            z_ref[c, pl.ds(t, nb, stride=Z_STRIDE), :] = oc[c][t]
        for b in range(nb):
            oc_ref[b, :, cs] = z_ref[c, Z_STRIDE * b:Z_STRIDE * b + TT_PROJ, :].astype(bf16)

    ut_ref[:, 0:CONV_PAST] = ut_ref[:, TT_PROJ:TT_PROJ + CONV_PAST]


def _inproj_prompt(x, g, w, wdw, bdw, lng, lnb):
    B, S, D = x.shape
    assert S % TT_PROJ == 0 and W_BAND % TT_PROJ == 0 and TT_PROJ >= CONV_PAST
    first_kept = (S - W_BAND) // TT_PROJ
    tile = lambda n: pl.BlockSpec((B, TT_PROJ, n), lambda i: (0, i, 0))
    kept = pl.BlockSpec((B, TT_PROJ, ATTN_WIDTH), lambda i: (0, jnp.maximum(i - first_kept, 0), 0))
    act = jax.ShapeDtypeStruct((B, S, ATTN_WIDTH), bf16)
    rows = B * TT_PROJ
    return pl.pallas_call(
        _inproj_prompt_kernel,
        grid=(S // TT_PROJ,),
        in_specs=[tile(D)] + [_const_spec(a.shape) for a in (g, w, wdw, bdw, lng, lnb)],
        out_specs=[tile(ATTN_WIDTH)] * 5 + [kept] * 2 + [_const_spec((B, CONV_PAST, CONV_WIDTH))],
        out_shape=[act] * 5 + [jax.ShapeDtypeStruct((B, W_BAND, ATTN_WIDTH), f32)] * 2
                  + [jax.ShapeDtypeStruct((B, CONV_PAST, CONV_WIDTH), f32)],
        scratch_shapes=[pltpu.VMEM(w.shape, bf16),
                        pltpu.VMEM((rows, D), bf16),
                        pltpu.VMEM((CONV_WIDTH // LANES, B * Z_STRIDE, LANES), f32),
                        pltpu.VMEM((CONV_WIDTH // LANES, CONV_PAST + TT_PROJ, B, LANES), f32),
                        pltpu.VMEM((CONV_WIDTH // LANES, TT_PROJ, B, LANES), f32)],
        compiler_params=pltpu.CompilerParams(
            dimension_semantics=("arbitrary",), vmem_limit_bytes=VMEM_LIMIT),
        name="inproj_prompt",
    )(x, g, w, wdw, bdw, lng, lnb)


def _inproj_sample_kernel(x_ref, st_ref, g_ref, w32_ref, wdw_ref, bdw_ref, lng_ref, lnb_ref,
                          qe_ref, qo_ref, k_ref, v_ref, oc_ref, nc_ref,
                          w_ref, h_ref, cbuf_ref, sh_ref, y_ref):
    B, T, _ = oc_ref.shape
    _cast_weight(w32_ref, w_ref)
    h_ref[...] = _rmsnorm(x_ref[...], g_ref[...]).astype(bf16)
    u = _glu(h_ref, w_ref)

    def store(ref):
        def put(cs, val):
            ref[:, cs] = val
        return put

    proj = _project_stages(h_ref, w_ref, store(qe_ref), store(qo_ref), store(k_ref), store(v_ref))
    lead = CONV_LEAD - CONV_PAST

    def conv_stage(b):
        cbuf_ref[0:SUBLANES, :] = jnp.zeros((SUBLANES, CONV_WIDTH), f32)
        cbuf_ref[lead:CONV_LEAD, :] = st_ref[b]
        cbuf_ref[CONV_LEAD:CONV_LEAD + T, :] = u[b * T:(b + 1) * T]
        nc_ref[b] = cbuf_ref[CONV_LEAD + T - CONV_PAST:CONV_LEAD + T, :]
        shifts, taps = _conv_stages(cbuf_ref, sh_ref, y_ref, T, wdw_ref, bdw_ref)
        for stage in shifts + taps:
            stage()
        oc_ref[b] = _ln_swish(y_ref[...], lng_ref, lnb_ref).astype(bf16)

    _pipeline(proj, [functools.partial(conv_stage, b) for b in range(B)])


def _inproj_sample(x, st, g, w, wdw, bdw, lng, lnb):
    N, D = x.shape
    B = st.shape[0]
    T = N // B
    assert T >= CONV_PAST and T % SUBLANES == 0
    act = lambda dt: jax.ShapeDtypeStruct((N, ATTN_WIDTH), dt)
    return pl.pallas_call(
        _inproj_sample_kernel,
        grid=(1,),
        in_specs=[_const_spec(a.shape) for a in (x, st, g, w, wdw, bdw, lng, lnb)],
        out_specs=[_const_spec((N, ATTN_WIDTH))] * 4
                  + [_const_spec((B, T, CONV_WIDTH)), _const_spec((B, CONV_PAST, CONV_WIDTH))],
        out_shape=[act(bf16)] * 2 + [act(f32)] * 2
                  + [jax.ShapeDtypeStruct((B, T, CONV_WIDTH), bf16),
                     jax.ShapeDtypeStruct((B, CONV_PAST, CONV_WIDTH), f32)],
        scratch_shapes=[pltpu.VMEM(w.shape, bf16),
                        pltpu.VMEM((N, D), bf16),
                        pltpu.VMEM((CONV_LEAD + T, CONV_WIDTH), f32),
                        pltpu.VMEM((SUBLANES - 1, CONV_LEAD + T - SUBLANES, CONV_WIDTH), f32),
                        pltpu.VMEM((T, CONV_WIDTH), f32)],
        compiler_params=pltpu.CompilerParams(vmem_limit_bytes=VMEM_LIMIT),
        name="inproj_sample",
    )(x, st, g, w, wdw, bdw, lng, lnb)


def _toeplitz(frow, rows):
    return pltpu.roll(jnp.broadcast_to(frow, (rows, F_LEN)), 0, 1, stride=1, stride_axis=0)


def _softmax_parts(s_parts):
    m = functools.reduce(jnp.maximum, [jnp.max(s, axis=-1, keepdims=True) for s in s_parts])
    e_parts = [jnp.exp(s - m) for s in s_parts]
    l = functools.reduce(jnp.add, [jnp.sum(e, axis=-1, keepdims=True) for e in e_parts])
    return e_parts, l


def _mix_prompt_kernel(qe_ref, qo_ref, *refs):
    k_refs, v_refs = refs[:N_KBLK], refs[N_KBLK:2 * N_KBLK]
    oc_ref, x_ref, f_ref, wout_ref, o_ref, bias_ref, cat_ref = refs[2 * N_KBLK:]
    i = pl.program_id(1)

    @pl.when((pl.program_id(0) == 0) & (i == 0))
    def _():
        for h in range(N_HEADS):
            p, par = divmod(h, 2)
            bias_ref[p, par * SUB:(par + 1) * SUB, :] = _toeplitz(f_ref[h:h + 1, :], SUB)[:, :WIN]

    row = lax.broadcasted_iota(jnp.int32, (2 * SUB, LANES), 0)
    lane = lax.broadcasted_iota(jnp.int32, (2 * SUB, LANES), 1)
    odd_chunk = (row & CHUNK) != 0
    edge_ok = {0: jnp.logical_not(odd_chunk & (lane < CHUNK)),
               WIN // LANES - 1: odd_chunk | (lane < CHUNK)}
    blk_ok = [i >= N_KBLK - 1 - b for b in range(N_KBLK - 1)]
    even_head = lax.broadcasted_iota(jnp.int32, (SUB, LANES), 1) < HEAD_DIM

    def pieces(r):
        out = []
        for b in range(N_KBLK):
            lo, hi = max(SUB * r, TQ * b), min(SUB * r + WIN, TQ * (b + 1))
            if lo < hi:
                out.append((b, lo - TQ * b, hi - TQ * b))
        return out

    def scores(r, p):
        rs, ps = slice(r * SUB, (r + 1) * SUB), slice(p * LANES, (p + 1) * LANES)
        q2 = jnp.concatenate([qe_ref[0, rs, ps], qo_ref[0, rs, ps]], axis=0)
        return jnp.concatenate(
            [_dot_nt(q2, k_refs[b][0, a:z, ps]) for b, a, z in pieces(r)], axis=1)

    def finish(r, p, s):
        rs, ps = slice(r * SUB, (r + 1) * SUB), slice(p * LANES, (p + 1) * LANES)
        s = s + bias_ref[p]
        tiles = []
        for t in range(WIN // LANES):
            st = s[:, t * LANES:(t + 1) * LANES]
            b = (SUB * r + LANES * t) // TQ
            ok = edge_ok.get(t)
            if b < N_KBLK - 1:
                ok = blk_ok[b] if ok is None else ok & blk_ok[b]
            tiles.append(st if ok is None else jnp.where(ok, st, NEG_INF))
        (e,), l = _softmax_parts([jnp.concatenate(tiles, axis=1)])
        pb = e.astype(bf16)
        o, c0 = None, 0
        for b, a, z in pieces(r):
            ob = _dot(pb[:, c0:c0 + z - a], v_refs[b][0, a:z, ps])
            o = ob if o is None else o + ob
            c0 += z - a
        o = o / l
        cat_ref[rs, ps] = jnp.where(even_head, o[:SUB], o[SUB:]).astype(bf16)

    units = [(r, p) for r in range(TQ // SUB) for p in range(N_PAIRS)]
    s_next = scores(*units[0])
    for n, unit in enumerate(units):
        s_cur = s_next
        if n + 1 < len(units):
            s_next = scores(*units[n + 1])
        finish(*unit, s_cur)

    o_ref[0] = (x_ref[0] + _dot(cat_ref[...], wout_ref[0:ATTN_WIDTH, :])
                + _dot(oc_ref[0], wout_ref[ATTN_WIDTH:, :]))


def _mix_prompt(qe, qo, k, v, oc, x, fvec, wout):
    B, S, D = x.shape
    assert S % TQ == 0 and W_BAND % TQ == 0 and TQ % SUB == 0 and WIN % LANES == 0
    qspec = pl.BlockSpec((1, TQ, ATTN_WIDTH), lambda b, i: (b, i, 0))
    kspec = lambda back: pl.BlockSpec(
        (1, TQ, ATTN_WIDTH), lambda b, i: (b, jnp.maximum(i - back, 0), 0))
    kspecs = [kspec(N_KBLK - 1 - n) for n in range(N_KBLK)]
    xspec = pl.BlockSpec((1, TQ, D), lambda b, i: (b, i, 0))
    return pl.pallas_call(
        _mix_prompt_kernel,
        grid=(B, S // TQ),
        in_specs=[qspec, qspec] + kspecs + kspecs + [qspec, xspec,
                  _const_spec(fvec.shape), _const_spec(wout.shape)],
        out_specs=xspec,
        out_shape=jax.ShapeDtypeStruct((B, S, D), f32),
        scratch_shapes=[pltpu.VMEM((N_PAIRS, 2 * SUB, WIN), f32),
                        pltpu.VMEM((TQ, ATTN_WIDTH), bf16)],
        compiler_params=pltpu.CompilerParams(
            dimension_semantics=("arbitrary", "arbitrary"), vmem_limit_bytes=VMEM_LIMIT),
        name="mix_prompt",
    )(qe, qo, *([k] * N_KBLK), *([v] * N_KBLK), oc, x, fvec, wout)


def _mix_sample_kernel(qe_ref, qo_ref, kn_ref, vn_ref, ck_ref, cv_ref, oc_ref, x_ref,
                       f_ref, wout_ref, o_ref, nk_ref, nv_ref, bias_ref, cat_ref):
    T = oc_ref.shape[1]
    W = ck_ref.shape[1]

    @pl.when(pl.program_id(0) == 0)
    def _():
        for h in range(N_HEADS):
            p, par = divmod(h, 2)
            bias_ref[p, par * T:(par + 1) * T, :] = _toeplitz(f_ref[h:h + 1, :], T)

    nk_ref[0, 0:W - T, :] = ck_ref[0, T:W, :]
    nk_ref[0, W - T:W, :] = kn_ref[0]
    nv_ref[0, 0:W - T, :] = cv_ref[0, T:W, :]
    nv_ref[0, W - T:W, :] = vn_ref[0]

    lane = lax.broadcasted_iota(jnp.int32, (T, LANES), 1)
    even_head = lane < HEAD_DIM
    for p in range(N_PAIRS):
        ps = slice(p * LANES, (p + 1) * LANES)
        kc = ck_ref[0, :, ps].astype(bf16)
        vc = cv_ref[0, :, ps].astype(bf16)
        kn = kn_ref[0, :, ps].astype(bf16)
        vn = vn_ref[0, :, ps].astype(bf16)
        q2 = jnp.concatenate([qe_ref[0, :, ps], qo_ref[0, :, ps]], axis=0)
        (ec, en), l = _softmax_parts([_dot_nt(q2, kc) + bias_ref[p, :, 0:W],
                                      _dot_nt(q2, kn) + bias_ref[p, :, W:W + T]])
        o = (_dot(ec.astype(bf16), vc) + _dot(en.astype(bf16), vn)) / l
        cat_ref[:, ps] = jnp.where(even_head, o[:T], o[T:]).astype(bf16)

    o_ref[0] = (x_ref[0] + _dot(cat_ref[...], wout_ref[0:ATTN_WIDTH, :])
                + _dot(oc_ref[0], wout_ref[ATTN_WIDTH:, :]))


def _mix_sample(qe, qo, kn, vn, ck, cv, oc, x, fvec, wout):
    B, T, D = x.shape
    W = ck.shape[1]
    assert W == W_BAND and W + T <= F_LEN - T
    row = lambda r, n: pl.BlockSpec((1, r, n), lambda b: (b, 0, 0))
    return pl.pallas_call(
        _mix_sample_kernel,
        grid=(B,),
        in_specs=[row(T, ATTN_WIDTH)] * 4 + [row(W, ATTN_WIDTH)] * 2
                 + [row(T, CONV_WIDTH), row(T, D), _const_spec(fvec.shape), _const_spec(wout.shape)],
        out_specs=[row(T, D), row(W, ATTN_WIDTH), row(W, ATTN_WIDTH)],
        out_shape=[jax.ShapeDtypeStruct((B, T, D), f32),
                   jax.ShapeDtypeStruct((B, W, ATTN_WIDTH), f32),
                   jax.ShapeDtypeStruct((B, W, ATTN_WIDTH), f32)],
        scratch_shapes=[pltpu.VMEM((N_PAIRS, 2 * T, F_LEN), f32),
                        pltpu.VMEM((T, ATTN_WIDTH), bf16)],
        compiler_params=pltpu.CompilerParams(
            dimension_semantics=("arbitrary",), vmem_limit_bytes=VMEM_LIMIT),
        name="mix_sample",
    )(qe, qo, kn, vn, ck, cv, oc, x, fvec, wout)


def _ffn_kernel(x_ref, g_ref, wg_ref, wu_ref, wd_ref, gf_ref, o_ref):
    x = x_ref[...]
    h = _rmsnorm(x, g_ref[...]).astype(bf16)
    acc = x
    for c0 in range(0, D_FF, FFN_COLS):
        cs = slice(c0, min(c0 + FFN_COLS, D_FF))
        a = jax.nn.silu(_dot(h, wg_ref[:, cs])) * _dot(h, wu_ref[:, cs])
        acc = acc + _dot(a.astype(bf16), wd_ref[cs, :])
    o_ref[...] = _rmsnorm(acc, gf_ref[...])


def _ffn(x, g, wg, wu, wd, gf, tm):
    N, D = x.shape
    assert N % tm == 0
    tile = pl.BlockSpec((tm, D), lambda i: (i, 0))
    return pl.pallas_call(
        _ffn_kernel,
        grid=(N // tm,),
        in_specs=[tile, _const_spec((1, D)), _const_spec(wg.shape), _const_spec(wu.shape),
                  _const_spec(wd.shape), _const_spec((1, D))],
        out_specs=tile,
        out_shape=jax.ShapeDtypeStruct((N, D), f32),
        compiler_params=pltpu.CompilerParams(
            dimension_semantics=("arbitrary",), vmem_limit_bytes=VMEM_LIMIT),
        name="ffn",
    )(x, g, wg, wu, wd, gf)


def _rel_bias_row(table):
    n_rel = table.shape[1]
    far = W_BAND - REL_CLIP
    assert n_rel == 2 * REL_CLIP + 1 and far + n_rel + SUB <= F_LEN
    rep = lambda col, n: jnp.broadcast_to(table[:, col:col + 1], (table.shape[0], n))
    return jnp.concatenate([rep(n_rel - 1, far), table[:, ::-1],
                            rep(0, F_LEN - SUB - far - n_rel), rep(n_rel - 1, SUB)], axis=1)


def kernel(x_prompt, x_sample, cache_k, cache_v, state_conv, g_mix, w_in, rel_table, w_dw, b_dw,
           ln_g, ln_b, w_out, g_ffn, w_gate, w_up, w_down, g_final):
    assert g_mix.shape[0] == 1, "single-layer trunk: the final RMSNorm is fused into the FFN kernel"
    B, S, D = x_prompt.shape
    Bs, T, _ = x_sample.shape
    W = cache_k.shape[2]
    gm, gff, gf = g_mix[0].reshape(1, D), g_ffn[0].reshape(1, D), g_final.reshape(1, D)
    win, wout = w_in[0], w_out[0].astype(bf16)
    wg, wu, wd = w_gate[0].astype(bf16), w_up[0].astype(bf16), w_down[0].astype(bf16)
    conv_params = (w_dw[0],) + tuple(a[0].reshape(1, CONV_WIDTH) for a in (b_dw, ln_g, ln_b))
    heads = lambda a: a.reshape(1, a.shape[0], a.shape[1], N_HEADS, HEAD_DIM)
    fvec = _rel_bias_row(rel_table[0])

    qe, qo, k, v, oc, kf, vf, ct = _inproj_prompt(x_prompt, gm, win, *conv_params)
    x1 = _mix_prompt(qe, qo, k, v, oc, x_prompt, fvec, wout)
    y_prompt = _ffn(x1.reshape(B * S, D), gff, wg, wu, wd, gf, TM_FFN).reshape(B, S, D)

    qe, qo, kn, vn, oc, nc = _inproj_sample(
        x_sample.reshape(Bs * T, D), state_conv[0], gm, win, *conv_params)
    r3 = lambda a: a.reshape(Bs, T, ATTN_WIDTH)
    x1, nk, nv = _mix_sample(
        r3(qe), r3(qo), r3(kn), r3(vn),
        cache_k[0].reshape(Bs, W, ATTN_WIDTH), cache_v[0].reshape(Bs, W, ATTN_WIDTH),
        oc, x_sample, fvec, wout)
    y_sample = _ffn(x1.reshape(Bs * T, D), gff, wg, wu, wd, gf, Bs * T).reshape(Bs, T, D)

    return (y_prompt, y_sample, heads(kf), heads(vf), ct[None],
            heads(nk), heads(nv), nc[None])
```

```python
import functools

import jax
import jax.numpy as jnp
from jax import lax
from jax.experimental import pallas as pl
from jax.experimental.pallas import tpu as pltpu

D_MODEL = 1024
CHUNK = 64
LEFT_CHUNKS = 8
W_BAND = LEFT_CHUNKS * CHUNK
ATTN_WIDTH = 512
N_HEADS = 8
HEAD_DIM = 64
CONV_WIDTH = 512
CONV_KERNEL = 31
CONV_PAST = CONV_KERNEL - 1
REL_CLIP = 128
D_FF = 2816
RMS_EPS = 1e-6
LN_EPS = 1e-5
NEG_INF = -1e30
SCALE = HEAD_DIM ** -0.5

LANES = 128
SUBLANES = 8
N_PAIRS = ATTN_WIDTH // LANES
TT_PROJ = 64
Z_STRIDE = TT_PROJ + SUBLANES
PROJ_COLS = 256
TQ = 512
N_KBLK = W_BAND // TQ + 1
SUB = 2 * CHUNK
WIN = W_BAND + SUB
F_LEN = 1024
LOOKAHEAD = 1
TM_FFN = 512
MXU_TILE = 256
FFN_COLS = 4 * MXU_TILE
CONV_ROWS = 64
CONV_STEPS = 16
CONV_LEAD = 32
VMEM_LIMIT = 56 * 1024 * 1024

f32 = jnp.float32
bf16 = jnp.bfloat16


def _rmsnorm(x, g):
    return (x * lax.rsqrt(jnp.mean(x * x, axis=-1, keepdims=True) + RMS_EPS)) * g


def _dot(a, b):
    return jnp.dot(a, b, preferred_element_type=f32)


def _dot_nt(a, b):
    return lax.dot_general(a, b, (((1,), (1,)), ((), ())), preferred_element_type=f32)


def _const_spec(shape):
    nd = len(shape)
    return pl.BlockSpec(shape, lambda *_: (0,) * nd, pipeline_mode=pl.Buffered(1))


def _project_stages(h_ref, w_ref, qe_store, qo_store, k_store, v_store):
    def issue(base, cs):
        return _dot(h_ref[...], w_ref[:, base + cs.start:base + cs.stop])

    def q_finish(cs, z):
        q = z * SCALE
        lane = lax.broadcasted_iota(jnp.int32, q.shape, 1)
        even = (lane & HEAD_DIM) == 0
        qe_store(cs, jnp.where(even, q, 0.0).astype(bf16))
        qo_store(cs, jnp.where(even, 0.0, q).astype(bf16))

    stages = []
    for base, finish in ((0, q_finish), (ATTN_WIDTH, k_store), (2 * ATTN_WIDTH, v_store)):
        for c0 in range(0, ATTN_WIDTH, PROJ_COLS):
            cs = slice(c0, c0 + PROJ_COLS)
            stages.append((functools.partial(issue, base, cs), functools.partial(finish, cs)))
    return stages


def _cast_weight(w32_ref, w_ref):
    for c0 in range(0, w_ref.shape[1], MXU_TILE):
        w_ref[:, c0:c0 + MXU_TILE] = w32_ref[:, c0:c0 + MXU_TILE].astype(bf16)


def _glu(h_ref, w_ref):
    base = 3 * ATTN_WIDTH
    a = _dot(h_ref[...], w_ref[:, base:base + CONV_WIDTH])
    g = _dot(h_ref[...], w_ref[:, base + CONV_WIDTH:base + 2 * CONV_WIDTH])
    return a * jax.nn.sigmoid(g)


def _conv_stages(cbuf_ref, sh_ref, y_ref, rows, wdw_ref, bdw_ref):
    lead = CONV_LEAD - CONV_PAST
    span = rows + CONV_LEAD - SUBLANES
    rb = min(CONV_ROWS, rows)

    def shift_stage(m):
        sh_ref[m - 1, 0:span, :] = cbuf_ref[m:m + span, :]

    def tap_stage(cs, r0):
        acc = jnp.zeros((rb, LANES), f32)
        for j in range(CONV_KERNEL):
            a, m = divmod(j + lead, SUBLANES)
            src = cbuf_ref if m == 0 else sh_ref.at[m - 1]
            lo = r0 + SUBLANES * a
            acc = acc + wdw_ref[j:j + 1, cs] * src[lo:lo + rb, cs]
        y_ref[r0:r0 + rb, cs] = acc + bdw_ref[:, cs]

    shifts = [functools.partial(shift_stage, m) for m in range(1, SUBLANES)]
    taps = [functools.partial(tap_stage, slice(c * LANES, (c + 1) * LANES), r0)
            for c in range(CONV_WIDTH // LANES) for r0 in range(0, rows, rb)]
    return shifts, taps


def _conv_time_major(ut_ref, y_ref, steps, wdw_ref, bdw_ref):
    for c in range(CONV_WIDTH // LANES):
        cs = slice(c * LANES, (c + 1) * LANES)
        for t0 in range(0, steps, CONV_STEPS):
            acc = jnp.zeros((CONV_STEPS, ut_ref.shape[2], LANES), f32)
            for j in range(CONV_KERNEL):
                acc = acc + wdw_ref[j:j + 1, cs] * ut_ref[c, t0 + j:t0 + j + CONV_STEPS]
            y_ref[c, t0:t0 + CONV_STEPS] = acc + bdw_ref[:, cs]


def _ln_swish_tiles(y_tiles, lng_ref, lnb_ref):
    n = len(y_tiles) * LANES
    mu = sum(jnp.sum(y, axis=-1, keepdims=True) for y in y_tiles) / n
    yc = [y - mu for y in y_tiles]
    var = sum(jnp.sum(y * y, axis=-1, keepdims=True) for y in yc) / n
    inv = lax.rsqrt(var + LN_EPS)
    out = []
    for c, y in enumerate(yc):
        cs = slice(c * LANES, (c + 1) * LANES)
        yn = y * inv * lng_ref[:, cs] + lnb_ref[:, cs]
        out.append(yn * jax.nn.sigmoid(yn))
    return out


def _ln_swish(y, lng_ref, lnb_ref):
    mu = jnp.mean(y, axis=-1, keepdims=True)
    yc = y - mu
    yn = yc * lax.rsqrt(jnp.mean(yc * yc, axis=-1, keepdims=True) + LN_EPS)
    yn = yn * lng_ref[...] + lnb_ref[...]
    return yn * jax.nn.sigmoid(yn)


def _pipeline(mxu_stages, valu_stages):
    pending = mxu_stages[0][0]()
    done = 0
    for n, (_, consume) in enumerate(mxu_stages):
        cur = pending
        if n + 1 < len(mxu_stages):
            pending = mxu_stages[n + 1][0]()
        upto = (n + 1) * len(valu_stages) // len(mxu_stages)
        for stage in valu_stages[done:upto]:
            stage()
        done = upto
        consume(cur)


def _inproj_prompt_kernel(x_ref, g_ref, w32_ref, wdw_ref, bdw_ref, lng_ref, lnb_ref, *refs,
                          n_cast):
    cast_in, refs = refs[:n_cast], refs[n_cast:]
    qe_ref, qo_ref, k_ref, v_ref, oc_ref, kf_ref, vf_ref, ct_ref = refs[:8]
    cast_out, (w_ref, h_ref, z_ref, ut_ref, y_ref) = refs[8:8 + n_cast], refs[8 + n_cast:]
    for src, dst in zip(cast_in, cast_out):
        dst[...] = src[...].astype(bf16)
    nb = x_ref.shape[0]
    rows = nb * TT_PROJ

    @pl.when(pl.program_id(0) == 0)
    def _():
        _cast_weight(w32_ref, w_ref)
        ut_ref[:, 0:CONV_PAST] = jnp.zeros((CONV_WIDTH // LANES, CONV_PAST, nb, LANES), f32)

    h_ref[...] = _rmsnorm(x_ref[...].reshape(rows, D_MODEL), g_ref[...]).astype(bf16)

    def store(ref):
        def put(cs, val):
            ref[:, :, cs] = val.reshape(nb, TT_PROJ, val.shape[-1])
        return put

    def kv_store(ref, full_ref):
        def put(cs, val):
            val = val.reshape(nb, TT_PROJ, val.shape[-1])
            ref[:, :, cs] = val.astype(bf16)
            full_ref[:, :, cs] = val
        return put

    u = _glu(h_ref, w_ref)
    for issue, consume in _project_stages(h_ref, w_ref, store(qe_ref), store(qo_ref),
                                          kv_store(k_ref, kf_ref), kv_store(v_ref, vf_ref)):
        consume(issue())
    lane_tiles = [slice(c * LANES, (c + 1) * LANES) for c in range(CONV_WIDTH // LANES)]
    for c, cs in enumerate(lane_tiles):
        for b in range(nb):
            z_ref[c, Z_STRIDE * b:Z_STRIDE * b + TT_PROJ, :] = u[b * TT_PROJ:(b + 1) * TT_PROJ, cs]
            ct_ref[b, :, cs] = z_ref[c, Z_STRIDE * b + TT_PROJ - CONV_PAST:Z_STRIDE * b + TT_PROJ, :]
        for t in range(TT_PROJ):
            ut_ref[c, CONV_PAST + t] = z_ref[c, pl.ds(t, nb, stride=Z_STRIDE), :]
    _conv_time_major(ut_ref, y_ref, TT_PROJ, wdw_ref, bdw_ref)
    oc = _ln_swish_tiles([y_ref[c] for c in range(len(lane_tiles))], lng_ref, lnb_ref)
    for c, cs in enumerate(lane_tiles):
        for t in range(TT_PROJ):
            z_ref[c, pl.ds(t, nb, stride=Z_STRIDE), :] = oc[c][t]
        for b in range(nb):
            oc_ref[b, :, cs] = z_ref[c, Z_STRIDE * b:Z_STRIDE * b + TT_PROJ, :].astype(bf16)

    ut_ref[:, 0:CONV_PAST] = ut_ref[:, TT_PROJ:TT_PROJ + CONV_PAST]


def _inproj_prompt(x, g, w, wdw, bdw, lng, lnb, cast_weights):
    B, S, D = x.shape
    assert S % TT_PROJ == 0 and W_BAND % TT_PROJ == 0 and TT_PROJ >= CONV_PAST
    steps = S // TT_PROJ
    first_kept = (S - W_BAND) // TT_PROJ
    tile = lambda n: pl.BlockSpec((B, TT_PROJ, n), lambda i: (0, i, 0))
    kept = pl.BlockSpec((B, TT_PROJ, ATTN_WIDTH), lambda i: (0, jnp.maximum(i - first_kept, 0), 0))
    act = jax.ShapeDtypeStruct((B, S, ATTN_WIDTH), bf16)
    rows = B * TT_PROJ

    def cast_spec(a):
        blk = next(r for r in (32, 64, 128, 256) if a.shape[0] % r == 0 and a.shape[0] // r <= steps)
        last = a.shape[0] // blk - 1
        return pl.BlockSpec((blk, a.shape[1]), lambda i: (jnp.minimum(i, last), 0))

    cast_specs = [cast_spec(a) for a in cast_weights]
    return pl.pallas_call(
        functools.partial(_inproj_prompt_kernel, n_cast=len(cast_weights)),
        grid=(steps,),
        in_specs=[tile(D)] + [_const_spec(a.shape) for a in (g, w, wdw, bdw, lng, lnb)] + cast_specs,
        out_specs=[tile(ATTN_WIDTH)] * 5 + [kept] * 2 + [_const_spec((B, CONV_PAST, CONV_WIDTH))]
                  + cast_specs,
        out_shape=[act] * 5 + [jax.ShapeDtypeStruct((B, W_BAND, ATTN_WIDTH), f32)] * 2
                  + [jax.ShapeDtypeStruct((B, CONV_PAST, CONV_WIDTH), f32)]
                  + [jax.ShapeDtypeStruct(a.shape, bf16) for a in cast_weights],
        scratch_shapes=[pltpu.VMEM(w.shape, bf16),
                        pltpu.VMEM((rows, D), bf16),
                        pltpu.VMEM((CONV_WIDTH // LANES, B * Z_STRIDE, LANES), f32),
                        pltpu.VMEM((CONV_WIDTH // LANES, CONV_PAST + TT_PROJ, B, LANES), f32),
                        pltpu.VMEM((CONV_WIDTH // LANES, TT_PROJ, B, LANES), f32)],
        compiler_params=pltpu.CompilerParams(
            dimension_semantics=("arbitrary",), vmem_limit_bytes=VMEM_LIMIT),
        name="inproj_prompt",
    )(x, g, w, wdw, bdw, lng, lnb, *cast_weights)


def _inproj_sample_kernel(x_ref, st_ref, g_ref, w32_ref, wdw_ref, bdw_ref, lng_ref, lnb_ref,
                          qe_ref, qo_ref, k_ref, v_ref, oc_ref, nc_ref,
                          w_ref, h_ref, cbuf_ref, sh_ref, y_ref):
    B, T, _ = oc_ref.shape
    _cast_weight(w32_ref, w_ref)
    h_ref[...] = _rmsnorm(x_ref[...], g_ref[...]).astype(bf16)
    u = _glu(h_ref, w_ref)

    def store(ref):
        def put(cs, val):
            ref[:, cs] = val
        return put

    proj = _project_stages(h_ref, w_ref, store(qe_ref), store(qo_ref), store(k_ref), store(v_ref))
    lead = CONV_LEAD - CONV_PAST

    def conv_stage(b):
        cbuf_ref[0:SUBLANES, :] = jnp.zeros((SUBLANES, CONV_WIDTH), f32)
        cbuf_ref[lead:CONV_LEAD, :] = st_ref[b]
        cbuf_ref[CONV_LEAD:CONV_LEAD + T, :] = u[b * T:(b + 1) * T]
        nc_ref[b] = cbuf_ref[CONV_LEAD + T - CONV_PAST:CONV_LEAD + T, :]
        shifts, taps = _conv_stages(cbuf_ref, sh_ref, y_ref, T, wdw_ref, bdw_ref)
        for stage in shifts + taps:
            stage()
        oc_ref[b] = _ln_swish(y_ref[...], lng_ref, lnb_ref).astype(bf16)

    _pipeline(proj, [functools.partial(conv_stage, b) for b in range(B)])


def _inproj_sample(x, st, g, w, wdw, bdw, lng, lnb):
    N, D = x.shape
    B = st.shape[0]
    T = N // B
    assert T >= CONV_PAST and T % SUBLANES == 0
    act = lambda dt: jax.ShapeDtypeStruct((N, ATTN_WIDTH), dt)
    return pl.pallas_call(
        _inproj_sample_kernel,
        grid=(1,),
        in_specs=[_const_spec(a.shape) for a in (x, st, g, w, wdw, bdw, lng, lnb)],
        out_specs=[_const_spec((N, ATTN_WIDTH))] * 4
                  + [_const_spec((B, T, CONV_WIDTH)), _const_spec((B, CONV_PAST, CONV_WIDTH))],
        out_shape=[act(bf16)] * 2 + [act(f32)] * 2
                  + [jax.ShapeDtypeStruct((B, T, CONV_WIDTH), bf16),
                     jax.ShapeDtypeStruct((B, CONV_PAST, CONV_WIDTH), f32)],
        scratch_shapes=[pltpu.VMEM(w.shape, bf16),
                        pltpu.VMEM((N, D), bf16),
                        pltpu.VMEM((CONV_LEAD + T, CONV_WIDTH), f32),
                        pltpu.VMEM((SUBLANES - 1, CONV_LEAD + T - SUBLANES, CONV_WIDTH), f32),
                        pltpu.VMEM((T, CONV_WIDTH), f32)],
        compiler_params=pltpu.CompilerParams(vmem_limit_bytes=VMEM_LIMIT),
        name="inproj_sample",
    )(x, st, g, w, wdw, bdw, lng, lnb)


def _toeplitz(frow, rows):
    return pltpu.roll(jnp.broadcast_to(frow, (rows, F_LEN)), 0, 1, stride=1, stride_axis=0)


def _softmax_parts(s_parts):
    m = functools.reduce(jnp.maximum, [jnp.max(s, axis=-1, keepdims=True) for s in s_parts])
    e_parts = [jnp.exp(s - m) for s in s_parts]
    l = functools.reduce(jnp.add, [jnp.sum(e, axis=-1, keepdims=True) for e in e_parts])
    return e_parts, l


def _mix_prompt_kernel(qe_ref, qo_ref, *refs):
    k_refs, v_refs = refs[:N_KBLK], refs[N_KBLK:2 * N_KBLK]
    oc_ref, x_ref, f_ref, wout_ref, o_ref, bias_ref, cat_ref = refs[2 * N_KBLK:]
    i = pl.program_id(1)

    @pl.when((pl.program_id(0) == 0) & (i == 0))
    def _():
        for h in range(N_HEADS):
            p, par = divmod(h, 2)
            bias_ref[p, par * SUB:(par + 1) * SUB, :] = _toeplitz(f_ref[h:h + 1, :], SUB)[:, :WIN]

    row = lax.broadcasted_iota(jnp.int32, (2 * SUB, LANES), 0)
    lane = lax.broadcasted_iota(jnp.int32, (2 * SUB, LANES), 1)
    odd_chunk = (row & CHUNK) != 0
    edge_ok = {0: jnp.logical_not(odd_chunk & (lane < CHUNK)),
               WIN // LANES - 1: odd_chunk | (lane < CHUNK)}
    blk_ok = [i >= N_KBLK - 1 - b for b in range(N_KBLK - 1)]
    even_head = lax.broadcasted_iota(jnp.int32, (SUB, LANES), 1) < HEAD_DIM

    def pieces(r):
        out = []
        for b in range(N_KBLK):
            lo, hi = max(SUB * r, TQ * b), min(SUB * r + WIN, TQ * (b + 1))
            if lo < hi:
                out.append((b, lo - TQ * b, hi - TQ * b))
        return out

    def scores(r, p):
        rs, ps = slice(r * SUB, (r + 1) * SUB), slice(p * LANES, (p + 1) * LANES)
        q2 = jnp.concatenate([qe_ref[0, rs, ps], qo_ref[0, rs, ps]], axis=0)
        return jnp.concatenate(
            [_dot_nt(q2, k_refs[b][0, a:z, ps]) for b, a, z in pieces(r)], axis=1)

    def finish(r, p, s):
        rs, ps = slice(r * SUB, (r + 1) * SUB), slice(p * LANES, (p + 1) * LANES)
        s = s + bias_ref[p]
        tiles = []
        for t in range(WIN // LANES):
            st = s[:, t * LANES:(t + 1) * LANES]
            b = (SUB * r + LANES * t) // TQ
            ok = edge_ok.get(t)
            if b < N_KBLK - 1:
                ok = blk_ok[b] if ok is None else ok & blk_ok[b]
            tiles.append(st if ok is None else jnp.where(ok, st, NEG_INF))
        (e,), l = _softmax_parts([jnp.concatenate(tiles, axis=1)])
        pb = e.astype(bf16)
        o, c0 = None, 0
        for b, a, z in pieces(r):
            ob = _dot(pb[:, c0:c0 + z - a], v_refs[b][0, a:z, ps])
            o = ob if o is None else o + ob
            c0 += z - a
        o = o / l
        cat_ref[rs, ps] = jnp.where(even_head, o[:SUB], o[SUB:]).astype(bf16)

    units = [(r, p) for r in range(TQ // SUB) for p in range(N_PAIRS)]
    pending = [scores(*u) for u in units[:LOOKAHEAD]]
    for n, unit in enumerate(units):
        if n + LOOKAHEAD < len(units):
            pending.append(scores(*units[n + LOOKAHEAD]))
        finish(*unit, pending.pop(0))

    o_ref[0] = (x_ref[0] + _dot(cat_ref[...], wout_ref[0:ATTN_WIDTH, :])
                + _dot(oc_ref[0], wout_ref[ATTN_WIDTH:, :]))


def _mix_prompt(qe, qo, k, v, oc, x, fvec, wout):
    B, S, D = x.shape
    assert S % TQ == 0 and W_BAND % TQ == 0 and TQ % SUB == 0 and WIN % LANES == 0
    qspec = pl.BlockSpec((1, TQ, ATTN_WIDTH), lambda b, i: (b, i, 0))
    kspec = lambda back: pl.BlockSpec(
        (1, TQ, ATTN_WIDTH), lambda b, i: (b, jnp.maximum(i - back, 0), 0))
    kspecs = [kspec(N_KBLK - 1 - n) for n in range(N_KBLK)]
    xspec = pl.BlockSpec((1, TQ, D), lambda b, i: (b, i, 0))
    return pl.pallas_call(
        _mix_prompt_kernel,
        grid=(B, S // TQ),
        in_specs=[qspec, qspec] + kspecs + kspecs + [qspec, xspec,
                  _const_spec(fvec.shape), _const_spec(wout.shape)],
        out_specs=xspec,
        out_shape=jax.ShapeDtypeStruct((B, S, D), f32),
        scratch_shapes=[pltpu.VMEM((N_PAIRS, 2 * SUB, WIN), f32),
                        pltpu.VMEM((TQ, ATTN_WIDTH), bf16)],
        compiler_params=pltpu.CompilerParams(
            dimension_semantics=("arbitrary", "arbitrary"), vmem_limit_bytes=VMEM_LIMIT),
        name="mix_prompt",
    )(qe, qo, *([k] * N_KBLK), *([v] * N_KBLK), oc, x, fvec, wout)


def _mix_sample_kernel(qe_ref, qo_ref, kn_ref, vn_ref, ck_ref, cv_ref, oc_ref, x_ref,
                       f_ref, wout_ref, o_ref, nk_ref, nv_ref, bias_ref, cat_ref):
    T = oc_ref.shape[1]
    W = ck_ref.shape[1]

    @pl.when(pl.program_id(0) == 0)
    def _():
        for h in range(N_HEADS):
            p, par = divmod(h, 2)
            bias_ref[p, par * T:(par + 1) * T, :] = _toeplitz(f_ref[h:h + 1, :], T)

    nk_ref[0, 0:W - T, :] = ck_ref[0, T:W, :]
    nk_ref[0, W - T:W, :] = kn_ref[0]
    nv_ref[0, 0:W - T, :] = cv_ref[0, T:W, :]
    nv_ref[0, W - T:W, :] = vn_ref[0]

    lane = lax.broadcasted_iota(jnp.int32, (T, LANES), 1)
    even_head = lane < HEAD_DIM
    for p in range(N_PAIRS):
        ps = slice(p * LANES, (p + 1) * LANES)
        kc = ck_ref[0, :, ps].astype(bf16)
        vc = cv_ref[0, :, ps].astype(bf16)
        kn = kn_ref[0, :, ps].astype(bf16)
        vn = vn_ref[0, :, ps].astype(bf16)
        q2 = jnp.concatenate([qe_ref[0, :, ps], qo_ref[0, :, ps]], axis=0)
        (ec, en), l = _softmax_parts([_dot_nt(q2, kc) + bias_ref[p, :, 0:W],
                                      _dot_nt(q2, kn) + bias_ref[p, :, W:W + T]])
        o = (_dot(ec.astype(bf16), vc) + _dot(en.astype(bf16), vn)) / l
        cat_ref[:, ps] = jnp.where(even_head, o[:T], o[T:]).astype(bf16)

    o_ref[0] = (x_ref[0] + _dot(cat_ref[...], wout_ref[0:ATTN_WIDTH, :])
                + _dot(oc_ref[0], wout_ref[ATTN_WIDTH:, :]))


def _mix_sample(qe, qo, kn, vn, ck, cv, oc, x, fvec, wout):
    B, T, D = x.shape
    W = ck.shape[1]
    assert W == W_BAND and W + T <= F_LEN - T
    row = lambda r, n: pl.BlockSpec((1, r, n), lambda b: (b, 0, 0))
    return pl.pallas_call(
        _mix_sample_kernel,
        grid=(B,),
        in_specs=[row(T, ATTN_WIDTH)] * 4 + [row(W, ATTN_WIDTH)] * 2
                 + [row(T, CONV_WIDTH), row(T, D), _const_spec(fvec.shape), _const_spec(wout.shape)],
        out_specs=[row(T, D), row(W, ATTN_WIDTH), row(W, ATTN_WIDTH)],
        out_shape=[jax.ShapeDtypeStruct((B, T, D), f32),
                   jax.ShapeDtypeStruct((B, W, ATTN_WIDTH), f32),
                   jax.ShapeDtypeStruct((B, W, ATTN_WIDTH), f32)],
        scratch_shapes=[pltpu.VMEM((N_PAIRS, 2 * T, F_LEN), f32),
                        pltpu.VMEM((T, ATTN_WIDTH), bf16)],
        compiler_params=pltpu.CompilerParams(
            dimension_semantics=("arbitrary",), vmem_limit_bytes=VMEM_LIMIT),
        name="mix_sample",
    )(qe, qo, kn, vn, ck, cv, oc, x, fvec, wout)


def _ffn_kernel(x_ref, g_ref, wg_ref, wu_ref, wd_ref, gf_ref, o_ref):
    x = x_ref[...]
    h = _rmsnorm(x, g_ref[...]).astype(bf16)
    acc = x
    for c0 in range(0, D_FF, FFN_COLS):
        cs = slice(c0, min(c0 + FFN_COLS, D_FF))
        a = jax.nn.silu(_dot(h, wg_ref[:, cs])) * _dot(h, wu_ref[:, cs])
        acc = acc + _dot(a.astype(bf16), wd_ref[cs, :])
    o_ref[...] = _rmsnorm(acc, gf_ref[...])


def _ffn(x, g, wg, wu, wd, gf, tm):
    N, D = x.shape
    assert N % tm == 0
    tile = pl.BlockSpec((tm, D), lambda i: (i, 0))
    return pl.pallas_call(
        _ffn_kernel,
        grid=(N // tm,),
        in_specs=[tile, _const_spec((1, D)), _const_spec(wg.shape), _const_spec(wu.shape),
                  _const_spec(wd.shape), _const_spec((1, D))],
        out_specs=tile,
        out_shape=jax.ShapeDtypeStruct((N, D), f32),
        compiler_params=pltpu.CompilerParams(
            dimension_semantics=("arbitrary",), vmem_limit_bytes=VMEM_LIMIT),
        name="ffn",
    )(x, g, wg, wu, wd, gf)


def _rel_bias_row(table):
    n_rel = table.shape[1]
    far = W_BAND - REL_CLIP
    assert n_rel == 2 * REL_CLIP + 1 and far + n_rel + SUB <= F_LEN
    rep = lambda col, n: jnp.broadcast_to(table[:, col:col + 1], (table.shape[0], n))
    return jnp.concatenate([rep(n_rel - 1, far), table[:, ::-1],
                            rep(0, F_LEN - SUB - far - n_rel), rep(n_rel - 1, SUB)], axis=1)


def kernel(x_prompt, x_sample, cache_k, cache_v, state_conv, g_mix, w_in, rel_table, w_dw, b_dw,
           ln_g, ln_b, w_out, g_ffn, w_gate, w_up, w_down, g_final):
    assert g_mix.shape[0] == 1, "single-layer trunk: the final RMSNorm is fused into the FFN kernel"
    B, S, D = x_prompt.shape
    Bs, T, _ = x_sample.shape
    W = cache_k.shape[2]
    gm, gff, gf = g_mix[0].reshape(1, D), g_ffn[0].reshape(1, D), g_final.reshape(1, D)
    win = w_in[0]
    conv_params = (w_dw[0],) + tuple(a[0].reshape(1, CONV_WIDTH) for a in (b_dw, ln_g, ln_b))
    heads = lambda a: a.reshape(1, a.shape[0], a.shape[1], N_HEADS, HEAD_DIM)
    fvec = _rel_bias_row(rel_table[0])

    qe, qo, k, v, oc, kf, vf, ct, wout, wg, wu, wd = _inproj_prompt(
        x_prompt, gm, win, *conv_params, (w_out[0], w_gate[0], w_up[0], w_down[0]))
    x1 = _mix_prompt(qe, qo, k, v, oc, x_prompt, fvec, wout)
    y_prompt = _ffn(x1.reshape(B * S, D), gff, wg, wu, wd, gf, TM_FFN).reshape(B, S, D)

    qe, qo, kn, vn, oc, nc = _inproj_sample(
        x_sample.reshape(Bs * T, D), state_conv[0], gm, win, *conv_params)
    r3 = lambda a: a.reshape(Bs, T, ATTN_WIDTH)
    x1, nk, nv = _mix_sample(
        r3(qe), r3(qo), r3(kn), r3(vn),
        cache_k[0].reshape(Bs, W, ATTN_WIDTH), cache_v[0].reshape(Bs, W, ATTN_WIDTH),
        oc, x_sample, fvec, wout)
    y_sample = _ffn(x1.reshape(Bs * T, D), gff, wg, wu, wd, gf, Bs * T).reshape(Bs, T, D)

    return (y_prompt, y_sample, heads(kf), heads(vf), ct[None],
            heads(nk), heads(nv), nc[None])
```

```python
import functools

import jax
import jax.numpy as jnp
from jax import lax
from jax.experimental import pallas as pl
from jax.experimental.pallas import tpu as pltpu

D_MODEL = 1024
CHUNK = 64
LEFT_CHUNKS = 8
W_BAND = LEFT_CHUNKS * CHUNK
ATTN_WIDTH = 512
N_HEADS = 8
HEAD_DIM = 64
CONV_WIDTH = 512
CONV_KERNEL = 31
CONV_PAST = CONV_KERNEL - 1
REL_CLIP = 128
D_FF = 2816
RMS_EPS = 1e-6
LN_EPS = 1e-5
NEG_INF = -1e30
SCALE = HEAD_DIM ** -0.5

LANES = 128
SUBLANES = 8
N_PAIRS = ATTN_WIDTH // LANES
TT_PROJ = 64
Z_STRIDE = TT_PROJ + SUBLANES
PROJ_COLS = 256
TQ = 512
N_KBLK = W_BAND // TQ + 1
SUB = 2 * CHUNK
WIN = W_BAND + SUB
F_LEN = 1024
LOOKAHEAD = 1
TM_FFN = 512
MXU_TILE = 256
FFN_COLS = 4 * MXU_TILE
CONV_ROWS = 64
CONV_STEPS = 16
CONV_LEAD = 32
VMEM_LIMIT = 56 * 1024 * 1024

f32 = jnp.float32
bf16 = jnp.bfloat16


def _rmsnorm(x, g):
    return (x * lax.rsqrt(jnp.mean(x * x, axis=-1, keepdims=True) + RMS_EPS)) * g


def _dot(a, b):
    return jnp.dot(a, b, preferred_element_type=f32)


def _dot_nt(a, b):
    return lax.dot_general(a, b, (((1,), (1,)), ((), ())), preferred_element_type=f32)


def _const_spec(shape):
    nd = len(shape)
    return pl.BlockSpec(shape, lambda *_: (0,) * nd, pipeline_mode=pl.Buffered(1))


def _project_stages(h_ref, w_ref, qe_store, qo_store, k_store, v_store):
    def issue(base, cs):
        return _dot(h_ref[...], w_ref[:, base + cs.start:base + cs.stop])

    def q_finish(cs, z):
        q = z * SCALE
        lane = lax.broadcasted_iota(jnp.int32, q.shape, 1)
        even = (lane & HEAD_DIM) == 0
        qe_store(cs, jnp.where(even, q, 0.0).astype(bf16))
        qo_store(cs, jnp.where(even, 0.0, q).astype(bf16))

    stages = []
    for base, finish in ((0, q_finish), (ATTN_WIDTH, k_store), (2 * ATTN_WIDTH, v_store)):
        for c0 in range(0, ATTN_WIDTH, PROJ_COLS):
            cs = slice(c0, c0 + PROJ_COLS)
            stages.append((functools.partial(issue, base, cs), functools.partial(finish, cs)))
    return stages


def _cast_weight(w32_ref, w_ref):
    for c0 in range(0, w_ref.shape[1], MXU_TILE):
        w_ref[:, c0:c0 + MXU_TILE] = w32_ref[:, c0:c0 + MXU_TILE].astype(bf16)


def _glu(h_ref, w_ref):
    base = 3 * ATTN_WIDTH
    a = _dot(h_ref[...], w_ref[:, base:base + CONV_WIDTH])
    g = _dot(h_ref[...], w_ref[:, base + CONV_WIDTH:base + 2 * CONV_WIDTH])
    return a * jax.nn.sigmoid(g)


def _conv_stages(cbuf_ref, sh_ref, y_ref, rows, wdw_ref, bdw_ref):
    lead = CONV_LEAD - CONV_PAST
    span = rows + CONV_LEAD - SUBLANES
    rb = min(CONV_ROWS, rows)

    def shift_stage(m):
        sh_ref[m - 1, 0:span, :] = cbuf_ref[m:m + span, :]

    def tap_stage(cs, r0):
        acc = jnp.zeros((rb, LANES), f32)
        for j in range(CONV_KERNEL):
            a, m = divmod(j + lead, SUBLANES)
            src = cbuf_ref if m == 0 else sh_ref.at[m - 1]
            lo = r0 + SUBLANES * a
            acc = acc + wdw_ref[j:j + 1, cs] * src[lo:lo + rb, cs]
        y_ref[r0:r0 + rb, cs] = acc + bdw_ref[:, cs]

    shifts = [functools.partial(shift_stage, m) for m in range(1, SUBLANES)]
    taps = [functools.partial(tap_stage, slice(c * LANES, (c + 1) * LANES), r0)
            for c in range(CONV_WIDTH // LANES) for r0 in range(0, rows, rb)]
    return shifts, taps


def _conv_time_major(ut_ref, y_ref, steps, wdw_ref, bdw_ref):
    for c in range(CONV_WIDTH // LANES):
        cs = slice(c * LANES, (c + 1) * LANES)
        for t0 in range(0, steps, CONV_STEPS):
            acc = jnp.zeros((CONV_STEPS, ut_ref.shape[2], LANES), f32)
            for j in range(CONV_KERNEL):
                acc = acc + wdw_ref[j:j + 1, cs] * ut_ref[c, t0 + j:t0 + j + CONV_STEPS]
            y_ref[c, t0:t0 + CONV_STEPS] = acc + bdw_ref[:, cs]


def _ln_swish_tiles(y_tiles, lng_ref, lnb_ref):
    n = len(y_tiles) * LANES
    mu = sum(jnp.sum(y, axis=-1, keepdims=True) for y in y_tiles) / n
    yc = [y - mu for y in y_tiles]
    var = sum(jnp.sum(y * y, axis=-1, keepdims=True) for y in yc) / n
    inv = lax.rsqrt(var + LN_EPS)
    out = []
    for c, y in enumerate(yc):
        cs = slice(c * LANES, (c + 1) * LANES)
        yn = y * inv * lng_ref[:, cs] + lnb_ref[:, cs]
        out.append(yn * jax.nn.sigmoid(yn))
    return out


def _ln_swish(y, lng_ref, lnb_ref):
    mu = jnp.mean(y, axis=-1, keepdims=True)
    yc = y - mu
    yn = yc * lax.rsqrt(jnp.mean(yc * yc, axis=-1, keepdims=True) + LN_EPS)
    yn = yn * lng_ref[...] + lnb_ref[...]
    return yn * jax.nn.sigmoid(yn)


def _pipeline(mxu_stages, valu_stages):
    pending = mxu_stages[0][0]()
    done = 0
    for n, (_, consume) in enumerate(mxu_stages):
        cur = pending
        if n + 1 < len(mxu_stages):
            pending = mxu_stages[n + 1][0]()
        upto = (n + 1) * len(valu_stages) // len(mxu_stages)
        for stage in valu_stages[done:upto]:
            stage()
        done = upto
        consume(cur)


def _inproj_prompt_kernel(x_ref, g_ref, w32_ref, wdw_ref, bdw_ref, lng_ref, lnb_ref, *refs,
                          n_cast):
    cast_in, refs = refs[:n_cast], refs[n_cast:]
    qe_ref, qo_ref, k_ref, v_ref, oc_ref, kf_ref, vf_ref, ct_ref = refs[:8]
    cast_out, (w_ref, h_ref, z_ref, ut_ref, y_ref) = refs[8:8 + n_cast], refs[8 + n_cast:]
    for src, dst in zip(cast_in, cast_out):
        dst[...] = src[...].astype(bf16)
    nb = x_ref.shape[0]
    rows = nb * TT_PROJ

    @pl.when(pl.program_id(0) == 0)
    def _():
        _cast_weight(w32_ref, w_ref)
        ut_ref[:, 0:CONV_PAST] = jnp.zeros((CONV_WIDTH // LANES, CONV_PAST, nb, LANES), f32)

    h_ref[...] = _rmsnorm(x_ref[...].reshape(rows, D_MODEL), g_ref[...]).astype(bf16)

    def store(ref):
        def put(cs, val):
            ref[:, :, cs] = val.reshape(nb, TT_PROJ, val.shape[-1])
        return put

    def kv_store(ref, full_ref):
        def put(cs, val):
            val = val.reshape(nb, TT_PROJ, val.shape[-1])
            ref[:, :, cs] = val.astype(bf16)
            full_ref[:, :, cs] = val
        return put

    u = _glu(h_ref, w_ref)
    for issue, consume in _project_stages(h_ref, w_ref, store(qe_ref), store(qo_ref),
                                          kv_store(k_ref, kf_ref), kv_store(v_ref, vf_ref)):
        consume(issue())
    lane_tiles = [slice(c * LANES, (c + 1) * LANES) for c in range(CONV_WIDTH // LANES)]
    for c, cs in enumerate(lane_tiles):
        for b in range(nb):
            z_ref[c, Z_STRIDE * b:Z_STRIDE * b + TT_PROJ, :] = u[b * TT_PROJ:(b + 1) * TT_PROJ, cs]
            ct_ref[b, :, cs] = z_ref[c, Z_STRIDE * b + TT_PROJ - CONV_PAST:Z_STRIDE * b + TT_PROJ, :]
        for t in range(TT_PROJ):
            ut_ref[c, CONV_PAST + t] = z_ref[c, pl.ds(t, nb, stride=Z_STRIDE), :]
    _conv_time_major(ut_ref, y_ref, TT_PROJ, wdw_ref, bdw_ref)
    oc = _ln_swish_tiles([y_ref[c] for c in range(len(lane_tiles))], lng_ref, lnb_ref)
    for c, cs in enumerate(lane_tiles):
        for t in range(TT_PROJ):
            z_ref[c, pl.ds(t, nb, stride=Z_STRIDE), :] = oc[c][t]
        for b in range(nb):
            oc_ref[b, :, cs] = z_ref[c, Z_STRIDE * b:Z_STRIDE * b + TT_PROJ, :].astype(bf16)

    ut_ref[:, 0:CONV_PAST] = ut_ref[:, TT_PROJ:TT_PROJ + CONV_PAST]


def _inproj_prompt(x, g, w, wdw, bdw, lng, lnb, cast_weights):
    B, S, D = x.shape
    assert S % TT_PROJ == 0 and W_BAND % TT_PROJ == 0 and TT_PROJ >= CONV_PAST
    steps = S // TT_PROJ
    first_kept = (S - W_BAND) // TT_PROJ
    tile = lambda n: pl.BlockSpec((B, TT_PROJ, n), lambda i: (0, i, 0))
    kept = pl.BlockSpec((B, TT_PROJ, ATTN_WIDTH), lambda i: (0, jnp.maximum(i - first_kept, 0), 0))
    act = jax.ShapeDtypeStruct((B, S, ATTN_WIDTH), bf16)
    rows = B * TT_PROJ

    def cast_spec(a):
        blk = next(r for r in (32, 64, 128, 256) if a.shape[0] % r == 0 and a.shape[0] // r <= steps)
        last = a.shape[0] // blk - 1
        return pl.BlockSpec((blk, a.shape[1]), lambda i: (jnp.minimum(i, last), 0))

    cast_specs = [cast_spec(a) for a in cast_weights]
    return pl.pallas_call(
        functools.partial(_inproj_prompt_kernel, n_cast=len(cast_weights)),
        grid=(steps,),
        in_specs=[tile(D)] + [_const_spec(a.shape) for a in (g, w, wdw, bdw, lng, lnb)] + cast_specs,
        out_specs=[tile(ATTN_WIDTH)] * 5 + [kept] * 2 + [_const_spec((B, CONV_PAST, CONV_WIDTH))]
                  + cast_specs,
        out_shape=[act] * 5 + [jax.ShapeDtypeStruct((B, W_BAND, ATTN_WIDTH), f32)] * 2
                  + [jax.ShapeDtypeStruct((B, CONV_PAST, CONV_WIDTH), f32)]
                  + [jax.ShapeDtypeStruct(a.shape, bf16) for a in cast_weights],
        scratch_shapes=[pltpu.VMEM(w.shape, bf16),
                        pltpu.VMEM((rows, D), bf16),
                        pltpu.VMEM((CONV_WIDTH // LANES, B * Z_STRIDE, LANES), f32),
                        pltpu.VMEM((CONV_WIDTH // LANES, CONV_PAST + TT_PROJ, B, LANES), f32),
                        pltpu.VMEM((CONV_WIDTH // LANES, TT_PROJ, B, LANES), f32)],
        compiler_params=pltpu.CompilerParams(
            dimension_semantics=("arbitrary",), vmem_limit_bytes=VMEM_LIMIT),
        name="inproj_prompt",
    )(x, g, w, wdw, bdw, lng, lnb, *cast_weights)


def _inproj_sample_kernel(x_ref, st_ref, g_ref, w32_ref, wdw_ref, bdw_ref, lng_ref, lnb_ref,
                          qe_ref, qo_ref, k_ref, v_ref, oc_ref, nc_ref,
                          w_ref, h_ref, cbuf_ref, sh_ref, y_ref):
    B, T, _ = oc_ref.shape
    _cast_weight(w32_ref, w_ref)
    h_ref[...] = _rmsnorm(x_ref[...], g_ref[...]).astype(bf16)
    u = _glu(h_ref, w_ref)

    def store(ref):
        def put(cs, val):
            ref[:, cs] = val
        return put

    proj = _project_stages(h_ref, w_ref, store(qe_ref), store(qo_ref), store(k_ref), store(v_ref))
    lead = CONV_LEAD - CONV_PAST

    def conv_stage(b):
        cbuf_ref[0:SUBLANES, :] = jnp.zeros((SUBLANES, CONV_WIDTH), f32)
        cbuf_ref[lead:CONV_LEAD, :] = st_ref[b]
        cbuf_ref[CONV_LEAD:CONV_LEAD + T, :] = u[b * T:(b + 1) * T]
        nc_ref[b] = cbuf_ref[CONV_LEAD + T - CONV_PAST:CONV_LEAD + T, :]
        shifts, taps = _conv_stages(cbuf_ref, sh_ref, y_ref, T, wdw_ref, bdw_ref)
        for stage in shifts + taps:
            stage()
        oc_ref[b] = _ln_swish(y_ref[...], lng_ref, lnb_ref).astype(bf16)

    _pipeline(proj, [functools.partial(conv_stage, b) for b in range(B)])


def _inproj_sample(x, st, g, w, wdw, bdw, lng, lnb):
    N, D = x.shape
    B = st.shape[0]
    T = N // B
    assert T >= CONV_PAST and T % SUBLANES == 0
    act = lambda dt: jax.ShapeDtypeStruct((N, ATTN_WIDTH), dt)
    return pl.pallas_call(
        _inproj_sample_kernel,
        grid=(1,),
        in_specs=[_const_spec(a.shape) for a in (x, st, g, w, wdw, bdw, lng, lnb)],
        out_specs=[_const_spec((N, ATTN_WIDTH))] * 4
                  + [_const_spec((B, T, CONV_WIDTH)), _const_spec((B, CONV_PAST, CONV_WIDTH))],
        out_shape=[act(bf16)] * 2 + [act(f32)] * 2
                  + [jax.ShapeDtypeStruct((B, T, CONV_WIDTH), bf16),
                     jax.ShapeDtypeStruct((B, CONV_PAST, CONV_WIDTH), f32)],
        scratch_shapes=[pltpu.VMEM(w.shape, bf16),
                        pltpu.VMEM((N, D), bf16),
                        pltpu.VMEM((CONV_LEAD + T, CONV_WIDTH), f32),
                        pltpu.VMEM((SUBLANES - 1, CONV_LEAD + T - SUBLANES, CONV_WIDTH), f32),
                        pltpu.VMEM((T, CONV_WIDTH), f32)],
        compiler_params=pltpu.CompilerParams(vmem_limit_bytes=VMEM_LIMIT),
        name="inproj_sample",
    )(x, st, g, w, wdw, bdw, lng, lnb)


def _toeplitz(frow, rows):
    return pltpu.roll(jnp.broadcast_to(frow, (rows, F_LEN)), 0, 1, stride=1, stride_axis=0)


def _softmax_parts(s_parts):
    m = functools.reduce(jnp.maximum, [jnp.max(s, axis=-1, keepdims=True) for s in s_parts])
    e_parts = [jnp.exp(s - m) for s in s_parts]
    l = functools.reduce(jnp.add, [jnp.sum(e, axis=-1, keepdims=True) for e in e_parts])
    return e_parts, l


def _mix_prompt_kernel(qe_ref, qo_ref, *refs):
    k_refs, v_refs = refs[:N_KBLK], refs[N_KBLK:2 * N_KBLK]
    oc_ref, x_ref, f_ref, wout_ref, o_ref, bias_ref, cat_ref = refs[2 * N_KBLK:]
    i = pl.program_id(1)

    @pl.when((pl.program_id(0) == 0) & (i == 0))
    def _():
        for h in range(N_HEADS):
            p, par = divmod(h, 2)
            bias_ref[p, par * SUB:(par + 1) * SUB, :] = _toeplitz(f_ref[h:h + 1, :], SUB)[:, :WIN]

    row = lax.broadcasted_iota(jnp.int32, (2 * SUB, LANES), 0)
    lane = lax.broadcasted_iota(jnp.int32, (2 * SUB, LANES), 1)
    odd_chunk = (row & CHUNK) != 0
    edge_ok = {0: jnp.logical_not(odd_chunk & (lane < CHUNK)),
               WIN // LANES - 1: odd_chunk | (lane < CHUNK)}
    even_head = lax.broadcasted_iota(jnp.int32, (SUB, LANES), 1) < HEAD_DIM

    def attend(lo):
        def pieces(r):
            out = []
            for b in range(N_KBLK):
                a, z = max(SUB * r, lo, TQ * b), min(SUB * r + WIN, TQ * (b + 1))
                if a < z:
                    out.append((b, a - TQ * b, z - TQ * b))
            return out

        def scores(r, p):
            rs, ps = slice(r * SUB, (r + 1) * SUB), slice(p * LANES, (p + 1) * LANES)
            q2 = jnp.concatenate([qe_ref[0, rs, ps], qo_ref[0, rs, ps]], axis=0)
            return jnp.concatenate(
                [_dot_nt(q2, k_refs[b][0, a:z, ps]) for b, a, z in pieces(r)], axis=1)

        def finish(r, p, s):
            rs, ps = slice(r * SUB, (r + 1) * SUB), slice(p * LANES, (p + 1) * LANES)
            t_lo = max(lo - SUB * r, 0) // LANES
            s = s + bias_ref[p, :, t_lo * LANES:WIN]
            tiles = []
            for t in range(t_lo, WIN // LANES):
                st = s[:, (t - t_lo) * LANES:(t - t_lo + 1) * LANES]
                ok = edge_ok.get(t)
                tiles.append(st if ok is None else jnp.where(ok, st, NEG_INF))
            (e,), l = _softmax_parts([jnp.concatenate(tiles, axis=1)])
            pb = e.astype(bf16)
            o, c0 = None, 0
            for b, a, z in pieces(r):
                ob = _dot(pb[:, c0:c0 + z - a], v_refs[b][0, a:z, ps])
                o = ob if o is None else o + ob
                c0 += z - a
            o = o / l
            cat_ref[rs, ps] = jnp.where(even_head, o[:SUB], o[SUB:]).astype(bf16)

        units = [(r, p) for r in range(TQ // SUB) for p in range(N_PAIRS)]
        pending = [scores(*u) for u in units[:LOOKAHEAD]]
        for n, unit in enumerate(units):
            if n + LOOKAHEAD < len(units):
                pending.append(scores(*units[n + LOOKAHEAD]))
            finish(*unit, pending.pop(0))

    pl.when(i == 0)(functools.partial(attend, W_BAND))
    pl.when(i > 0)(functools.partial(attend, 0))

    o_ref[0] = (x_ref[0] + _dot(cat_ref[...], wout_ref[0:ATTN_WIDTH, :])
                + _dot(oc_ref[0], wout_ref[ATTN_WIDTH:, :]))


def _mix_prompt(qe, qo, k, v, oc, x, fvec, wout):
    B, S, D = x.shape
    assert S % TQ == 0 and TQ == W_BAND and TQ % SUB == 0 and WIN % LANES == 0
    qspec = pl.BlockSpec((1, TQ, ATTN_WIDTH), lambda b, i: (b, i, 0))
    kspec = lambda back: pl.BlockSpec(
        (1, TQ, ATTN_WIDTH), lambda b, i: (b, jnp.maximum(i - back, 0), 0))
    kspecs = [kspec(N_KBLK - 1 - n) for n in range(N_KBLK)]
    xspec = pl.BlockSpec((1, TQ, D), lambda b, i: (b, i, 0))
    return pl.pallas_call(
        _mix_prompt_kernel,
        grid=(B, S // TQ),
        in_specs=[qspec, qspec] + kspecs + kspecs + [qspec, xspec,
                  _const_spec(fvec.shape), _const_spec(wout.shape)],
        out_specs=xspec,
        out_shape=jax.ShapeDtypeStruct((B, S, D), f32),
        scratch_shapes=[pltpu.VMEM((N_PAIRS, 2 * SUB, WIN), f32),
                        pltpu.VMEM((TQ, ATTN_WIDTH), bf16)],
        compiler_params=pltpu.CompilerParams(
            dimension_semantics=("arbitrary", "arbitrary"), vmem_limit_bytes=VMEM_LIMIT),
        name="mix_prompt",
    )(qe, qo, *([k] * N_KBLK), *([v] * N_KBLK), oc, x, fvec, wout)


def _mix_sample_kernel(qe_ref, qo_ref, kn_ref, vn_ref, ck_ref, cv_ref, oc_ref, x_ref,
                       f_ref, wout_ref, o_ref, nk_ref, nv_ref, bias_ref, cat_ref):
    T = oc_ref.shape[1]
    W = ck_ref.shape[1]

    @pl.when(pl.program_id(0) == 0)
    def _():
        for h in range(N_HEADS):
            p, par = divmod(h, 2)
            bias_ref[p, par * T:(par + 1) * T, :] = _toeplitz(f_ref[h:h + 1, :], T)

    nk_ref[0, 0:W - T, :] = ck_ref[0, T:W, :]
    nk_ref[0, W - T:W, :] = kn_ref[0]
    nv_ref[0, 0:W - T, :] = cv_ref[0, T:W, :]
    nv_ref[0, W - T:W, :] = vn_ref[0]

    lane = lax.broadcasted_iota(jnp.int32, (T, LANES), 1)
    even_head = lane < HEAD_DIM
    for p in range(N_PAIRS):
        ps = slice(p * LANES, (p + 1) * LANES)
        kc = ck_ref[0, :, ps].astype(bf16)
        vc = cv_ref[0, :, ps].astype(bf16)
        kn = kn_ref[0, :, ps].astype(bf16)
        vn = vn_ref[0, :, ps].astype(bf16)
        q2 = jnp.concatenate([qe_ref[0, :, ps], qo_ref[0, :, ps]], axis=0)
        (ec, en), l = _softmax_parts([_dot_nt(q2, kc) + bias_ref[p, :, 0:W],
                                      _dot_nt(q2, kn) + bias_ref[p, :, W:W + T]])
        o = (_dot(ec.astype(bf16), vc) + _dot(en.astype(bf16), vn)) / l
        cat_ref[:, ps] = jnp.where(even_head, o[:T], o[T:]).astype(bf16)

    o_ref[0] = (x_ref[0] + _dot(cat_ref[...], wout_ref[0:ATTN_WIDTH, :])
                + _dot(oc_ref[0], wout_ref[ATTN_WIDTH:, :]))


def _mix_sample(qe, qo, kn, vn, ck, cv, oc, x, fvec, wout):
    B, T, D = x.shape
    W = ck.shape[1]
    assert W == W_BAND and W + T <= F_LEN - T
    row = lambda r, n: pl.BlockSpec((1, r, n), lambda b: (b, 0, 0))
    return pl.pallas_call(
        _mix_sample_kernel,
        grid=(B,),
        in_specs=[row(T, ATTN_WIDTH)] * 4 + [row(W, ATTN_WIDTH)] * 2
                 + [row(T, CONV_WIDTH), row(T, D), _const_spec(fvec.shape), _const_spec(wout.shape)],
        out_specs=[row(T, D), row(W, ATTN_WIDTH), row(W, ATTN_WIDTH)],
        out_shape=[jax.ShapeDtypeStruct((B, T, D), f32),
                   jax.ShapeDtypeStruct((B, W, ATTN_WIDTH), f32),
                   jax.ShapeDtypeStruct((B, W, ATTN_WIDTH), f32)],
        scratch_shapes=[pltpu.VMEM((N_PAIRS, 2 * T, F_LEN), f32),
                        pltpu.VMEM((T, ATTN_WIDTH), bf16)],
        compiler_params=pltpu.CompilerParams(
            dimension_semantics=("arbitrary",), vmem_limit_bytes=VMEM_LIMIT),
        name="mix_sample",
    )(qe, qo, kn, vn, ck, cv, oc, x, fvec, wout)


def _ffn_kernel(x_ref, g_ref, wg_ref, wu_ref, wd_ref, gf_ref, o_ref):
    x = x_ref[...]
    h = _rmsnorm(x, g_ref[...]).astype(bf16)
    acc = x
    for c0 in range(0, D_FF, FFN_COLS):
        cs = slice(c0, min(c0 + FFN_COLS, D_FF))
        a = jax.nn.silu(_dot(h, wg_ref[:, cs])) * _dot(h, wu_ref[:, cs])
        acc = acc + _dot(a.astype(bf16), wd_ref[cs, :])
    o_ref[...] = _rmsnorm(acc, gf_ref[...])


def _ffn(x, g, wg, wu, wd, gf, tm):
    N, D = x.shape
    assert N % tm == 0
    tile = pl.BlockSpec((tm, D), lambda i: (i, 0))
    return pl.pallas_call(
        _ffn_kernel,
        grid=(N // tm,),
        in_specs=[tile, _const_spec((1, D)), _const_spec(wg.shape), _const_spec(wu.shape),
                  _const_spec(wd.shape), _const_spec((1, D))],
        out_specs=tile,
        out_shape=jax.ShapeDtypeStruct((N, D), f32),
        compiler_params=pltpu.CompilerParams(
            dimension_semantics=("arbitrary",), vmem_limit_bytes=VMEM_LIMIT),
        name="ffn",
    )(x, g, wg, wu, wd, gf)


def _rel_bias_row(table):
    n_rel = table.shape[1]
    far = W_BAND - REL_CLIP
    assert n_rel == 2 * REL_CLIP + 1 and far + n_rel + SUB <= F_LEN
    rep = lambda col, n: jnp.broadcast_to(table[:, col:col + 1], (table.shape[0], n))
    return jnp.concatenate([rep(n_rel - 1, far), table[:, ::-1],
                            rep(0, F_LEN - SUB - far - n_rel), rep(n_rel - 1, SUB)], axis=1)


def kernel(x_prompt, x_sample, cache_k, cache_v, state_conv, g_mix, w_in, rel_table, w_dw, b_dw,
           ln_g, ln_b, w_out, g_ffn, w_gate, w_up, w_down, g_final):
    assert g_mix.shape[0] == 1, "single-layer trunk: the final RMSNorm is fused into the FFN kernel"
    B, S, D = x_prompt.shape
    Bs, T, _ = x_sample.shape
    W = cache_k.shape[2]
    gm, gff, gf = g_mix[0].reshape(1, D), g_ffn[0].reshape(1, D), g_final.reshape(1, D)
    win = w_in[0]
    conv_params = (w_dw[0],) + tuple(a[0].reshape(1, CONV_WIDTH) for a in (b_dw, ln_g, ln_b))
    heads = lambda a: a.reshape(1, a.shape[0], a.shape[1], N_HEADS, HEAD_DIM)
    fvec = _rel_bias_row(rel_table[0])

    qe, qo, k, v, oc, kf, vf, ct, wout, wg, wu, wd = _inproj_prompt(
        x_prompt, gm, win, *conv_params, (w_out[0], w_gate[0], w_up[0], w_down[0]))
    x1 = _mix_prompt(qe, qo, k, v, oc, x_prompt, fvec, wout)
    y_prompt = _ffn(x1.reshape(B * S, D), gff, wg, wu, wd, gf, TM_FFN).reshape(B, S, D)

    qe, qo, kn, vn, oc, nc = _inproj_sample(
        x_sample.reshape(Bs * T, D), state_conv[0], gm, win, *conv_params)
    r3 = lambda a: a.reshape(Bs, T, ATTN_WIDTH)
    x1, nk, nv = _mix_sample(
        r3(qe), r3(qo), r3(kn), r3(vn),
        cache_k[0].reshape(Bs, W, ATTN_WIDTH), cache_v[0].reshape(Bs, W, ATTN_WIDTH),
        oc, x_sample, fvec, wout)
    y_sample = _ffn(x1.reshape(Bs * T, D), gff, wg, wu, wd, gf, Bs * T).reshape(Bs, T, D)

    return (y_prompt, y_sample, heads(kf), heads(vf), ct[None],
            heads(nk), heads(nv), nc[None])
```

```python
import functools

import jax
import jax.numpy as jnp
from jax import lax
from jax.experimental import pallas as pl
from jax.experimental.pallas import tpu as pltpu

D_MODEL = 1024
CHUNK = 64
LEFT_CHUNKS = 8
W_BAND = LEFT_CHUNKS * CHUNK
ATTN_WIDTH = 512
N_HEADS = 8
HEAD_DIM = 64
CONV_WIDTH = 512
CONV_KERNEL = 31
CONV_PAST = CONV_KERNEL - 1
REL_CLIP = 128
D_FF = 2816
RMS_EPS = 1e-6
LN_EPS = 1e-5
NEG_INF = -1e30
SCALE = HEAD_DIM ** -0.5

LANES = 128
SUBLANES = 8
N_PAIRS = ATTN_WIDTH // LANES
TT_PROJ = 64
Z_STRIDE = TT_PROJ + SUBLANES
PROJ_COLS = 256
TQ = 512
N_KBLK = W_BAND // TQ + 1
SUB = 2 * CHUNK
WIN = W_BAND + SUB
F_LEN = 1024
LOOKAHEAD = 1
TM_FFN = 512
MXU_TILE = 256
FFN_COLS = 4 * MXU_TILE
CONV_ROWS = 64
CONV_STEPS = 16
CONV_LEAD = 32
VMEM_LIMIT = 56 * 1024 * 1024

f32 = jnp.float32
bf16 = jnp.bfloat16


def _rmsnorm(x, g):
    return (x * lax.rsqrt(jnp.mean(x * x, axis=-1, keepdims=True) + RMS_EPS)) * g


def _dot(a, b):
    return jnp.dot(a, b, preferred_element_type=f32)


def _dot_nt(a, b):
    return lax.dot_general(a, b, (((1,), (1,)), ((), ())), preferred_element_type=f32)


def _const_spec(shape):
    nd = len(shape)
    return pl.BlockSpec(shape, lambda *_: (0,) * nd, pipeline_mode=pl.Buffered(1))


def _project_stages(h_ref, w_ref, qe_store, qo_store, k_store, v_store):
    def issue(base, cs):
        return _dot(h_ref[...], w_ref[:, base + cs.start:base + cs.stop])

    def q_finish(cs, z):
        q = z * SCALE
        lane = lax.broadcasted_iota(jnp.int32, q.shape, 1)
        even = (lane & HEAD_DIM) == 0
        qe_store(cs, jnp.where(even, q, 0.0).astype(bf16))
        qo_store(cs, jnp.where(even, 0.0, q).astype(bf16))

    stages = []
    for base, finish in ((0, q_finish), (ATTN_WIDTH, k_store), (2 * ATTN_WIDTH, v_store)):
        for c0 in range(0, ATTN_WIDTH, PROJ_COLS):
            cs = slice(c0, c0 + PROJ_COLS)
            stages.append((functools.partial(issue, base, cs), functools.partial(finish, cs)))
    return stages


def _cast_weight(w32_ref, w_ref):
    for c0 in range(0, w_ref.shape[1], MXU_TILE):
        w_ref[:, c0:c0 + MXU_TILE] = w32_ref[:, c0:c0 + MXU_TILE].astype(bf16)


def _glu(h_ref, w_ref):
    base = 3 * ATTN_WIDTH
    a = _dot(h_ref[...], w_ref[:, base:base + CONV_WIDTH])
    g = _dot(h_ref[...], w_ref[:, base + CONV_WIDTH:base + 2 * CONV_WIDTH])
    return a * jax.nn.sigmoid(g)


def _conv_stages(cbuf_ref, sh_ref, y_ref, rows, wdw_ref, bdw_ref):
    lead = CONV_LEAD - CONV_PAST
    span = rows + CONV_LEAD - SUBLANES
    rb = min(CONV_ROWS, rows)

    def shift_stage(m):
        sh_ref[m - 1, 0:span, :] = cbuf_ref[m:m + span, :]

    def tap_stage(cs, r0):
        acc = jnp.zeros((rb, LANES), f32)
        for j in range(CONV_KERNEL):
            a, m = divmod(j + lead, SUBLANES)
            src = cbuf_ref if m == 0 else sh_ref.at[m - 1]
            lo = r0 + SUBLANES * a
            acc = acc + wdw_ref[j:j + 1, cs] * src[lo:lo + rb, cs]
        y_ref[r0:r0 + rb, cs] = acc + bdw_ref[:, cs]

    shifts = [functools.partial(shift_stage, m) for m in range(1, SUBLANES)]
    taps = [functools.partial(tap_stage, slice(c * LANES, (c + 1) * LANES), r0)
            for c in range(CONV_WIDTH // LANES) for r0 in range(0, rows, rb)]
    return shifts, taps


def _conv_time_major(ut_ref, y_ref, steps, wdw_ref, bdw_ref):
    for c in range(CONV_WIDTH // LANES):
        cs = slice(c * LANES, (c + 1) * LANES)
        for t0 in range(0, steps, CONV_STEPS):
            acc = jnp.zeros((CONV_STEPS, ut_ref.shape[2], LANES), f32)
            for j in range(CONV_KERNEL):
                acc = acc + wdw_ref[j:j + 1, cs] * ut_ref[c, t0 + j:t0 + j + CONV_STEPS]
            y_ref[c, t0:t0 + CONV_STEPS] = acc + bdw_ref[:, cs]


def _ln_swish_tiles(y_tiles, lng_ref, lnb_ref):
    n = len(y_tiles) * LANES
    mu = sum(jnp.sum(y, axis=-1, keepdims=True) for y in y_tiles) / n
    yc = [y - mu for y in y_tiles]
    var = sum(jnp.sum(y * y, axis=-1, keepdims=True) for y in yc) / n
    inv = lax.rsqrt(var + LN_EPS)
    out = []
    for c, y in enumerate(yc):
        cs = slice(c * LANES, (c + 1) * LANES)
        yn = y * inv * lng_ref[:, cs] + lnb_ref[:, cs]
        out.append(yn * jax.nn.sigmoid(yn))
    return out


def _ln_swish(y, lng_ref, lnb_ref):
    mu = jnp.mean(y, axis=-1, keepdims=True)
    yc = y - mu
    yn = yc * lax.rsqrt(jnp.mean(yc * yc, axis=-1, keepdims=True) + LN_EPS)
    yn = yn * lng_ref[...] + lnb_ref[...]
    return yn * jax.nn.sigmoid(yn)


def _pipeline(mxu_stages, valu_stages):
    pending = mxu_stages[0][0]()
    done = 0
    for n, (_, consume) in enumerate(mxu_stages):
        cur = pending
        if n + 1 < len(mxu_stages):
            pending = mxu_stages[n + 1][0]()
        upto = (n + 1) * len(valu_stages) // len(mxu_stages)
        for stage in valu_stages[done:upto]:
            stage()
        done = upto
        consume(cur)


def _inproj_prompt_kernel(x_ref, g_ref, w32_ref, wdw_ref, bdw_ref, lng_ref, lnb_ref, *refs,
                          n_cast):
    cast_in, refs = refs[:n_cast], refs[n_cast:]
    qe_ref, qo_ref, k_ref, v_ref, oc_ref, kf_ref, vf_ref, ct_ref = refs[:8]
    cast_out, (w_ref, h_ref, z_ref, ut_ref, y_ref) = refs[8:8 + n_cast], refs[8 + n_cast:]
    for src, dst in zip(cast_in, cast_out):
        dst[...] = src[...].astype(bf16)
    nb = x_ref.shape[0]
    rows = nb * TT_PROJ

    @pl.when(pl.program_id(0) == 0)
    def _():
        _cast_weight(w32_ref, w_ref)
        ut_ref[:, 0:CONV_PAST] = jnp.zeros((CONV_WIDTH // LANES, CONV_PAST, nb, LANES), f32)

    h_ref[...] = _rmsnorm(x_ref[...].reshape(rows, D_MODEL), g_ref[...]).astype(bf16)

    def store(ref):
        def put(cs, val):
            ref[:, :, cs] = val.reshape(nb, TT_PROJ, val.shape[-1])
        return put

    def kv_store(ref, full_ref):
        def put(cs, val):
            val = val.reshape(nb, TT_PROJ, val.shape[-1])
            ref[:, :, cs] = val.astype(bf16)
            full_ref[:, :, cs] = val
        return put

    u = _glu(h_ref, w_ref)
    for issue, consume in _project_stages(h_ref, w_ref, store(qe_ref), store(qo_ref),
                                          kv_store(k_ref, kf_ref), kv_store(v_ref, vf_ref)):
        consume(issue())
    lane_tiles = [slice(c * LANES, (c + 1) * LANES) for c in range(CONV_WIDTH // LANES)]
    for c, cs in enumerate(lane_tiles):
        for b in range(nb):
            z_ref[c, Z_STRIDE * b:Z_STRIDE * b + TT_PROJ, :] = u[b * TT_PROJ:(b + 1) * TT_PROJ, cs]
            ct_ref[b, :, cs] = z_ref[c, Z_STRIDE * b + TT_PROJ - CONV_PAST:Z_STRIDE * b + TT_PROJ, :]
        for t in range(TT_PROJ):
            ut_ref[c, CONV_PAST + t] = z_ref[c, pl.ds(t, nb, stride=Z_STRIDE), :]
    _conv_time_major(ut_ref, y_ref, TT_PROJ, wdw_ref, bdw_ref)
    oc = _ln_swish_tiles([y_ref[c] for c in range(len(lane_tiles))], lng_ref, lnb_ref)
    for c, cs in enumerate(lane_tiles):
        for t in range(TT_PROJ):
            z_ref[c, pl.ds(t, nb, stride=Z_STRIDE), :] = oc[c][t]
        for b in range(nb):
            oc_ref[b, :, cs] = z_ref[c, Z_STRIDE * b:Z_STRIDE * b + TT_PROJ, :].astype(bf16)

    ut_ref[:, 0:CONV_PAST] = ut_ref[:, TT_PROJ:TT_PROJ + CONV_PAST]


def _inproj_prompt(x, g, w, wdw, bdw, lng, lnb, cast_weights):
    B, S, D = x.shape
    assert S % TT_PROJ == 0 and W_BAND % TT_PROJ == 0 and TT_PROJ >= CONV_PAST
    steps = S // TT_PROJ
    first_kept = (S - W_BAND) // TT_PROJ
    tile = lambda n: pl.BlockSpec((B, TT_PROJ, n), lambda i: (0, i, 0))
    kept = pl.BlockSpec((B, TT_PROJ, ATTN_WIDTH), lambda i: (0, jnp.maximum(i - first_kept, 0), 0))
    act = jax.ShapeDtypeStruct((B, S, ATTN_WIDTH), bf16)
    rows = B * TT_PROJ

    def cast_spec(a):
        blk = next(r for r in (32, 64, 128, 256) if a.shape[0] % r == 0 and a.shape[0] // r <= steps)
        last = a.shape[0] // blk - 1
        return pl.BlockSpec((blk, a.shape[1]), lambda i: (jnp.minimum(i, last), 0))

    cast_specs = [cast_spec(a) for a in cast_weights]
    return pl.pallas_call(
        functools.partial(_inproj_prompt_kernel, n_cast=len(cast_weights)),
        grid=(steps,),
        in_specs=[tile(D)] + [_const_spec(a.shape) for a in (g, w, wdw, bdw, lng, lnb)] + cast_specs,
        out_specs=[tile(ATTN_WIDTH)] * 5 + [kept] * 2 + [_const_spec((B, CONV_PAST, CONV_WIDTH))]
                  + cast_specs,
        out_shape=[act] * 5 + [jax.ShapeDtypeStruct((B, W_BAND, ATTN_WIDTH), f32)] * 2
                  + [jax.ShapeDtypeStruct((B, CONV_PAST, CONV_WIDTH), f32)]
                  + [jax.ShapeDtypeStruct(a.shape, bf16) for a in cast_weights],
        scratch_shapes=[pltpu.VMEM(w.shape, bf16),
                        pltpu.VMEM((rows, D), bf16),
                        pltpu.VMEM((CONV_WIDTH // LANES, B * Z_STRIDE, LANES), f32),
                        pltpu.VMEM((CONV_WIDTH // LANES, CONV_PAST + TT_PROJ, B, LANES), f32),
                        pltpu.VMEM((CONV_WIDTH // LANES, TT_PROJ, B, LANES), f32)],
        compiler_params=pltpu.CompilerParams(
            dimension_semantics=("arbitrary",), vmem_limit_bytes=VMEM_LIMIT),
        name="inproj_prompt",
    )(x, g, w, wdw, bdw, lng, lnb, *cast_weights)


def _inproj_sample_kernel(x_ref, st_ref, g_ref, w32_ref, wdw_ref, bdw_ref, lng_ref, lnb_ref,
                          qe_ref, qo_ref, k_ref, v_ref, oc_ref, nc_ref,
                          w_ref, h_ref, cbuf_ref, sh_ref, y_ref):
    B, T, _ = oc_ref.shape
    _cast_weight(w32_ref, w_ref)
    h_ref[...] = _rmsnorm(x_ref[...], g_ref[...]).astype(bf16)
    u = _glu(h_ref, w_ref)

    def store(ref):
        def put(cs, val):
            ref[:, cs] = val
        return put

    proj = _project_stages(h_ref, w_ref, store(qe_ref), store(qo_ref), store(k_ref), store(v_ref))
    lead = CONV_LEAD - CONV_PAST

    def conv_stage(b):
        cbuf_ref[0:SUBLANES, :] = jnp.zeros((SUBLANES, CONV_WIDTH), f32)
        cbuf_ref[lead:CONV_LEAD, :] = st_ref[b]
        cbuf_ref[CONV_LEAD:CONV_LEAD + T, :] = u[b * T:(b + 1) * T]
        nc_ref[b] = cbuf_ref[CONV_LEAD + T - CONV_PAST:CONV_LEAD + T, :]
        shifts, taps = _conv_stages(cbuf_ref, sh_ref, y_ref, T, wdw_ref, bdw_ref)
        for stage in shifts + taps:
            stage()
        oc_ref[b] = _ln_swish(y_ref[...], lng_ref, lnb_ref).astype(bf16)

    _pipeline(proj, [functools.partial(conv_stage, b) for b in range(B)])


def _inproj_sample(x, st, g, w, wdw, bdw, lng, lnb):
    N, D = x.shape
    B = st.shape[0]
    T = N // B
    assert T >= CONV_PAST and T % SUBLANES == 0
    act = lambda dt: jax.ShapeDtypeStruct((N, ATTN_WIDTH), dt)
    return pl.pallas_call(
        _inproj_sample_kernel,
        grid=(1,),
        in_specs=[_const_spec(a.shape) for a in (x, st, g, w, wdw, bdw, lng, lnb)],
        out_specs=[_const_spec((N, ATTN_WIDTH))] * 4
                  + [_const_spec((B, T, CONV_WIDTH)), _const_spec((B, CONV_PAST, CONV_WIDTH))],
        out_shape=[act(bf16)] * 2 + [act(f32)] * 2
                  + [jax.ShapeDtypeStruct((B, T, CONV_WIDTH), bf16),
                     jax.ShapeDtypeStruct((B, CONV_PAST, CONV_WIDTH), f32)],
        scratch_shapes=[pltpu.VMEM(w.shape, bf16),
                        pltpu.VMEM((N, D), bf16),
                        pltpu.VMEM((CONV_LEAD + T, CONV_WIDTH), f32),
                        pltpu.VMEM((SUBLANES - 1, CONV_LEAD + T - SUBLANES, CONV_WIDTH), f32),
                        pltpu.VMEM((T, CONV_WIDTH), f32)],
        compiler_params=pltpu.CompilerParams(vmem_limit_bytes=VMEM_LIMIT),
        name="inproj_sample",
    )(x, st, g, w, wdw, bdw, lng, lnb)


def _toeplitz(frow, rows):
    return pltpu.roll(jnp.broadcast_to(frow, (rows, F_LEN)), 0, 1, stride=1, stride_axis=0)


def _softmax_parts(s_parts):
    m = functools.reduce(jnp.maximum, [jnp.max(s, axis=-1, keepdims=True) for s in s_parts])
    e_parts = [jnp.exp(s - m) for s in s_parts]
    l = functools.reduce(jnp.add, [jnp.sum(e, axis=-1, keepdims=True) for e in e_parts])
    return e_parts, l


def _residual_out(x_ref, cat_ref, oc_ref, wout_ref, gn_ref, o_ref, hn_ref):
    x1 = (x_ref[0] + _dot(cat_ref[...], wout_ref[0:ATTN_WIDTH, :])
          + _dot(oc_ref[0], wout_ref[ATTN_WIDTH:, :]))
    o_ref[0] = x1
    hn_ref[0] = _rmsnorm(x1, gn_ref[...]).astype(bf16)


def _mix_prompt_kernel(qe_ref, qo_ref, *refs):
    k_refs, v_refs = refs[:N_KBLK], refs[N_KBLK:2 * N_KBLK]
    oc_ref, x_ref, f_ref, wout_ref, gn_ref, o_ref, hn_ref, bias_ref, cat_ref = refs[2 * N_KBLK:]
    i = pl.program_id(1)

    @pl.when((pl.program_id(0) == 0) & (i == 0))
    def _():
        for h in range(N_HEADS):
            p, par = divmod(h, 2)
            bias_ref[p, par * SUB:(par + 1) * SUB, :] = _toeplitz(f_ref[h:h + 1, :], SUB)[:, :WIN]

    row = lax.broadcasted_iota(jnp.int32, (2 * SUB, LANES), 0)
    lane = lax.broadcasted_iota(jnp.int32, (2 * SUB, LANES), 1)
    odd_chunk = (row & CHUNK) != 0
    edge_ok = {0: jnp.logical_not(odd_chunk & (lane < CHUNK)),
               WIN // LANES - 1: odd_chunk | (lane < CHUNK)}
    blk_ok = [i >= N_KBLK - 1 - b for b in range(N_KBLK - 1)]
    even_head = lax.broadcasted_iota(jnp.int32, (SUB, LANES), 1) < HEAD_DIM

    def pieces(r):
        out = []
        for b in range(N_KBLK):
            lo, hi = max(SUB * r, TQ * b), min(SUB * r + WIN, TQ * (b + 1))
            if lo < hi:
                out.append((b, lo - TQ * b, hi - TQ * b))
        return out

    def scores(r, p):
        rs, ps = slice(r * SUB, (r + 1) * SUB), slice(p * LANES, (p + 1) * LANES)
        q2 = jnp.concatenate([qe_ref[0, rs, ps], qo_ref[0, rs, ps]], axis=0)
        return jnp.concatenate(
            [_dot_nt(q2, k_refs[b][0, a:z, ps]) for b, a, z in pieces(r)], axis=1)

    def finish(r, p, s):
        rs, ps = slice(r * SUB, (r + 1) * SUB), slice(p * LANES, (p + 1) * LANES)
        s = s + bias_ref[p]
        tiles = []
        for t in range(WIN // LANES):
            st = s[:, t * LANES:(t + 1) * LANES]
            b = (SUB * r + LANES * t) // TQ
            ok = edge_ok.get(t)
            if b < N_KBLK - 1:
                ok = blk_ok[b] if ok is None else ok & blk_ok[b]
            tiles.append(st if ok is None else jnp.where(ok, st, NEG_INF))
        (e,), l = _softmax_parts([jnp.concatenate(tiles, axis=1)])
        pb = e.astype(bf16)
        o, c0 = None, 0
        for b, a, z in pieces(r):
            ob = _dot(pb[:, c0:c0 + z - a], v_refs[b][0, a:z, ps])
            o = ob if o is None else o + ob
            c0 += z - a
        o = o / l
        cat_ref[rs, ps] = jnp.where(even_head, o[:SUB], o[SUB:]).astype(bf16)

    units = [(r, p) for r in range(TQ // SUB) for p in range(N_PAIRS)]
    pending = [scores(*u) for u in units[:LOOKAHEAD]]
    for n, unit in enumerate(units):
        if n + LOOKAHEAD < len(units):
            pending.append(scores(*units[n + LOOKAHEAD]))
        finish(*unit, pending.pop(0))

    _residual_out(x_ref, cat_ref, oc_ref, wout_ref, gn_ref, o_ref, hn_ref)


def _mix_prompt(qe, qo, k, v, oc, x, fvec, wout, gn):
    B, S, D = x.shape
    assert S % TQ == 0 and W_BAND % TQ == 0 and TQ % SUB == 0 and WIN % LANES == 0
    qspec = pl.BlockSpec((1, TQ, ATTN_WIDTH), lambda b, i: (b, i, 0))
    kspec = lambda back: pl.BlockSpec(
        (1, TQ, ATTN_WIDTH), lambda b, i: (b, jnp.maximum(i - back, 0), 0))
    kspecs = [kspec(N_KBLK - 1 - n) for n in range(N_KBLK)]
    xspec = pl.BlockSpec((1, TQ, D), lambda b, i: (b, i, 0))
    return pl.pallas_call(
        _mix_prompt_kernel,
        grid=(B, S // TQ),
        in_specs=[qspec, qspec] + kspecs + kspecs + [qspec, xspec,
                  _const_spec(fvec.shape), _const_spec(wout.shape), _const_spec(gn.shape)],
        out_specs=[xspec, xspec],
        out_shape=[jax.ShapeDtypeStruct((B, S, D), f32), jax.ShapeDtypeStruct((B, S, D), bf16)],
        scratch_shapes=[pltpu.VMEM((N_PAIRS, 2 * SUB, WIN), f32),
                        pltpu.VMEM((TQ, ATTN_WIDTH), bf16)],
        compiler_params=pltpu.CompilerParams(
            dimension_semantics=("arbitrary", "arbitrary"), vmem_limit_bytes=VMEM_LIMIT),
        name="mix_prompt",
    )(qe, qo, *([k] * N_KBLK), *([v] * N_KBLK), oc, x, fvec, wout, gn)


def _mix_sample_kernel(qe_ref, qo_ref, kn_ref, vn_ref, ck_ref, cv_ref, oc_ref, x_ref,
                       f_ref, wout_ref, gn_ref, o_ref, hn_ref, nk_ref, nv_ref, bias_ref, cat_ref):
    T = oc_ref.shape[1]
    W = ck_ref.shape[1]

    @pl.when(pl.program_id(0) == 0)
    def _():
        for h in range(N_HEADS):
            p, par = divmod(h, 2)
            bias_ref[p, par * T:(par + 1) * T, :] = _toeplitz(f_ref[h:h + 1, :], T)

    nk_ref[0, 0:W - T, :] = ck_ref[0, T:W, :]
    nk_ref[0, W - T:W, :] = kn_ref[0]
    nv_ref[0, 0:W - T, :] = cv_ref[0, T:W, :]
    nv_ref[0, W - T:W, :] = vn_ref[0]

    lane = lax.broadcasted_iota(jnp.int32, (T, LANES), 1)
    even_head = lane < HEAD_DIM
    for p in range(N_PAIRS):
        ps = slice(p * LANES, (p + 1) * LANES)
        kc = ck_ref[0, :, ps].astype(bf16)
        vc = cv_ref[0, :, ps].astype(bf16)
        kn = kn_ref[0, :, ps].astype(bf16)
        vn = vn_ref[0, :, ps].astype(bf16)
        q2 = jnp.concatenate([qe_ref[0, :, ps], qo_ref[0, :, ps]], axis=0)
        (ec, en), l = _softmax_parts([_dot_nt(q2, kc) + bias_ref[p, :, 0:W],
                                      _dot_nt(q2, kn) + bias_ref[p, :, W:W + T]])
        o = (_dot(ec.astype(bf16), vc) + _dot(en.astype(bf16), vn)) / l
        cat_ref[:, ps] = jnp.where(even_head, o[:T], o[T:]).astype(bf16)

    _residual_out(x_ref, cat_ref, oc_ref, wout_ref, gn_ref, o_ref, hn_ref)


def _mix_sample(qe, qo, kn, vn, ck, cv, oc, x, fvec, wout, gn):
    B, T, D = x.shape
    W = ck.shape[1]
    assert W == W_BAND and W + T <= F_LEN - T
    row = lambda r, n: pl.BlockSpec((1, r, n), lambda b: (b, 0, 0))
    return pl.pallas_call(
        _mix_sample_kernel,
        grid=(B,),
        in_specs=[row(T, ATTN_WIDTH)] * 4 + [row(W, ATTN_WIDTH)] * 2
                 + [row(T, CONV_WIDTH), row(T, D)]
                 + [_const_spec(a.shape) for a in (fvec, wout, gn)],
        out_specs=[row(T, D), row(T, D), row(W, ATTN_WIDTH), row(W, ATTN_WIDTH)],
        out_shape=[jax.ShapeDtypeStruct((B, T, D), f32),
                   jax.ShapeDtypeStruct((B, T, D), bf16),
                   jax.ShapeDtypeStruct((B, W, ATTN_WIDTH), f32),
                   jax.ShapeDtypeStruct((B, W, ATTN_WIDTH), f32)],
        scratch_shapes=[pltpu.VMEM((N_PAIRS, 2 * T, F_LEN), f32),
                        pltpu.VMEM((T, ATTN_WIDTH), bf16)],
        compiler_params=pltpu.CompilerParams(
            dimension_semantics=("arbitrary",), vmem_limit_bytes=VMEM_LIMIT),
        name="mix_sample",
    )(qe, qo, kn, vn, ck, cv, oc, x, fvec, wout, gn)


def _ffn_block(x, h, wg_ref, wu_ref, wd_ref, gf_ref):
    acc = x
    for c0 in range(0, D_FF, FFN_COLS):
        cs = slice(c0, min(c0 + FFN_COLS, D_FF))
        a = jax.nn.silu(_dot(h, wg_ref[:, cs])) * _dot(h, wu_ref[:, cs])
        acc = acc + _dot(a.astype(bf16), wd_ref[cs, :])
    return _rmsnorm(acc, gf_ref[...])


def _ffn_kernel(x_ref, h_ref, xs_ref, hs_ref, wg_ref, wu_ref, wd_ref, gf_ref, o_ref, os_ref):
    last = pl.num_programs(0) - 1

    @pl.when(pl.program_id(0) < last)
    def _():
        o_ref[...] = _ffn_block(x_ref[...], h_ref[...], wg_ref, wu_ref, wd_ref, gf_ref)

    @pl.when(pl.program_id(0) == last)
    def _():
        os_ref[...] = _ffn_block(xs_ref[...], hs_ref[...], wg_ref, wu_ref, wd_ref, gf_ref)


def _ffn(x, h, xs, hs, wg, wu, wd, gf):
    N, D = x.shape
    assert N % TM_FFN == 0
    n = N // TM_FFN
    tile = pl.BlockSpec((TM_FFN, D), lambda i: (jnp.minimum(i, n - 1), 0))
    return pl.pallas_call(
        _ffn_kernel,
        grid=(n + 1,),
        in_specs=[tile, tile] + [_const_spec(a.shape) for a in (xs, hs, wg, wu, wd, gf)],
        out_specs=[tile, _const_spec(xs.shape)],
        out_shape=[jax.ShapeDtypeStruct((N, D), f32), jax.ShapeDtypeStruct(xs.shape, f32)],
        compiler_params=pltpu.CompilerParams(
            dimension_semantics=("arbitrary",), vmem_limit_bytes=VMEM_LIMIT),
        name="ffn",
    )(x, h, xs, hs, wg, wu, wd, gf)


def _rel_bias_row(table):
    n_rel = table.shape[1]
    far = W_BAND - REL_CLIP
    assert n_rel == 2 * REL_CLIP + 1 and far + n_rel + SUB <= F_LEN
    rep = lambda col, n: jnp.broadcast_to(table[:, col:col + 1], (table.shape[0], n))
    return jnp.concatenate([rep(n_rel - 1, far), table[:, ::-1],
                            rep(0, F_LEN - SUB - far - n_rel), rep(n_rel - 1, SUB)], axis=1)


def kernel(x_prompt, x_sample, cache_k, cache_v, state_conv, g_mix, w_in, rel_table, w_dw, b_dw,
           ln_g, ln_b, w_out, g_ffn, w_gate, w_up, w_down, g_final):
    assert g_mix.shape[0] == 1, "single-layer trunk: the final RMSNorm is fused into the FFN kernel"
    B, S, D = x_prompt.shape
    Bs, T, _ = x_sample.shape
    W = cache_k.shape[2]
    gm, gff, gf = g_mix[0].reshape(1, D), g_ffn[0].reshape(1, D), g_final.reshape(1, D)
    win = w_in[0]
    conv_params = (w_dw[0],) + tuple(a[0].reshape(1, CONV_WIDTH) for a in (b_dw, ln_g, ln_b))
    heads = lambda a: a.reshape(1, a.shape[0], a.shape[1], N_HEADS, HEAD_DIM)
    fvec = _rel_bias_row(rel_table[0])

    qe, qo, k, v, oc, kf, vf, ct, wout, wg, wu, wd = _inproj_prompt(
        x_prompt, gm, win, *conv_params, (w_out[0], w_gate[0], w_up[0], w_down[0]))
    x1p, hp = _mix_prompt(qe, qo, k, v, oc, x_prompt, fvec, wout, gff)

    qe, qo, kn, vn, oc, nc = _inproj_sample(
        x_sample.reshape(Bs * T, D), state_conv[0], gm, win, *conv_params)
    r3 = lambda a: a.reshape(Bs, T, ATTN_WIDTH)
    x1s, hs, nk, nv = _mix_sample(
        r3(qe), r3(qo), r3(kn), r3(vn),
        cache_k[0].reshape(Bs, W, ATTN_WIDTH), cache_v[0].reshape(Bs, W, ATTN_WIDTH),
        oc, x_sample, fvec, wout, gff)

    flat = lambda a: a.reshape(-1, D)
    y_prompt, y_sample = _ffn(flat(x1p), flat(hp), flat(x1s), flat(hs), wg, wu, wd, gf)
    return (y_prompt.reshape(B, S, D), y_sample.reshape(Bs, T, D), heads(kf), heads(vf), ct[None],
            heads(nk), heads(nv), nc[None])
```

```python
import functools

import jax
import jax.numpy as jnp
from jax import lax
from jax.experimental import pallas as pl
from jax.experimental.pallas import tpu as pltpu

D_MODEL = 1024
CHUNK = 64
LEFT_CHUNKS = 8
W_BAND = LEFT_CHUNKS * CHUNK
ATTN_WIDTH = 512
N_HEADS = 8
HEAD_DIM = 64
CONV_WIDTH = 512
CONV_KERNEL = 31
CONV_PAST = CONV_KERNEL - 1
REL_CLIP = 128
D_FF = 2816
RMS_EPS = 1e-6
LN_EPS = 1e-5
NEG_INF = -1e30
SCALE = HEAD_DIM ** -0.5

LANES = 128
SUBLANES = 8
N_PAIRS = ATTN_WIDTH // LANES
TT_PROJ = 64
Z_STRIDE = TT_PROJ + SUBLANES
PROJ_COLS = 256
TQ = 512
N_KBLK = W_BAND // TQ + 1
SUB = 2 * CHUNK
WIN = W_BAND + SUB
F_LEN = 1024
LOOKAHEAD = 1
TM_FFN = 512
MXU_TILE = 256
FFN_COLS = 4 * MXU_TILE
CONV_ROWS = 64
CONV_STEPS = 16
CONV_LEAD = 32
VMEM_LIMIT = 56 * 1024 * 1024

f32 = jnp.float32
bf16 = jnp.bfloat16


def _rmsnorm(x, g):
    return (x * lax.rsqrt(jnp.mean(x * x, axis=-1, keepdims=True) + RMS_EPS)) * g


def _dot(a, b):
    return jnp.dot(a, b, preferred_element_type=f32)


def _dot_nt(a, b):
    return lax.dot_general(a, b, (((1,), (1,)), ((), ())), preferred_element_type=f32)


def _const_spec(shape):
    nd = len(shape)
    return pl.BlockSpec(shape, lambda *_: (0,) * nd, pipeline_mode=pl.Buffered(1))


def _project_stages(h_ref, w_ref, qe_store, qo_store, k_store, v_store):
    def issue(base, cs):
        return _dot(h_ref[...], w_ref[:, base + cs.start:base + cs.stop])

    def q_finish(cs, z):
        q = z * SCALE
        lane = lax.broadcasted_iota(jnp.int32, q.shape, 1)
        even = (lane & HEAD_DIM) == 0
        qe_store(cs, jnp.where(even, q, 0.0).astype(bf16))
        qo_store(cs, jnp.where(even, 0.0, q).astype(bf16))

    stages = []
    for base, finish in ((0, q_finish), (ATTN_WIDTH, k_store), (2 * ATTN_WIDTH, v_store)):
        for c0 in range(0, ATTN_WIDTH, PROJ_COLS):
            cs = slice(c0, c0 + PROJ_COLS)
            stages.append((functools.partial(issue, base, cs), functools.partial(finish, cs)))
    return stages


def _cast_weight(w32_ref, w_ref):
    for c0 in range(0, w_ref.shape[1], MXU_TILE):
        w_ref[:, c0:c0 + MXU_TILE] = w32_ref[:, c0:c0 + MXU_TILE].astype(bf16)


def _glu(h_ref, w_ref):
    base = 3 * ATTN_WIDTH
    a = _dot(h_ref[...], w_ref[:, base:base + CONV_WIDTH])
    g = _dot(h_ref[...], w_ref[:, base + CONV_WIDTH:base + 2 * CONV_WIDTH])
    return a * jax.nn.sigmoid(g)


def _conv_stages(cbuf_ref, sh_ref, y_ref, rows, wdw_ref, bdw_ref):
    lead = CONV_LEAD - CONV_PAST
    span = rows + CONV_LEAD - SUBLANES
    rb = min(CONV_ROWS, rows)

    def shift_stage(m):
        sh_ref[m - 1, 0:span, :] = cbuf_ref[m:m + span, :]

    def tap_stage(cs, r0):
        acc = jnp.zeros((rb, LANES), f32)
        for j in range(CONV_KERNEL):
            a, m = divmod(j + lead, SUBLANES)
            src = cbuf_ref if m == 0 else sh_ref.at[m - 1]
            lo = r0 + SUBLANES * a
            acc = acc + wdw_ref[j:j + 1, cs] * src[lo:lo + rb, cs]
        y_ref[r0:r0 + rb, cs] = acc + bdw_ref[:, cs]

    shifts = [functools.partial(shift_stage, m) for m in range(1, SUBLANES)]
    taps = [functools.partial(tap_stage, slice(c * LANES, (c + 1) * LANES), r0)
            for c in range(CONV_WIDTH // LANES) for r0 in range(0, rows, rb)]
    return shifts, taps


def _conv_time_major(ut_ref, y_ref, steps, wdw_ref, bdw_ref):
    for c in range(CONV_WIDTH // LANES):
        cs = slice(c * LANES, (c + 1) * LANES)
        for t0 in range(0, steps, CONV_STEPS):
            acc = jnp.zeros((CONV_STEPS, ut_ref.shape[2], LANES), f32)
            for j in range(CONV_KERNEL):
                acc = acc + wdw_ref[j:j + 1, cs] * ut_ref[c, t0 + j:t0 + j + CONV_STEPS]
            y_ref[c, t0:t0 + CONV_STEPS] = acc + bdw_ref[:, cs]


def _ln_swish_tiles(y_tiles, lng_ref, lnb_ref):
    n = len(y_tiles) * LANES
    mu = sum(jnp.sum(y, axis=-1, keepdims=True) for y in y_tiles) / n
    yc = [y - mu for y in y_tiles]
    var = sum(jnp.sum(y * y, axis=-1, keepdims=True) for y in yc) / n
    inv = lax.rsqrt(var + LN_EPS)
    out = []
    for c, y in enumerate(yc):
        cs = slice(c * LANES, (c + 1) * LANES)
        yn = y * inv * lng_ref[:, cs] + lnb_ref[:, cs]
        out.append(yn * jax.nn.sigmoid(yn))
    return out


def _ln_swish(y, lng_ref, lnb_ref):
    mu = jnp.mean(y, axis=-1, keepdims=True)
    yc = y - mu
    yn = yc * lax.rsqrt(jnp.mean(yc * yc, axis=-1, keepdims=True) + LN_EPS)
    yn = yn * lng_ref[...] + lnb_ref[...]
    return yn * jax.nn.sigmoid(yn)


def _pipeline(mxu_stages, valu_stages):
    pending = mxu_stages[0][0]()
    done = 0
    for n, (_, consume) in enumerate(mxu_stages):
        cur = pending
        if n + 1 < len(mxu_stages):
            pending = mxu_stages[n + 1][0]()
        upto = (n + 1) * len(valu_stages) // len(mxu_stages)
        for stage in valu_stages[done:upto]:
            stage()
        done = upto
        consume(cur)


def _inproj_kernel(x_ref, xs_ref, st_ref, g_ref, w32_ref, wdw_ref, bdw_ref, lng_ref, lnb_ref,
                   *refs, n_cast):
    cast_in, refs = refs[:n_cast], refs[n_cast:]
    prompt_out, sample_out = refs[:8], refs[8:14]
    cast_out, scratch = refs[14:14 + n_cast], refs[14 + n_cast:]
    w_ref, prompt_scratch, sample_scratch = scratch[0], scratch[1:5], scratch[5:]
    for src, dst in zip(cast_in, cast_out):
        dst[...] = src[...].astype(bf16)
    last = pl.num_programs(0) - 1

    @pl.when(pl.program_id(0) == 0)
    def _():
        _cast_weight(w32_ref, w_ref)
        ut_ref = prompt_scratch[2]
        ut_ref[:, 0:CONV_PAST] = jnp.zeros((ut_ref.shape[0], CONV_PAST) + ut_ref.shape[2:], f32)

    conv_refs = (wdw_ref, bdw_ref, lng_ref, lnb_ref)
    pl.when(pl.program_id(0) < last)(functools.partial(
        _inproj_prompt_tile, x_ref, g_ref, w_ref, conv_refs, prompt_out, prompt_scratch))
    pl.when(pl.program_id(0) == last)(functools.partial(
        _inproj_sample_tokens, xs_ref, st_ref, g_ref, w_ref, conv_refs, sample_out, sample_scratch))


def _inproj_prompt_tile(x_ref, g_ref, w_ref, conv_refs, out_refs, scratch):
    wdw_ref, bdw_ref, lng_ref, lnb_ref = conv_refs
    qe_ref, qo_ref, k_ref, v_ref, oc_ref, kf_ref, vf_ref, ct_ref = out_refs
    h_ref, z_ref, ut_ref, y_ref = scratch
    nb = x_ref.shape[0]
    rows = nb * TT_PROJ
    h_ref[...] = _rmsnorm(x_ref[...].reshape(rows, D_MODEL), g_ref[...]).astype(bf16)

    def store(ref):
        def put(cs, val):
            ref[:, :, cs] = val.reshape(nb, TT_PROJ, val.shape[-1])
        return put

    def kv_store(ref, full_ref):
        def put(cs, val):
            val = val.reshape(nb, TT_PROJ, val.shape[-1])
            ref[:, :, cs] = val.astype(bf16)
            full_ref[:, :, cs] = val
        return put

    u = _glu(h_ref, w_ref)
    for issue, consume in _project_stages(h_ref, w_ref, store(qe_ref), store(qo_ref),
                                          kv_store(k_ref, kf_ref), kv_store(v_ref, vf_ref)):
        consume(issue())
    lane_tiles = [slice(c * LANES, (c + 1) * LANES) for c in range(CONV_WIDTH // LANES)]
    for c, cs in enumerate(lane_tiles):
        for b in range(nb):
            z_ref[c, Z_STRIDE * b:Z_STRIDE * b + TT_PROJ, :] = u[b * TT_PROJ:(b + 1) * TT_PROJ, cs]
            ct_ref[b, :, cs] = z_ref[c, Z_STRIDE * b + TT_PROJ - CONV_PAST:Z_STRIDE * b + TT_PROJ, :]
        for t in range(TT_PROJ):
            ut_ref[c, CONV_PAST + t] = z_ref[c, pl.ds(t, nb, stride=Z_STRIDE), :]
    _conv_time_major(ut_ref, y_ref, TT_PROJ, wdw_ref, bdw_ref)
    oc = _ln_swish_tiles([y_ref[c] for c in range(len(lane_tiles))], lng_ref, lnb_ref)
    for c, cs in enumerate(lane_tiles):
        for t in range(TT_PROJ):
            z_ref[c, pl.ds(t, nb, stride=Z_STRIDE), :] = oc[c][t]
        for b in range(nb):
            oc_ref[b, :, cs] = z_ref[c, Z_STRIDE * b:Z_STRIDE * b + TT_PROJ, :].astype(bf16)

    ut_ref[:, 0:CONV_PAST] = ut_ref[:, TT_PROJ:TT_PROJ + CONV_PAST]


def _inproj(x, xs, st, g, w, wdw, bdw, lng, lnb, cast_weights):
    B, S, D = x.shape
    Ns, Bs = xs.shape[0], st.shape[0]
    T = Ns // Bs
    assert S % TT_PROJ == 0 and W_BAND % TT_PROJ == 0 and TT_PROJ >= CONV_PAST
    assert T >= CONV_PAST and T % SUBLANES == 0
    steps = S // TT_PROJ
    first_kept = (S - W_BAND) // TT_PROJ
    tile = lambda n: pl.BlockSpec((B, TT_PROJ, n), lambda i: (0, jnp.minimum(i, steps - 1), 0))
    kept = pl.BlockSpec((B, TT_PROJ, ATTN_WIDTH),
                        lambda i: (0, jnp.clip(i - first_kept, 0, steps - 1 - first_kept), 0))
    act = jax.ShapeDtypeStruct((B, S, ATTN_WIDTH), bf16)
    rows = B * TT_PROJ
    sample_shapes = ([jax.ShapeDtypeStruct((Ns, ATTN_WIDTH), bf16)] * 2
                     + [jax.ShapeDtypeStruct((Ns, ATTN_WIDTH), f32)] * 2
                     + [jax.ShapeDtypeStruct((Bs, T, CONV_WIDTH), bf16),
                        jax.ShapeDtypeStruct((Bs, CONV_PAST, CONV_WIDTH), f32)])

    def cast_spec(a):
        blk = next(r for r in (32, 64, 128, 256) if a.shape[0] % r == 0 and a.shape[0] // r <= steps)
        last = a.shape[0] // blk - 1
        return pl.BlockSpec((blk, a.shape[1]), lambda i: (jnp.minimum(i, last), 0))

    cast_specs = [cast_spec(a) for a in cast_weights]
    return pl.pallas_call(
        functools.partial(_inproj_kernel, n_cast=len(cast_weights)),
        grid=(steps + 1,),
        in_specs=[tile(D)] + [_const_spec(a.shape) for a in (xs, st, g, w, wdw, bdw, lng, lnb)]
                 + cast_specs,
        out_specs=[tile(ATTN_WIDTH)] * 5 + [kept] * 2 + [_const_spec((B, CONV_PAST, CONV_WIDTH))]
                  + [_const_spec(a.shape) for a in sample_shapes] + cast_specs,
        out_shape=[act] * 5 + [jax.ShapeDtypeStruct((B, W_BAND, ATTN_WIDTH), f32)] * 2
                  + [jax.ShapeDtypeStruct((B, CONV_PAST, CONV_WIDTH), f32)] + sample_shapes
                  + [jax.ShapeDtypeStruct(a.shape, bf16) for a in cast_weights],
        scratch_shapes=[pltpu.VMEM(w.shape, bf16),
                        pltpu.VMEM((rows, D), bf16),
                        pltpu.VMEM((CONV_WIDTH // LANES, B * Z_STRIDE, LANES), f32),
                        pltpu.VMEM((CONV_WIDTH // LANES, CONV_PAST + TT_PROJ, B, LANES), f32),
                        pltpu.VMEM((CONV_WIDTH // LANES, TT_PROJ, B, LANES), f32),
                        pltpu.VMEM((Ns, D), bf16),
                        pltpu.VMEM((CONV_LEAD + T, CONV_WIDTH), f32),
                        pltpu.VMEM((SUBLANES - 1, CONV_LEAD + T - SUBLANES, CONV_WIDTH), f32),
                        pltpu.VMEM((T, CONV_WIDTH), f32)],
        compiler_params=pltpu.CompilerParams(
            dimension_semantics=("arbitrary",), vmem_limit_bytes=VMEM_LIMIT),
        name="inproj",
    )(x, xs, st, g, w, wdw, bdw, lng, lnb, *cast_weights)


def _inproj_sample_tokens(x_ref, st_ref, g_ref, w_ref, conv_refs, out_refs, scratch):
    wdw_ref, bdw_ref, lng_ref, lnb_ref = conv_refs
    qe_ref, qo_ref, k_ref, v_ref, oc_ref, nc_ref = out_refs
    h_ref, cbuf_ref, sh_ref, y_ref = scratch
    B, T, _ = oc_ref.shape
    h_ref[...] = _rmsnorm(x_ref[...], g_ref[...]).astype(bf16)
    u = _glu(h_ref, w_ref)

    def store(ref):
        def put(cs, val):
            ref[:, cs] = val
        return put

    proj = _project_stages(h_ref, w_ref, store(qe_ref), store(qo_ref), store(k_ref), store(v_ref))
    lead = CONV_LEAD - CONV_PAST

    def conv_stage(b):
        cbuf_ref[0:SUBLANES, :] = jnp.zeros((SUBLANES, CONV_WIDTH), f32)
        cbuf_ref[lead:CONV_LEAD, :] = st_ref[b]
        cbuf_ref[CONV_LEAD:CONV_LEAD + T, :] = u[b * T:(b + 1) * T]
        nc_ref[b] = cbuf_ref[CONV_LEAD + T - CONV_PAST:CONV_LEAD + T, :]
        shifts, taps = _conv_stages(cbuf_ref, sh_ref, y_ref, T, wdw_ref, bdw_ref)
        for stage in shifts + taps:
            stage()
        oc_ref[b] = _ln_swish(y_ref[...], lng_ref, lnb_ref).astype(bf16)

    _pipeline(proj, [functools.partial(conv_stage, b) for b in range(B)])


def _toeplitz(frow, rows):
    return pltpu.roll(jnp.broadcast_to(frow, (rows, F_LEN)), 0, 1, stride=1, stride_axis=0)


def _softmax_parts(s_parts):
    m = functools.reduce(jnp.maximum, [jnp.max(s, axis=-1, keepdims=True) for s in s_parts])
    e_parts = [jnp.exp(s - m) for s in s_parts]
    l = functools.reduce(jnp.add, [jnp.sum(e, axis=-1, keepdims=True) for e in e_parts])
    return e_parts, l


def _residual_out(x_ref, cat_ref, oc_ref, wout_ref, gn_ref, o_ref, hn_ref):
    x1 = (x_ref[0] + _dot(cat_ref[...], wout_ref[0:ATTN_WIDTH, :])
          + _dot(oc_ref[0], wout_ref[ATTN_WIDTH:, :]))
    o_ref[0] = x1
    hn_ref[0] = _rmsnorm(x1, gn_ref[...]).astype(bf16)


def _mix_prompt_kernel(qe_ref, qo_ref, *refs):
    k_refs, v_refs = refs[:N_KBLK], refs[N_KBLK:2 * N_KBLK]
    oc_ref, x_ref, f_ref, wout_ref, gn_ref, o_ref, hn_ref, bias_ref, cat_ref = refs[2 * N_KBLK:]
    i = pl.program_id(1)

    @pl.when((pl.program_id(0) == 0) & (i == 0))
    def _():
        for h in range(N_HEADS):
            p, par = divmod(h, 2)
            bias_ref[p, par * SUB:(par + 1) * SUB, :] = _toeplitz(f_ref[h:h + 1, :], SUB)[:, :WIN]

    row = lax.broadcasted_iota(jnp.int32, (2 * SUB, LANES), 0)
    lane = lax.broadcasted_iota(jnp.int32, (2 * SUB, LANES), 1)
    odd_chunk = (row & CHUNK) != 0
    edge_ok = {0: jnp.logical_not(odd_chunk & (lane < CHUNK)),
               WIN // LANES - 1: odd_chunk | (lane < CHUNK)}
    blk_ok = [i >= N_KBLK - 1 - b for b in range(N_KBLK - 1)]
    even_head = lax.broadcasted_iota(jnp.int32, (SUB, LANES), 1) < HEAD_DIM

    def pieces(r):
        out = []
        for b in range(N_KBLK):
            lo, hi = max(SUB * r, TQ * b), min(SUB * r + WIN, TQ * (b + 1))
            if lo < hi:
                out.append((b, lo - TQ * b, hi - TQ * b))
        return out

    def scores(r, p):
        rs, ps = slice(r * SUB, (r + 1) * SUB), slice(p * LANES, (p + 1) * LANES)
        q2 = jnp.concatenate([qe_ref[0, rs, ps], qo_ref[0, rs, ps]], axis=0)
        return jnp.concatenate(
            [_dot_nt(q2, k_refs[b][0, a:z, ps]) for b, a, z in pieces(r)], axis=1)

    def finish(r, p, s):
        rs, ps = slice(r * SUB, (r + 1) * SUB), slice(p * LANES, (p + 1) * LANES)
        s = s + bias_ref[p]
        tiles = []
        for t in range(WIN // LANES):
            st = s[:, t * LANES:(t + 1) * LANES]
            b = (SUB * r + LANES * t) // TQ
            ok = edge_ok.get(t)
            if b < N_KBLK - 1:
                ok = blk_ok[b] if ok is None else ok & blk_ok[b]
            tiles.append(st if ok is None else jnp.where(ok, st, NEG_INF))
        (e,), l = _softmax_parts([jnp.concatenate(tiles, axis=1)])
        pb = e.astype(bf16)
        o, c0 = None, 0
        for b, a, z in pieces(r):
            ob = _dot(pb[:, c0:c0 + z - a], v_refs[b][0, a:z, ps])
            o = ob if o is None else o + ob
            c0 += z - a
        o = o / l
        cat_ref[rs, ps] = jnp.where(even_head, o[:SUB], o[SUB:]).astype(bf16)

    units = [(r, p) for r in range(TQ // SUB) for p in range(N_PAIRS)]
    pending = [scores(*u) for u in units[:LOOKAHEAD]]
    for n, unit in enumerate(units):
        if n + LOOKAHEAD < len(units):
            pending.append(scores(*units[n + LOOKAHEAD]))
        finish(*unit, pending.pop(0))

    _residual_out(x_ref, cat_ref, oc_ref, wout_ref, gn_ref, o_ref, hn_ref)


def _mix_prompt(qe, qo, k, v, oc, x, fvec, wout, gn):
    B, S, D = x.shape
    assert S % TQ == 0 and W_BAND % TQ == 0 and TQ % SUB == 0 and WIN % LANES == 0
    qspec = pl.BlockSpec((1, TQ, ATTN_WIDTH), lambda b, i: (b, i, 0))
    kspec = lambda back: pl.BlockSpec(
        (1, TQ, ATTN_WIDTH), lambda b, i: (b, jnp.maximum(i - back, 0), 0))
    kspecs = [kspec(N_KBLK - 1 - n) for n in range(N_KBLK)]
    xspec = pl.BlockSpec((1, TQ, D), lambda b, i: (b, i, 0))
    return pl.pallas_call(
        _mix_prompt_kernel,
        grid=(B, S // TQ),
        in_specs=[qspec, qspec] + kspecs + kspecs + [qspec, xspec,
                  _const_spec(fvec.shape), _const_spec(wout.shape), _const_spec(gn.shape)],
        out_specs=[xspec, xspec],
        out_shape=[jax.ShapeDtypeStruct((B, S, D), f32), jax.ShapeDtypeStruct((B, S, D), bf16)],
        scratch_shapes=[pltpu.VMEM((N_PAIRS, 2 * SUB, WIN), f32),
                        pltpu.VMEM((TQ, ATTN_WIDTH), bf16)],
        compiler_params=pltpu.CompilerParams(
            dimension_semantics=("arbitrary", "arbitrary"), vmem_limit_bytes=VMEM_LIMIT),
        name="mix_prompt",
    )(qe, qo, *([k] * N_KBLK), *([v] * N_KBLK), oc, x, fvec, wout, gn)


def _mix_sample_kernel(qe_ref, qo_ref, kn_ref, vn_ref, ck_ref, cv_ref, oc_ref, x_ref,
                       f_ref, wout_ref, gn_ref, o_ref, hn_ref, nk_ref, nv_ref, bias_ref, cat_ref):
    T = oc_ref.shape[1]
    W = ck_ref.shape[1]

    @pl.when(pl.program_id(0) == 0)
    def _():
        for h in range(N_HEADS):
            p, par = divmod(h, 2)
            bias_ref[p, par * T:(par + 1) * T, :] = _toeplitz(f_ref[h:h + 1, :], T)

    nk_ref[0, 0:W - T, :] = ck_ref[0, T:W, :]
    nk_ref[0, W - T:W, :] = kn_ref[0]
    nv_ref[0, 0:W - T, :] = cv_ref[0, T:W, :]
    nv_ref[0, W - T:W, :] = vn_ref[0]

    lane = lax.broadcasted_iota(jnp.int32, (T, LANES), 1)
    even_head = lane < HEAD_DIM
    for p in range(N_PAIRS):
        ps = slice(p * LANES, (p + 1) * LANES)
        kc = ck_ref[0, :, ps].astype(bf16)
        vc = cv_ref[0, :, ps].astype(bf16)
        kn = kn_ref[0, :, ps].astype(bf16)
        vn = vn_ref[0, :, ps].astype(bf16)
        q2 = jnp.concatenate([qe_ref[0, :, ps], qo_ref[0, :, ps]], axis=0)
        (ec, en), l = _softmax_parts([_dot_nt(q2, kc) + bias_ref[p, :, 0:W],
                                      _dot_nt(q2, kn) + bias_ref[p, :, W:W + T]])
        o = (_dot(ec.astype(bf16), vc) + _dot(en.astype(bf16), vn)) / l
        cat_ref[:, ps] = jnp.where(even_head, o[:T], o[T:]).astype(bf16)

    _residual_out(x_ref, cat_ref, oc_ref, wout_ref, gn_ref, o_ref, hn_ref)


def _mix_sample(qe, qo, kn, vn, ck, cv, oc, x, fvec, wout, gn):
    B, T, D = x.shape
    W = ck.shape[1]
    assert W == W_BAND and W + T <= F_LEN - T
    row = lambda r, n: pl.BlockSpec((1, r, n), lambda b: (b, 0, 0))
    return pl.pallas_call(
        _mix_sample_kernel,
        grid=(B,),
        in_specs=[row(T, ATTN_WIDTH)] * 4 + [row(W, ATTN_WIDTH)] * 2
                 + [row(T, CONV_WIDTH), row(T, D)]
                 + [_const_spec(a.shape) for a in (fvec, wout, gn)],
        out_specs=[row(T, D), row(T, D), row(W, ATTN_WIDTH), row(W, ATTN_WIDTH)],
        out_shape=[jax.ShapeDtypeStruct((B, T, D), f32),
                   jax.ShapeDtypeStruct((B, T, D), bf16),
                   jax.ShapeDtypeStruct((B, W, ATTN_WIDTH), f32),
                   jax.ShapeDtypeStruct((B, W, ATTN_WIDTH), f32)],
        scratch_shapes=[pltpu.VMEM((N_PAIRS, 2 * T, F_LEN), f32),
                        pltpu.VMEM((T, ATTN_WIDTH), bf16)],
        compiler_params=pltpu.CompilerParams(
            dimension_semantics=("arbitrary",), vmem_limit_bytes=VMEM_LIMIT),
        name="mix_sample",
    )(qe, qo, kn, vn, ck, cv, oc, x, fvec, wout, gn)


def _ffn_block(x, h, wg_ref, wu_ref, wd_ref, gf_ref):
    acc = x
    for c0 in range(0, D_FF, FFN_COLS):
        cs = slice(c0, min(c0 + FFN_COLS, D_FF))
        a = jax.nn.silu(_dot(h, wg_ref[:, cs])) * _dot(h, wu_ref[:, cs])
        acc = acc + _dot(a.astype(bf16), wd_ref[cs, :])
    return _rmsnorm(acc, gf_ref[...])


def _ffn_kernel(x_ref, h_ref, xs_ref, hs_ref, wg_ref, wu_ref, wd_ref, gf_ref, o_ref, os_ref):
    last = pl.num_programs(0) - 1

    @pl.when(pl.program_id(0) < last)
    def _():
        o_ref[...] = _ffn_block(x_ref[...], h_ref[...], wg_ref, wu_ref, wd_ref, gf_ref)

    @pl.when(pl.program_id(0) == last)
    def _():
        os_ref[...] = _ffn_block(xs_ref[...], hs_ref[...], wg_ref, wu_ref, wd_ref, gf_ref)


def _ffn(x, h, xs, hs, wg, wu, wd, gf):
    N, D = x.shape
    assert N % TM_FFN == 0
    n = N // TM_FFN
    tile = pl.BlockSpec((TM_FFN, D), lambda i: (jnp.minimum(i, n - 1), 0))
    return pl.pallas_call(
        _ffn_kernel,
        grid=(n + 1,),
        in_specs=[tile, tile] + [_const_spec(a.shape) for a in (xs, hs, wg, wu, wd, gf)],
        out_specs=[tile, _const_spec(xs.shape)],
        out_shape=[jax.ShapeDtypeStruct((N, D), f32), jax.ShapeDtypeStruct(xs.shape, f32)],
        compiler_params=pltpu.CompilerParams(
            dimension_semantics=("arbitrary",), vmem_limit_bytes=VMEM_LIMIT),
        name="ffn",
    )(x, h, xs, hs, wg, wu, wd, gf)


def _rel_bias_row(table):
    n_rel = table.shape[1]
    far = W_BAND - REL_CLIP
    assert n_rel == 2 * REL_CLIP + 1 and far + n_rel + SUB <= F_LEN
    rep = lambda col, n: jnp.broadcast_to(table[:, col:col + 1], (table.shape[0], n))
    return jnp.concatenate([rep(n_rel - 1, far), table[:, ::-1],
                            rep(0, F_LEN - SUB - far - n_rel), rep(n_rel - 1, SUB)], axis=1)


def kernel(x_prompt, x_sample, cache_k, cache_v, state_conv, g_mix, w_in, rel_table, w_dw, b_dw,
           ln_g, ln_b, w_out, g_ffn, w_gate, w_up, w_down, g_final):
    assert g_mix.shape[0] == 1, "single-layer trunk: the final RMSNorm is fused into the FFN kernel"
    B, S, D = x_prompt.shape
    Bs, T, _ = x_sample.shape
    W = cache_k.shape[2]
    gm, gff, gf = g_mix[0].reshape(1, D), g_ffn[0].reshape(1, D), g_final.reshape(1, D)
    win = w_in[0]
    conv_params = (w_dw[0],) + tuple(a[0].reshape(1, CONV_WIDTH) for a in (b_dw, ln_g, ln_b))
    heads = lambda a: a.reshape(1, a.shape[0], a.shape[1], N_HEADS, HEAD_DIM)
    fvec = _rel_bias_row(rel_table[0])

    (qe, qo, k, v, oc, kf, vf, ct, qes, qos, kn, vn, ocs, nc, wout, wg, wu, wd) = _inproj(
        x_prompt, x_sample.reshape(Bs * T, D), state_conv[0], gm, win, *conv_params,
        (w_out[0], w_gate[0], w_up[0], w_down[0]))
    x1p, hp = _mix_prompt(qe, qo, k, v, oc, x_prompt, fvec, wout, gff)

    r3 = lambda a: a.reshape(Bs, T, ATTN_WIDTH)
    x1s, hs, nk, nv = _mix_sample(
        r3(qes), r3(qos), r3(kn), r3(vn),
        cache_k[0].reshape(Bs, W, ATTN_WIDTH), cache_v[0].reshape(Bs, W, ATTN_WIDTH),
        ocs, x_sample, fvec, wout, gff)

    flat = lambda a: a.reshape(-1, D)
    y_prompt, y_sample = _ffn(flat(x1p), flat(hp), flat(x1s), flat(hs), wg, wu, wd, gf)
    return (y_prompt.reshape(B, S, D), y_sample.reshape(Bs, T, D), heads(kf), heads(vf), ct[None],
            heads(nk), heads(nv), nc[None])
```

```python
import functools

import jax
import jax.numpy as jnp
from jax import lax
from jax.experimental import pallas as pl
from jax.experimental.pallas import tpu as pltpu

D_MODEL = 1024
CHUNK = 64
LEFT_CHUNKS = 8
W_BAND = LEFT_CHUNKS * CHUNK
ATTN_WIDTH = 512
N_HEADS = 8
HEAD_DIM = 64
CONV_WIDTH = 512
CONV_KERNEL = 31
CONV_PAST = CONV_KERNEL - 1
REL_CLIP = 128
D_FF = 2816
RMS_EPS = 1e-6
LN_EPS = 1e-5
NEG_INF = -1e30
SCALE = HEAD_DIM ** -0.5

LANES = 128
SUBLANES = 8
N_PAIRS = ATTN_WIDTH // LANES
TT_PROJ = 64
Z_STRIDE = TT_PROJ + SUBLANES
PROJ_COLS = 256
TQ = 512
N_KBLK = W_BAND // TQ + 1
SUB = 2 * CHUNK
WIN = W_BAND + SUB
F_LEN = 1024
LOOKAHEAD = 1
TM_FFN = 512
MXU_TILE = 256
FFN_COLS = 4 * MXU_TILE
CONV_ROWS = 64
CONV_STEPS = 16
CONV_LEAD = 32
VMEM_LIMIT = 56 * 1024 * 1024

f32 = jnp.float32
bf16 = jnp.bfloat16


def _rmsnorm(x, g):
    return (x * lax.rsqrt(jnp.mean(x * x, axis=-1, keepdims=True) + RMS_EPS)) * g


def _dot(a, b):
    return jnp.dot(a, b, preferred_element_type=f32)


def _dot_nt(a, b):
    return lax.dot_general(a, b, (((1,), (1,)), ((), ())), preferred_element_type=f32)


def _const_spec(shape):
    nd = len(shape)
    return pl.BlockSpec(shape, lambda *_: (0,) * nd, pipeline_mode=pl.Buffered(1))


def _project_stages(h_ref, w_ref, qe_store, qo_store, k_store, v_store):
    def issue(base, cs):
        return _dot(h_ref[...], w_ref[:, base + cs.start:base + cs.stop])

    def q_finish(cs, z):
        q = z * SCALE
        lane = lax.broadcasted_iota(jnp.int32, q.shape, 1)
        even = (lane & HEAD_DIM) == 0
        qe_store(cs, jnp.where(even, q, 0.0).astype(bf16))
        qo_store(cs, jnp.where(even, 0.0, q).astype(bf16))

    stages = []
    for base, finish in ((0, q_finish), (ATTN_WIDTH, k_store), (2 * ATTN_WIDTH, v_store)):
        for c0 in range(0, ATTN_WIDTH, PROJ_COLS):
            cs = slice(c0, c0 + PROJ_COLS)
            stages.append((functools.partial(issue, base, cs), functools.partial(finish, cs)))
    return stages


def _cast_weight(w32_ref, w_ref):
    for c0 in range(0, w_ref.shape[1], MXU_TILE):
        w_ref[:, c0:c0 + MXU_TILE] = w32_ref[:, c0:c0 + MXU_TILE].astype(bf16)


def _glu(h_ref, w_ref):
    base = 3 * ATTN_WIDTH
    a = _dot(h_ref[...], w_ref[:, base:base + CONV_WIDTH])
    g = _dot(h_ref[...], w_ref[:, base + CONV_WIDTH:base + 2 * CONV_WIDTH])
    return a * jax.nn.sigmoid(g)


def _conv_stages(cbuf_ref, sh_ref, y_ref, rows, wdw_ref, bdw_ref):
    lead = CONV_LEAD - CONV_PAST
    span = rows + CONV_LEAD - SUBLANES
    rb = min(CONV_ROWS, rows)

    def shift_stage(m):
        sh_ref[m - 1, 0:span, :] = cbuf_ref[m:m + span, :]

    def tap_stage(cs, r0):
        acc = jnp.zeros((rb, LANES), f32)
        for j in range(CONV_KERNEL):
            a, m = divmod(j + lead, SUBLANES)
            src = cbuf_ref if m == 0 else sh_ref.at[m - 1]
            lo = r0 + SUBLANES * a
            acc = acc + wdw_ref[j:j + 1, cs] * src[lo:lo + rb, cs]
        y_ref[r0:r0 + rb, cs] = acc + bdw_ref[:, cs]

    shifts = [functools.partial(shift_stage, m) for m in range(1, SUBLANES)]
    taps = [functools.partial(tap_stage, slice(c * LANES, (c + 1) * LANES), r0)
            for c in range(CONV_WIDTH // LANES) for r0 in range(0, rows, rb)]
    return shifts, taps


def _conv_time_major(ut_ref, y_ref, steps, wdw_ref, bdw_ref):
    for c in range(CONV_WIDTH // LANES):
        cs = slice(c * LANES, (c + 1) * LANES)
        for t0 in range(0, steps, CONV_STEPS):
            acc = jnp.zeros((CONV_STEPS, ut_ref.shape[2], LANES), f32)
            for j in range(CONV_KERNEL):
                acc = acc + wdw_ref[j:j + 1, cs] * ut_ref[c, t0 + j:t0 + j + CONV_STEPS]
            y_ref[c, t0:t0 + CONV_STEPS] = acc + bdw_ref[:, cs]


def _ln_swish_tiles(y_tiles, lng_ref, lnb_ref):
    n = len(y_tiles) * LANES
    mu = sum(jnp.sum(y, axis=-1, keepdims=True) for y in y_tiles) / n
    yc = [y - mu for y in y_tiles]
    var = sum(jnp.sum(y * y, axis=-1, keepdims=True) for y in yc) / n
    inv = lax.rsqrt(var + LN_EPS)
    out = []
    for c, y in enumerate(yc):
        cs = slice(c * LANES, (c + 1) * LANES)
        yn = y * inv * lng_ref[:, cs] + lnb_ref[:, cs]
        out.append(yn * jax.nn.sigmoid(yn))
    return out


def _ln_swish(y, lng_ref, lnb_ref):
    mu = jnp.mean(y, axis=-1, keepdims=True)
    yc = y - mu
    yn = yc * lax.rsqrt(jnp.mean(yc * yc, axis=-1, keepdims=True) + LN_EPS)
    yn = yn * lng_ref[...] + lnb_ref[...]
    return yn * jax.nn.sigmoid(yn)


def _pipeline(mxu_stages, valu_stages):
    pending = mxu_stages[0][0]()
    done = 0
    for n, (_, consume) in enumerate(mxu_stages):
        cur = pending
        if n + 1 < len(mxu_stages):
            pending = mxu_stages[n + 1][0]()
        upto = (n + 1) * len(valu_stages) // len(mxu_stages)
        for stage in valu_stages[done:upto]:
            stage()
        done = upto
        consume(cur)


def _inproj_kernel(x_ref, xs_ref, st_ref, g_ref, w32_ref, wdw_ref, bdw_ref, lng_ref, lnb_ref,
                   *refs, n_cast):
    cast_in, refs = refs[:n_cast], refs[n_cast:]
    prompt_out, sample_out = refs[:8], refs[8:14]
    cast_out, scratch = refs[14:14 + n_cast], refs[14 + n_cast:]
    w_ref, prompt_scratch, sample_scratch = scratch[0], scratch[1:5], scratch[5:]
    for src, dst in zip(cast_in, cast_out):
        dst[...] = src[...].astype(bf16)
    last = pl.num_programs(0) - 1

    @pl.when(pl.program_id(0) == 0)
    def _():
        _cast_weight(w32_ref, w_ref)
        ut_ref = prompt_scratch[2]
        ut_ref[:, 0:CONV_PAST] = jnp.zeros((ut_ref.shape[0], CONV_PAST) + ut_ref.shape[2:], f32)

    conv_refs = (wdw_ref, bdw_ref, lng_ref, lnb_ref)
    pl.when(pl.program_id(0) < last)(functools.partial(
        _inproj_prompt_tile, x_ref, g_ref, w_ref, conv_refs, prompt_out, prompt_scratch))
    pl.when(pl.program_id(0) == last)(functools.partial(
        _inproj_sample_tokens, xs_ref, st_ref, g_ref, w_ref, conv_refs, sample_out, sample_scratch))


def _inproj_prompt_tile(x_ref, g_ref, w_ref, conv_refs, out_refs, scratch):
    wdw_ref, bdw_ref, lng_ref, lnb_ref = conv_refs
    qe_ref, qo_ref, k_ref, v_ref, oc_ref, kf_ref, vf_ref, ct_ref = out_refs
    h_ref, z_ref, ut_ref, y_ref = scratch
    nb = x_ref.shape[0]
    rows = nb * TT_PROJ
    h_ref[...] = _rmsnorm(x_ref[...].reshape(rows, D_MODEL), g_ref[...]).astype(bf16)

    def store(ref):
        def put(cs, val):
            ref[:, :, cs] = val.reshape(nb, TT_PROJ, val.shape[-1])
        return put

    def kv_store(ref, full_ref):
        def put(cs, val):
            val = val.reshape(nb, TT_PROJ, val.shape[-1])
            ref[:, :, cs] = val.astype(bf16)
            full_ref[:, :, cs] = val
        return put

    u = _glu(h_ref, w_ref)
    for issue, consume in _project_stages(h_ref, w_ref, store(qe_ref), store(qo_ref),
                                          kv_store(k_ref, kf_ref), kv_store(v_ref, vf_ref)):
        consume(issue())
    lane_tiles = [slice(c * LANES, (c + 1) * LANES) for c in range(CONV_WIDTH // LANES)]
    for c, cs in enumerate(lane_tiles):
        for b in range(nb):
            z_ref[c, Z_STRIDE * b:Z_STRIDE * b + TT_PROJ, :] = u[b * TT_PROJ:(b + 1) * TT_PROJ, cs]
            ct_ref[b, :, cs] = z_ref[c, Z_STRIDE * b + TT_PROJ - CONV_PAST:Z_STRIDE * b + TT_PROJ, :]
        for t in range(TT_PROJ):
            ut_ref[c, CONV_PAST + t] = z_ref[c, pl.ds(t, nb, stride=Z_STRIDE), :]
    _conv_time_major(ut_ref, y_ref, TT_PROJ, wdw_ref, bdw_ref)
    oc = _ln_swish_tiles([y_ref[c] for c in range(len(lane_tiles))], lng_ref, lnb_ref)
    for c, cs in enumerate(lane_tiles):
        for t in range(TT_PROJ):
            z_ref[c, pl.ds(t, nb, stride=Z_STRIDE), :] = oc[c][t]
        for b in range(nb):
            oc_ref[b, :, cs] = z_ref[c, Z_STRIDE * b:Z_STRIDE * b + TT_PROJ, :].astype(bf16)

    ut_ref[:, 0:CONV_PAST] = ut_ref[:, TT_PROJ:TT_PROJ + CONV_PAST]


def _inproj(x, xs, st, g, w, wdw, bdw, lng, lnb, cast_weights):
    B, S, D = x.shape
    Ns, Bs = xs.shape[0], st.shape[0]
    T = Ns // Bs
    assert S % TT_PROJ == 0 and W_BAND % TT_PROJ == 0 and TT_PROJ >= CONV_PAST
    assert T >= CONV_PAST and T % SUBLANES == 0
    steps = S // TT_PROJ
    first_kept = (S - W_BAND) // TT_PROJ
    tile = lambda n: pl.BlockSpec((B, TT_PROJ, n), lambda i: (0, jnp.minimum(i, steps - 1), 0))
    kept = pl.BlockSpec((B, TT_PROJ, ATTN_WIDTH),
                        lambda i: (0, jnp.clip(i - first_kept, 0, steps - 1 - first_kept), 0))
    act = jax.ShapeDtypeStruct((B, S, ATTN_WIDTH), bf16)
    rows = B * TT_PROJ
    sample_shapes = ([jax.ShapeDtypeStruct((Ns, ATTN_WIDTH), bf16)] * 2
                     + [jax.ShapeDtypeStruct((Ns, ATTN_WIDTH), f32)] * 2
                     + [jax.ShapeDtypeStruct((Bs, T, CONV_WIDTH), bf16),
                        jax.ShapeDtypeStruct((Bs, CONV_PAST, CONV_WIDTH), f32)])

    def cast_spec(a):
        blk = next(r for r in (32, 64, 128, 256) if a.shape[0] % r == 0 and a.shape[0] // r <= steps)
        last = a.shape[0] // blk - 1
        return pl.BlockSpec((blk, a.shape[1]), lambda i: (jnp.minimum(i, last), 0))

    cast_specs = [cast_spec(a) for a in cast_weights]
    return pl.pallas_call(
        functools.partial(_inproj_kernel, n_cast=len(cast_weights)),
        grid=(steps + 1,),
        in_specs=[tile(D)] + [_const_spec(a.shape) for a in (xs, st, g, w, wdw, bdw, lng, lnb)]
                 + cast_specs,
        out_specs=[tile(ATTN_WIDTH)] * 5 + [kept] * 2 + [_const_spec((B, CONV_PAST, CONV_WIDTH))]
                  + [_const_spec(a.shape) for a in sample_shapes] + cast_specs,
        out_shape=[act] * 5 + [jax.ShapeDtypeStruct((B, W_BAND, ATTN_WIDTH), f32)] * 2
                  + [jax.ShapeDtypeStruct((B, CONV_PAST, CONV_WIDTH), f32)] + sample_shapes
                  + [jax.ShapeDtypeStruct(a.shape, bf16) for a in cast_weights],
        scratch_shapes=[pltpu.VMEM(w.shape, bf16),
                        pltpu.VMEM((rows, D), bf16),
                        pltpu.VMEM((CONV_WIDTH // LANES, B * Z_STRIDE, LANES), f32),
                        pltpu.VMEM((CONV_WIDTH // LANES, CONV_PAST + TT_PROJ, B, LANES), f32),
                        pltpu.VMEM((CONV_WIDTH // LANES, TT_PROJ, B, LANES), f32),
                        pltpu.VMEM((Ns, D), bf16),
                        pltpu.VMEM((CONV_LEAD + T, CONV_WIDTH), f32),
                        pltpu.VMEM((SUBLANES - 1, CONV_LEAD + T - SUBLANES, CONV_WIDTH), f32),
                        pltpu.VMEM((T, CONV_WIDTH), f32)],
        compiler_params=pltpu.CompilerParams(
            dimension_semantics=("arbitrary",), vmem_limit_bytes=VMEM_LIMIT),
        name="inproj",
    )(x, xs, st, g, w, wdw, bdw, lng, lnb, *cast_weights)


def _inproj_sample_tokens(x_ref, st_ref, g_ref, w_ref, conv_refs, out_refs, scratch):
    wdw_ref, bdw_ref, lng_ref, lnb_ref = conv_refs
    qe_ref, qo_ref, k_ref, v_ref, oc_ref, nc_ref = out_refs
    h_ref, cbuf_ref, sh_ref, y_ref = scratch
    B, T, _ = oc_ref.shape
    h_ref[...] = _rmsnorm(x_ref[...], g_ref[...]).astype(bf16)
    u = _glu(h_ref, w_ref)

    def store(ref):
        def put(cs, val):
            ref[:, cs] = val
        return put

    proj = _project_stages(h_ref, w_ref, store(qe_ref), store(qo_ref), store(k_ref), store(v_ref))
    lead = CONV_LEAD - CONV_PAST

    def conv_stage(b):
        cbuf_ref[0:SUBLANES, :] = jnp.zeros((SUBLANES, CONV_WIDTH), f32)
        cbuf_ref[lead:CONV_LEAD, :] = st_ref[b]
        cbuf_ref[CONV_LEAD:CONV_LEAD + T, :] = u[b * T:(b + 1) * T]
        nc_ref[b] = cbuf_ref[CONV_LEAD + T - CONV_PAST:CONV_LEAD + T, :]
        shifts, taps = _conv_stages(cbuf_ref, sh_ref, y_ref, T, wdw_ref, bdw_ref)
        for stage in shifts + taps:
            stage()
        oc_ref[b] = _ln_swish(y_ref[...], lng_ref, lnb_ref).astype(bf16)

    _pipeline(proj, [functools.partial(conv_stage, b) for b in range(B)])


def _toeplitz(frow, rows):
    return pltpu.roll(jnp.broadcast_to(frow, (rows, F_LEN)), 0, 1, stride=1, stride_axis=0)


def _softmax_parts(s_parts):
    m = functools.reduce(jnp.maximum, [jnp.max(s, axis=-1, keepdims=True) for s in s_parts])
    e_parts = [jnp.exp(s - m) for s in s_parts]
    l = functools.reduce(jnp.add, [jnp.sum(e, axis=-1, keepdims=True) for e in e_parts])
    return e_parts, l


def _residual_out(x_ref, cat_ref, oc_ref, wout_ref, gn_ref, o_ref, hn_ref):
    x1 = (x_ref[0] + _dot(cat_ref[...], wout_ref[0:ATTN_WIDTH, :])
          + _dot(oc_ref[0], wout_ref[ATTN_WIDTH:, :]))
    o_ref[0] = x1
    hn_ref[0] = _rmsnorm(x1, gn_ref[...]).astype(bf16)


def _mix_prompt_kernel(qe_ref, qo_ref, *refs):
    k_refs, v_refs = refs[:N_KBLK], refs[N_KBLK:2 * N_KBLK]
    oc_ref, x_ref, f_ref, wout_ref, gn_ref, o_ref, hn_ref, bias_ref, cat_ref = refs[2 * N_KBLK:]
    i = pl.program_id(1)

    @pl.when((pl.program_id(0) == 0) & (i == 0))
    def _():
        for h in range(N_HEADS):
            p, par = divmod(h, 2)
            bias_ref[p, par * SUB:(par + 1) * SUB, :] = _toeplitz(f_ref[h:h + 1, :], SUB)[:, :WIN]

    row = lax.broadcasted_iota(jnp.int32, (2 * SUB, LANES), 0)
    lane = lax.broadcasted_iota(jnp.int32, (2 * SUB, LANES), 1)
    odd_chunk = (row & CHUNK) != 0
    edge_ok = {0: jnp.logical_not(odd_chunk & (lane < CHUNK)),
               WIN // LANES - 1: odd_chunk | (lane < CHUNK)}
    blk_ok = [i >= N_KBLK - 1 - b for b in range(N_KBLK - 1)]
    even_head = lax.broadcasted_iota(jnp.int32, (SUB, LANES), 1) < HEAD_DIM

    def pieces(r):
        out = []
        for b in range(N_KBLK):
            lo, hi = max(SUB * r, TQ * b), min(SUB * r + WIN, TQ * (b + 1))
            if lo < hi:
                out.append((b, lo - TQ * b, hi - TQ * b))
        return out

    def scores(r, p):
        rs, ps = slice(r * SUB, (r + 1) * SUB), slice(p * LANES, (p + 1) * LANES)
        q2 = jnp.concatenate([qe_ref[0, rs, ps], qo_ref[0, rs, ps]], axis=0)
        return jnp.concatenate(
            [_dot_nt(q2, k_refs[b][0, a:z, ps]) for b, a, z in pieces(r)], axis=1)

    def finish(r, p, s):
        rs, ps = slice(r * SUB, (r + 1) * SUB), slice(p * LANES, (p + 1) * LANES)
        s = s + bias_ref[p]
        tiles = []
        for t in range(WIN // LANES):
            st = s[:, t * LANES:(t + 1) * LANES]
            b = (SUB * r + LANES * t) // TQ
            ok = edge_ok.get(t)
            if b < N_KBLK - 1:
                ok = blk_ok[b] if ok is None else ok & blk_ok[b]
            tiles.append(st if ok is None else jnp.where(ok, st, NEG_INF))
        (e,), l = _softmax_parts([jnp.concatenate(tiles, axis=1)])
        pb = e.astype(bf16)
        o, c0 = None, 0
        for b, a, z in pieces(r):
            ob = _dot(pb[:, c0:c0 + z - a], v_refs[b][0, a:z, ps])
            o = ob if o is None else o + ob
            c0 += z - a
        o = o / l
        cat_ref[rs, ps] = jnp.where(even_head, o[:SUB], o[SUB:]).astype(bf16)

    units = [(r, p) for r in range(TQ // SUB) for p in range(N_PAIRS)]
    pending = [scores(*u) for u in units[:LOOKAHEAD]]
    for n, unit in enumerate(units):
        if n + LOOKAHEAD < len(units):
            pending.append(scores(*units[n + LOOKAHEAD]))
        finish(*unit, pending.pop(0))

    _residual_out(x_ref, cat_ref, oc_ref, wout_ref, gn_ref, o_ref, hn_ref)


def _mix_prompt(qe, qo, k, v, oc, x, fvec, wout, gn):
    B, S, D = x.shape
    assert S % TQ == 0 and W_BAND % TQ == 0 and TQ % SUB == 0 and WIN % LANES == 0
    qspec = pl.BlockSpec((1, TQ, ATTN_WIDTH), lambda b, i: (b, i, 0))
    kspec = lambda back: pl.BlockSpec(
        (1, TQ, ATTN_WIDTH), lambda b, i: (b, jnp.maximum(i - back, 0), 0))
    kspecs = [kspec(N_KBLK - 1 - n) for n in range(N_KBLK)]
    xspec = pl.BlockSpec((1, TQ, D), lambda b, i: (b, i, 0))
    return pl.pallas_call(
        _mix_prompt_kernel,
        grid=(B, S // TQ),
        in_specs=[qspec, qspec] + kspecs + kspecs + [qspec, xspec,
                  _const_spec(fvec.shape), _const_spec(wout.shape), _const_spec(gn.shape)],
        out_specs=[xspec, xspec],
        out_shape=[jax.ShapeDtypeStruct((B, S, D), f32), jax.ShapeDtypeStruct((B, S, D), bf16)],
        scratch_shapes=[pltpu.VMEM((N_PAIRS, 2 * SUB, WIN), f32),
                        pltpu.VMEM((TQ, ATTN_WIDTH), bf16)],
        compiler_params=pltpu.CompilerParams(
            dimension_semantics=("arbitrary", "arbitrary"), vmem_limit_bytes=VMEM_LIMIT),
        name="mix_prompt",
    )(qe, qo, *([k] * N_KBLK), *([v] * N_KBLK), oc, x, fvec, wout, gn)


def _mix_sample_kernel(qe_ref, qo_ref, kn_ref, vn_ref, ck_hbm, cv_hbm, oc_ref, x_ref,
                       f_ref, wout_ref, gn_ref, o_ref, hn_ref, nk_ref, nv_ref,
                       bias_ref, cat_ref, ckv_ref, sem):
    T = oc_ref.shape[1]
    W = ck_hbm.shape[1]

    b = pl.program_id(0)
    copies = [pltpu.make_async_copy(src.at[b, :, h, :], ckv_ref.at[a, h], sem.at[a, h])
              for a, src in enumerate((ck_hbm, cv_hbm)) for h in range(N_HEADS)]
    for c in copies:
        c.start()
    for c in copies:
        c.wait()
    ck_ref, cv_ref = (jnp.concatenate([ckv_ref[a, h] for h in range(N_HEADS)], axis=1)
                      for a in range(2))

    @pl.when(pl.program_id(0) == 0)
    def _():
        for h in range(N_HEADS):
            p, par = divmod(h, 2)
            bias_ref[p, par * T:(par + 1) * T, :] = _toeplitz(f_ref[h:h + 1, :], T)

    nk_ref[0, 0:W - T, :] = ck_ref[T:W, :]
    nk_ref[0, W - T:W, :] = kn_ref[0]
    nv_ref[0, 0:W - T, :] = cv_ref[T:W, :]
    nv_ref[0, W - T:W, :] = vn_ref[0]

    lane = lax.broadcasted_iota(jnp.int32, (T, LANES), 1)
    even_head = lane < HEAD_DIM
    for p in range(N_PAIRS):
        ps = slice(p * LANES, (p + 1) * LANES)
        kc = ck_ref[:, ps].astype(bf16)
        vc = cv_ref[:, ps].astype(bf16)
        kn = kn_ref[0, :, ps].astype(bf16)
        vn = vn_ref[0, :, ps].astype(bf16)
        q2 = jnp.concatenate([qe_ref[0, :, ps], qo_ref[0, :, ps]], axis=0)
        (ec, en), l = _softmax_parts([_dot_nt(q2, kc) + bias_ref[p, :, 0:W],
                                      _dot_nt(q2, kn) + bias_ref[p, :, W:W + T]])
        o = (_dot(ec.astype(bf16), vc) + _dot(en.astype(bf16), vn)) / l
        cat_ref[:, ps] = jnp.where(even_head, o[:T], o[T:]).astype(bf16)

    _residual_out(x_ref, cat_ref, oc_ref, wout_ref, gn_ref, o_ref, hn_ref)


def _mix_sample(qe, qo, kn, vn, ck, cv, oc, x, fvec, wout, gn):
    B, T, D = x.shape
    W = ck.shape[1]
    assert W == W_BAND and W + T <= F_LEN - T and ck.shape[2:] == (N_HEADS, HEAD_DIM)
    row = lambda r, n: pl.BlockSpec((1, r, n), lambda b: (b, 0, 0))
    hbm = pl.BlockSpec(memory_space=pl.ANY)
    return pl.pallas_call(
        _mix_sample_kernel,
        grid=(B,),
        in_specs=[row(T, ATTN_WIDTH)] * 4 + [hbm, hbm]
                 + [row(T, CONV_WIDTH), row(T, D)]
                 + [_const_spec(a.shape) for a in (fvec, wout, gn)],
        out_specs=[row(T, D), row(T, D), row(W, ATTN_WIDTH), row(W, ATTN_WIDTH)],
        out_shape=[jax.ShapeDtypeStruct((B, T, D), f32),
                   jax.ShapeDtypeStruct((B, T, D), bf16),
                   jax.ShapeDtypeStruct((B, W, ATTN_WIDTH), f32),
                   jax.ShapeDtypeStruct((B, W, ATTN_WIDTH), f32)],
        scratch_shapes=[pltpu.VMEM((N_PAIRS, 2 * T, F_LEN), f32),
                        pltpu.VMEM((T, ATTN_WIDTH), bf16),
                        pltpu.VMEM((2, N_HEADS, W, HEAD_DIM), f32),
                        pltpu.SemaphoreType.DMA((2, N_HEADS))],
        compiler_params=pltpu.CompilerParams(
            dimension_semantics=("arbitrary",), vmem_limit_bytes=VMEM_LIMIT),
        name="mix_sample",
    )(qe, qo, kn, vn, ck, cv, oc, x, fvec, wout, gn)


def _ffn_block(x, h, wg_ref, wu_ref, wd_ref, gf_ref):
    acc = x
    for c0 in range(0, D_FF, FFN_COLS):
        cs = slice(c0, min(c0 + FFN_COLS, D_FF))
        a = jax.nn.silu(_dot(h, wg_ref[:, cs])) * _dot(h, wu_ref[:, cs])
        acc = acc + _dot(a.astype(bf16), wd_ref[cs, :])
    return _rmsnorm(acc, gf_ref[...])


def _ffn_kernel(x_ref, h_ref, xs_ref, hs_ref, wg_ref, wu_ref, wd_ref, gf_ref, o_ref, os_ref):
    last = pl.num_programs(0) - 1

    @pl.when(pl.program_id(0) < last)
    def _():
        o_ref[...] = _ffn_block(x_ref[...], h_ref[...], wg_ref, wu_ref, wd_ref, gf_ref)

    @pl.when(pl.program_id(0) == last)
    def _():
        os_ref[...] = _ffn_block(xs_ref[...], hs_ref[...], wg_ref, wu_ref, wd_ref, gf_ref)


def _ffn(x, h, xs, hs, wg, wu, wd, gf):
    N, D = x.shape
    assert N % TM_FFN == 0
    n = N // TM_FFN
    tile = pl.BlockSpec((TM_FFN, D), lambda i: (jnp.minimum(i, n - 1), 0))
    return pl.pallas_call(
        _ffn_kernel,
        grid=(n + 1,),
        in_specs=[tile, tile] + [_const_spec(a.shape) for a in (xs, hs, wg, wu, wd, gf)],
        out_specs=[tile, _const_spec(xs.shape)],
        out_shape=[jax.ShapeDtypeStruct((N, D), f32), jax.ShapeDtypeStruct(xs.shape, f32)],
        compiler_params=pltpu.CompilerParams(
            dimension_semantics=("arbitrary",), vmem_limit_bytes=VMEM_LIMIT),
        name="ffn",
    )(x, h, xs, hs, wg, wu, wd, gf)


def _rel_bias_row(table):
    n_rel = table.shape[1]
    far = W_BAND - REL_CLIP
    assert n_rel == 2 * REL_CLIP + 1 and far + n_rel + SUB <= F_LEN
    rep = lambda col, n: jnp.broadcast_to(table[:, col:col + 1], (table.shape[0], n))
    return jnp.concatenate([rep(n_rel - 1, far), table[:, ::-1],
                            rep(0, F_LEN - SUB - far - n_rel), rep(n_rel - 1, SUB)], axis=1)


def kernel(x_prompt, x_sample, cache_k, cache_v, state_conv, g_mix, w_in, rel_table, w_dw, b_dw,
           ln_g, ln_b, w_out, g_ffn, w_gate, w_up, w_down, g_final):
    assert g_mix.shape[0] == 1, "single-layer trunk: the final RMSNorm is fused into the FFN kernel"
    B, S, D = x_prompt.shape
    Bs, T, _ = x_sample.shape
    W = cache_k.shape[2]
    gm, gff, gf = g_mix[0].reshape(1, D), g_ffn[0].reshape(1, D), g_final.reshape(1, D)
    win = w_in[0]
    conv_params = (w_dw[0],) + tuple(a[0].reshape(1, CONV_WIDTH) for a in (b_dw, ln_g, ln_b))
    heads = lambda a: a.reshape(1, a.shape[0], a.shape[1], N_HEADS, HEAD_DIM)
    fvec = _rel_bias_row(rel_table[0])

    (qe, qo, k, v, oc, kf, vf, ct, qes, qos, kn, vn, ocs, nc, wout, wg, wu, wd) = _inproj(
        x_prompt, x_sample.reshape(Bs * T, D), state_conv[0], gm, win, *conv_params,
        (w_out[0], w_gate[0], w_up[0], w_down[0]))
    x1p, hp = _mix_prompt(qe, qo, k, v, oc, x_prompt, fvec, wout, gff)

    r3 = lambda a: a.reshape(Bs, T, ATTN_WIDTH)
    x1s, hs, nk, nv = _mix_sample(
        r3(qes), r3(qos), r3(kn), r3(vn),
        cache_k[0], cache_v[0],
        ocs, x_sample, fvec, wout, gff)

    flat = lambda a: a.reshape(-1, D)
    y_prompt, y_sample = _ffn(flat(x1p), flat(hp), flat(x1s), flat(hs), wg, wu, wd, gf)
    return (y_prompt.reshape(B, S, D), y_sample.reshape(Bs, T, D), heads(kf), heads(vf), ct[None],
            heads(nk), heads(nv), nc[None])
```

```python
import functools

import jax
import jax.numpy as jnp
from jax import lax
from jax.experimental import pallas as pl
from jax.experimental.pallas import tpu as pltpu

D_MODEL = 1024
CHUNK = 64
LEFT_CHUNKS = 8
W_BAND = LEFT_CHUNKS * CHUNK
ATTN_WIDTH = 512
N_HEADS = 8
HEAD_DIM = 64
CONV_WIDTH = 512
CONV_KERNEL = 31
CONV_PAST = CONV_KERNEL - 1
REL_CLIP = 128
D_FF = 2816
RMS_EPS = 1e-6
LN_EPS = 1e-5
NEG_INF = -1e30
SCALE = HEAD_DIM ** -0.5

LANES = 128
SUBLANES = 8
N_PAIRS = ATTN_WIDTH // LANES
TT_PROJ = 64
Z_STRIDE = TT_PROJ + SUBLANES
PROJ_COLS = 256
TQ = 512
N_KBLK = W_BAND // TQ + 1
SUB = 2 * CHUNK
WIN = W_BAND + SUB
F_LEN = 1024
LOOKAHEAD = 1
TM_FFN = 512
MXU_TILE = 256
FFN_COLS = 4 * MXU_TILE
CONV_ROWS = 64
CONV_STEPS = 16
CONV_LEAD = 32
VMEM_LIMIT = 56 * 1024 * 1024

f32 = jnp.float32
bf16 = jnp.bfloat16


def _rmsnorm(x, g):
    return (x * lax.rsqrt(jnp.mean(x * x, axis=-1, keepdims=True) + RMS_EPS)) * g


def _dot(a, b):
    return jnp.dot(a, b, preferred_element_type=f32)


def _dot_nt(a, b):
    return lax.dot_general(a, b, (((1,), (1,)), ((), ())), preferred_element_type=f32)


def _const_spec(shape):
    nd = len(shape)
    return pl.BlockSpec(shape, lambda *_: (0,) * nd, pipeline_mode=pl.Buffered(1))


def _project_stages(h_ref, w_ref, qe_store, qo_store, k_store, v_store):
    def issue(base, cs):
        return _dot(h_ref[...], w_ref[:, base + cs.start:base + cs.stop])

    def q_finish(cs, z):
        q = z * SCALE
        lane = lax.broadcasted_iota(jnp.int32, q.shape, 1)
        even = (lane & HEAD_DIM) == 0
        qe_store(cs, jnp.where(even, q, 0.0).astype(bf16))
        qo_store(cs, jnp.where(even, 0.0, q).astype(bf16))

    stages = []
    for base, finish in ((0, q_finish), (ATTN_WIDTH, k_store), (2 * ATTN_WIDTH, v_store)):
        for c0 in range(0, ATTN_WIDTH, PROJ_COLS):
            cs = slice(c0, c0 + PROJ_COLS)
            stages.append((functools.partial(issue, base, cs), functools.partial(finish, cs)))
    return stages


def _cast_weight(w32_ref, w_ref):
    for c0 in range(0, w_ref.shape[1], MXU_TILE):
        w_ref[:, c0:c0 + MXU_TILE] = w32_ref[:, c0:c0 + MXU_TILE].astype(bf16)


def _glu(h_ref, w_ref):
    base = 3 * ATTN_WIDTH
    a = _dot(h_ref[...], w_ref[:, base:base + CONV_WIDTH])
    g = _dot(h_ref[...], w_ref[:, base + CONV_WIDTH:base + 2 * CONV_WIDTH])
    return a * jax.nn.sigmoid(g)


def _conv_stages(cbuf_ref, sh_ref, y_ref, rows, wdw_ref, bdw_ref):
    lead = CONV_LEAD - CONV_PAST
    span = rows + CONV_LEAD - SUBLANES
    rb = min(CONV_ROWS, rows)

    def shift_stage(m):
        sh_ref[m - 1, 0:span, :] = cbuf_ref[m:m + span, :]

    def tap_stage(cs, r0):
        acc = jnp.zeros((rb, LANES), f32)
        for j in range(CONV_KERNEL):
            a, m = divmod(j + lead, SUBLANES)
            src = cbuf_ref if m == 0 else sh_ref.at[m - 1]
            lo = r0 + SUBLANES * a
            acc = acc + wdw_ref[j:j + 1, cs] * src[lo:lo + rb, cs]
        y_ref[r0:r0 + rb, cs] = acc + bdw_ref[:, cs]

    shifts = [functools.partial(shift_stage, m) for m in range(1, SUBLANES)]
    taps = [functools.partial(tap_stage, slice(c * LANES, (c + 1) * LANES), r0)
            for c in range(CONV_WIDTH // LANES) for r0 in range(0, rows, rb)]
    return shifts, taps


def _conv_time_major(ut_ref, y_ref, steps, wdw_ref, bdw_ref):
    for c in range(CONV_WIDTH // LANES):
        cs = slice(c * LANES, (c + 1) * LANES)
        for t0 in range(0, steps, CONV_STEPS):
            acc = jnp.zeros((CONV_STEPS, ut_ref.shape[2], LANES), f32)
            for j in range(CONV_KERNEL):
                acc = acc + wdw_ref[j:j + 1, cs] * ut_ref[c, t0 + j:t0 + j + CONV_STEPS]
            y_ref[c, t0:t0 + CONV_STEPS] = acc + bdw_ref[:, cs]


def _ln_swish_tiles(y_tiles, lng_ref, lnb_ref):
    n = len(y_tiles) * LANES
    mu = sum(jnp.sum(y, axis=-1, keepdims=True) for y in y_tiles) / n
    yc = [y - mu for y in y_tiles]
    var = sum(jnp.sum(y * y, axis=-1, keepdims=True) for y in yc) / n
    inv = lax.rsqrt(var + LN_EPS)
    out = []
    for c, y in enumerate(yc):
        cs = slice(c * LANES, (c + 1) * LANES)
        yn = y * inv * lng_ref[:, cs] + lnb_ref[:, cs]
        out.append(yn * jax.nn.sigmoid(yn))
    return out


def _ln_swish(y, lng_ref, lnb_ref):
    mu = jnp.mean(y, axis=-1, keepdims=True)
    yc = y - mu
    yn = yc * lax.rsqrt(jnp.mean(yc * yc, axis=-1, keepdims=True) + LN_EPS)
    yn = yn * lng_ref[...] + lnb_ref[...]
    return yn * jax.nn.sigmoid(yn)


def _pipeline(mxu_stages, valu_stages):
    pending = mxu_stages[0][0]()
    done = 0
    for n, (_, consume) in enumerate(mxu_stages):
        cur = pending
        if n + 1 < len(mxu_stages):
            pending = mxu_stages[n + 1][0]()
        upto = (n + 1) * len(valu_stages) // len(mxu_stages)
        for stage in valu_stages[done:upto]:
            stage()
        done = upto
        consume(cur)


def _inproj_kernel(x_ref, xs_ref, st_ref, g_ref, w32_ref, wdw_ref, bdw_ref, lng_ref, lnb_ref,
                   *refs, n_cast):
    cast_in, refs = refs[:n_cast], refs[n_cast:]
    prompt_out, sample_out = refs[:8], refs[8:14]
    cast_out, scratch = refs[14:14 + n_cast], refs[14 + n_cast:]
    w_ref, prompt_scratch, sample_scratch = scratch[0], scratch[1:5], scratch[5:]
    for src, dst in zip(cast_in, cast_out):
        dst[...] = src[...].astype(bf16)
    last = pl.num_programs(0) - 1

    @pl.when(pl.program_id(0) == 0)
    def _():
        _cast_weight(w32_ref, w_ref)
        ut_ref = prompt_scratch[2]
        ut_ref[:, 0:CONV_PAST] = jnp.zeros((ut_ref.shape[0], CONV_PAST) + ut_ref.shape[2:], f32)

    conv_refs = (wdw_ref, bdw_ref, lng_ref, lnb_ref)
    pl.when(pl.program_id(0) < last)(functools.partial(
        _inproj_prompt_tile, x_ref, g_ref, w_ref, conv_refs, prompt_out, prompt_scratch))
    pl.when(pl.program_id(0) == last)(functools.partial(
        _inproj_sample_tokens, xs_ref, st_ref, g_ref, w_ref, conv_refs, sample_out, sample_scratch))


def _inproj_prompt_tile(x_ref, g_ref, w_ref, conv_refs, out_refs, scratch):
    wdw_ref, bdw_ref, lng_ref, lnb_ref = conv_refs
    qe_ref, qo_ref, k_ref, v_ref, oc_ref, kf_ref, vf_ref, ct_ref = out_refs
    h_ref, z_ref, ut_ref, y_ref = scratch
    nb = x_ref.shape[0]
    rows = nb * TT_PROJ
    h_ref[...] = _rmsnorm(x_ref[...].reshape(rows, D_MODEL), g_ref[...]).astype(bf16)

    def store(ref):
        def put(cs, val):
            ref[:, :, cs] = val.reshape(nb, TT_PROJ, val.shape[-1])
        return put

    def kv_store(ref, full_ref):
        def put(cs, val):
            val = val.reshape(nb, TT_PROJ, val.shape[-1])
            ref[:, :, cs] = val.astype(bf16)
            full_ref[:, :, cs] = val
        return put

    u = _glu(h_ref, w_ref)
    for issue, consume in _project_stages(h_ref, w_ref, store(qe_ref), store(qo_ref),
                                          kv_store(k_ref, kf_ref), kv_store(v_ref, vf_ref)):
        consume(issue())
    lane_tiles = [slice(c * LANES, (c + 1) * LANES) for c in range(CONV_WIDTH // LANES)]
    for c, cs in enumerate(lane_tiles):
        for b in range(nb):
            z_ref[c, Z_STRIDE * b:Z_STRIDE * b + TT_PROJ, :] = u[b * TT_PROJ:(b + 1) * TT_PROJ, cs]
            ct_ref[b, :, cs] = z_ref[c, Z_STRIDE * b + TT_PROJ - CONV_PAST:Z_STRIDE * b + TT_PROJ, :]
        for t in range(TT_PROJ):
            ut_ref[c, CONV_PAST + t] = z_ref[c, pl.ds(t, nb, stride=Z_STRIDE), :]
    _conv_time_major(ut_ref, y_ref, TT_PROJ, wdw_ref, bdw_ref)
    for c, cs in enumerate(lane_tiles):
        for t in range(TT_PROJ):
            z_ref[c, pl.ds(t, nb, stride=Z_STRIDE), :] = y_ref[c, t]
        for b in range(nb):
            oc_ref[b, :, cs] = z_ref[c, Z_STRIDE * b:Z_STRIDE * b + TT_PROJ, :]

    ut_ref[:, 0:CONV_PAST] = ut_ref[:, TT_PROJ:TT_PROJ + CONV_PAST]


def _inproj(x, xs, st, g, w, wdw, bdw, lng, lnb, cast_weights):
    B, S, D = x.shape
    Ns, Bs = xs.shape[0], st.shape[0]
    T = Ns // Bs
    assert S % TT_PROJ == 0 and W_BAND % TT_PROJ == 0 and TT_PROJ >= CONV_PAST
    assert T >= CONV_PAST and T % SUBLANES == 0
    steps = S // TT_PROJ
    first_kept = (S - W_BAND) // TT_PROJ
    tile = lambda n: pl.BlockSpec((B, TT_PROJ, n), lambda i: (0, jnp.minimum(i, steps - 1), 0))
    kept = pl.BlockSpec((B, TT_PROJ, ATTN_WIDTH),
                        lambda i: (0, jnp.clip(i - first_kept, 0, steps - 1 - first_kept), 0))
    act = jax.ShapeDtypeStruct((B, S, ATTN_WIDTH), bf16)
    rows = B * TT_PROJ
    sample_shapes = ([jax.ShapeDtypeStruct((Ns, ATTN_WIDTH), bf16)] * 2
                     + [jax.ShapeDtypeStruct((Ns, ATTN_WIDTH), f32)] * 2
                     + [jax.ShapeDtypeStruct((Bs, T, CONV_WIDTH), bf16),
                        jax.ShapeDtypeStruct((Bs, CONV_PAST, CONV_WIDTH), f32)])

    def cast_spec(a):
        blk = next(r for r in (32, 64, 128, 256) if a.shape[0] % r == 0 and a.shape[0] // r <= steps)
        last = a.shape[0] // blk - 1
        return pl.BlockSpec((blk, a.shape[1]), lambda i: (jnp.minimum(i, last), 0))

    cast_specs = [cast_spec(a) for a in cast_weights]
    return pl.pallas_call(
        functools.partial(_inproj_kernel, n_cast=len(cast_weights)),
        grid=(steps + 1,),
        in_specs=[tile(D)] + [_const_spec(a.shape) for a in (xs, st, g, w, wdw, bdw, lng, lnb)]
                 + cast_specs,
        out_specs=[tile(ATTN_WIDTH)] * 5 + [kept] * 2 + [_const_spec((B, CONV_PAST, CONV_WIDTH))]
                  + [_const_spec(a.shape) for a in sample_shapes] + cast_specs,
        out_shape=[act] * 4 + [jax.ShapeDtypeStruct((B, S, CONV_WIDTH), f32)]
                  + [jax.ShapeDtypeStruct((B, W_BAND, ATTN_WIDTH), f32)] * 2
                  + [jax.ShapeDtypeStruct((B, CONV_PAST, CONV_WIDTH), f32)] + sample_shapes
                  + [jax.ShapeDtypeStruct(a.shape, bf16) for a in cast_weights],
        scratch_shapes=[pltpu.VMEM(w.shape, bf16),
                        pltpu.VMEM((rows, D), bf16),
                        pltpu.VMEM((CONV_WIDTH // LANES, B * Z_STRIDE, LANES), f32),
                        pltpu.VMEM((CONV_WIDTH // LANES, CONV_PAST + TT_PROJ, B, LANES), f32),
                        pltpu.VMEM((CONV_WIDTH // LANES, TT_PROJ, B, LANES), f32),
                        pltpu.VMEM((Ns, D), bf16),
                        pltpu.VMEM((CONV_LEAD + T, CONV_WIDTH), f32),
                        pltpu.VMEM((SUBLANES - 1, CONV_LEAD + T - SUBLANES, CONV_WIDTH), f32),
                        pltpu.VMEM((T, CONV_WIDTH), f32)],
        compiler_params=pltpu.CompilerParams(
            dimension_semantics=("arbitrary",), vmem_limit_bytes=VMEM_LIMIT),
        name="inproj",
    )(x, xs, st, g, w, wdw, bdw, lng, lnb, *cast_weights)


def _inproj_sample_tokens(x_ref, st_ref, g_ref, w_ref, conv_refs, out_refs, scratch):
    wdw_ref, bdw_ref, lng_ref, lnb_ref = conv_refs
    qe_ref, qo_ref, k_ref, v_ref, oc_ref, nc_ref = out_refs
    h_ref, cbuf_ref, sh_ref, y_ref = scratch
    B, T, _ = oc_ref.shape
    h_ref[...] = _rmsnorm(x_ref[...], g_ref[...]).astype(bf16)
    u = _glu(h_ref, w_ref)

    def store(ref):
        def put(cs, val):
            ref[:, cs] = val
        return put

    proj = _project_stages(h_ref, w_ref, store(qe_ref), store(qo_ref), store(k_ref), store(v_ref))
    lead = CONV_LEAD - CONV_PAST

    def conv_stage(b):
        cbuf_ref[0:SUBLANES, :] = jnp.zeros((SUBLANES, CONV_WIDTH), f32)
        cbuf_ref[lead:CONV_LEAD, :] = st_ref[b]
        cbuf_ref[CONV_LEAD:CONV_LEAD + T, :] = u[b * T:(b + 1) * T]
        nc_ref[b] = cbuf_ref[CONV_LEAD + T - CONV_PAST:CONV_LEAD + T, :]
        shifts, taps = _conv_stages(cbuf_ref, sh_ref, y_ref, T, wdw_ref, bdw_ref)
        for stage in shifts + taps:
            stage()
        oc_ref[b] = _ln_swish(y_ref[...], lng_ref, lnb_ref).astype(bf16)

    _pipeline(proj, [functools.partial(conv_stage, b) for b in range(B)])


def _toeplitz(frow, rows):
    return pltpu.roll(jnp.broadcast_to(frow, (rows, F_LEN)), 0, 1, stride=1, stride_axis=0)


def _softmax_parts(s_parts):
    m = functools.reduce(jnp.maximum, [jnp.max(s, axis=-1, keepdims=True) for s in s_parts])
    e_parts = [jnp.exp(s - m) for s in s_parts]
    l = functools.reduce(jnp.add, [jnp.sum(e, axis=-1, keepdims=True) for e in e_parts])
    return e_parts, l


def _residual_out(x_ref, cat_ref, oc, wout_ref, gn_ref, o_ref, hn_ref):
    x1 = (x_ref[0] + _dot(cat_ref[...], wout_ref[0:ATTN_WIDTH, :])
          + _dot(oc, wout_ref[ATTN_WIDTH:, :]))
    o_ref[0] = x1
    hn_ref[0] = _rmsnorm(x1, gn_ref[...]).astype(bf16)


def _mix_prompt_kernel(qe_ref, qo_ref, *refs):
    k_refs, v_refs = refs[:N_KBLK], refs[N_KBLK:2 * N_KBLK]
    (oc_ref, x_ref, f_ref, wout_ref, gn_ref, lng_ref, lnb_ref,
     o_ref, hn_ref, bias_ref, cat_ref) = refs[2 * N_KBLK:]
    i = pl.program_id(1)

    @pl.when((pl.program_id(0) == 0) & (i == 0))
    def _():
        for h in range(N_HEADS):
            p, par = divmod(h, 2)
            bias_ref[p, par * SUB:(par + 1) * SUB, :] = _toeplitz(f_ref[h:h + 1, :], SUB)[:, :WIN]

    row = lax.broadcasted_iota(jnp.int32, (2 * SUB, LANES), 0)
    lane = lax.broadcasted_iota(jnp.int32, (2 * SUB, LANES), 1)
    odd_chunk = (row & CHUNK) != 0
    edge_ok = {0: jnp.logical_not(odd_chunk & (lane < CHUNK)),
               WIN // LANES - 1: odd_chunk | (lane < CHUNK)}
    blk_ok = [i >= N_KBLK - 1 - b for b in range(N_KBLK - 1)]
    even_head = lax.broadcasted_iota(jnp.int32, (SUB, LANES), 1) < HEAD_DIM

    def pieces(r):
        out = []
        for b in range(N_KBLK):
            lo, hi = max(SUB * r, TQ * b), min(SUB * r + WIN, TQ * (b + 1))
            if lo < hi:
                out.append((b, lo - TQ * b, hi - TQ * b))
        return out

    def scores(r, p):
        rs, ps = slice(r * SUB, (r + 1) * SUB), slice(p * LANES, (p + 1) * LANES)
        q2 = jnp.concatenate([qe_ref[0, rs, ps], qo_ref[0, rs, ps]], axis=0)
        return jnp.concatenate(
            [_dot_nt(q2, k_refs[b][0, a:z, ps]) for b, a, z in pieces(r)], axis=1)

    def finish(r, p, s):
        rs, ps = slice(r * SUB, (r + 1) * SUB), slice(p * LANES, (p + 1) * LANES)
        s = s + bias_ref[p]
        tiles = []
        for t in range(WIN // LANES):
            st = s[:, t * LANES:(t + 1) * LANES]
            b = (SUB * r + LANES * t) // TQ
            ok = edge_ok.get(t)
            if b < N_KBLK - 1:
                ok = blk_ok[b] if ok is None else ok & blk_ok[b]
            tiles.append(st if ok is None else jnp.where(ok, st, NEG_INF))
        (e,), l = _softmax_parts([jnp.concatenate(tiles, axis=1)])
        pb = e.astype(bf16)
        o, c0 = None, 0
        for b, a, z in pieces(r):
            ob = _dot(pb[:, c0:c0 + z - a], v_refs[b][0, a:z, ps])
            o = ob if o is None else o + ob
            c0 += z - a
        o = o / l
        cat_ref[rs, ps] = jnp.where(even_head, o[:SUB], o[SUB:]).astype(bf16)

    units = [(r, p) for r in range(TQ // SUB) for p in range(N_PAIRS)]
    pending = [scores(*u) for u in units[:LOOKAHEAD]]
    for n, unit in enumerate(units):
        if n + LOOKAHEAD < len(units):
            pending.append(scores(*units[n + LOOKAHEAD]))
        finish(*unit, pending.pop(0))

    oc = _ln_swish(oc_ref[0], lng_ref, lnb_ref).astype(bf16)
    _residual_out(x_ref, cat_ref, oc, wout_ref, gn_ref, o_ref, hn_ref)


def _mix_prompt(qe, qo, k, v, oc, x, fvec, wout, gn, lng, lnb):
    B, S, D = x.shape
    assert S % TQ == 0 and W_BAND % TQ == 0 and TQ % SUB == 0 and WIN % LANES == 0
    qspec = pl.BlockSpec((1, TQ, ATTN_WIDTH), lambda b, i: (b, i, 0))
    kspec = lambda back: pl.BlockSpec(
        (1, TQ, ATTN_WIDTH), lambda b, i: (b, jnp.maximum(i - back, 0), 0))
    kspecs = [kspec(N_KBLK - 1 - n) for n in range(N_KBLK)]
    xspec = pl.BlockSpec((1, TQ, D), lambda b, i: (b, i, 0))
    return pl.pallas_call(
        _mix_prompt_kernel,
        grid=(B, S // TQ),
        in_specs=[qspec, qspec] + kspecs + kspecs + [qspec, xspec,
                  _const_spec(fvec.shape), _const_spec(wout.shape), _const_spec(gn.shape),
                  _const_spec(lng.shape), _const_spec(lnb.shape)],
        out_specs=[xspec, xspec],
        out_shape=[jax.ShapeDtypeStruct((B, S, D), f32), jax.ShapeDtypeStruct((B, S, D), bf16)],
        scratch_shapes=[pltpu.VMEM((N_PAIRS, 2 * SUB, WIN), f32),
                        pltpu.VMEM((TQ, ATTN_WIDTH), bf16)],
        compiler_params=pltpu.CompilerParams(
            dimension_semantics=("arbitrary", "arbitrary"), vmem_limit_bytes=VMEM_LIMIT),
        name="mix_prompt",
    )(qe, qo, *([k] * N_KBLK), *([v] * N_KBLK), oc, x, fvec, wout, gn, lng, lnb)


def _mix_sample_kernel(qe_ref, qo_ref, kn_ref, vn_ref, ck_ref, cv_ref, oc_ref, x_ref,
                       f_ref, wout_ref, gn_ref, o_ref, hn_ref, nk_ref, nv_ref, bias_ref, cat_ref):
    T = oc_ref.shape[1]
    W = ck_ref.shape[1]

    @pl.when(pl.program_id(0) == 0)
    def _():
        for h in range(N_HEADS):
            p, par = divmod(h, 2)
            bias_ref[p, par * T:(par + 1) * T, :] = _toeplitz(f_ref[h:h + 1, :], T)

    nk_ref[0, 0:W - T, :] = ck_ref[0, T:W, :]
    nk_ref[0, W - T:W, :] = kn_ref[0]
    nv_ref[0, 0:W - T, :] = cv_ref[0, T:W, :]
    nv_ref[0, W - T:W, :] = vn_ref[0]

    lane = lax.broadcasted_iota(jnp.int32, (T, LANES), 1)
    even_head = lane < HEAD_DIM
    for p in range(N_PAIRS):
        ps = slice(p * LANES, (p + 1) * LANES)
        kc = ck_ref[0, :, ps].astype(bf16)
        vc = cv_ref[0, :, ps].astype(bf16)
        kn = kn_ref[0, :, ps].astype(bf16)
        vn = vn_ref[0, :, ps].astype(bf16)
        q2 = jnp.concatenate([qe_ref[0, :, ps], qo_ref[0, :, ps]], axis=0)
        (ec, en), l = _softmax_parts([_dot_nt(q2, kc) + bias_ref[p, :, 0:W],
                                      _dot_nt(q2, kn) + bias_ref[p, :, W:W + T]])
        o = (_dot(ec.astype(bf16), vc) + _dot(en.astype(bf16), vn)) / l
        cat_ref[:, ps] = jnp.where(even_head, o[:T], o[T:]).astype(bf16)

    _residual_out(x_ref, cat_ref, oc_ref[0], wout_ref, gn_ref, o_ref, hn_ref)


def _mix_sample(qe, qo, kn, vn, ck, cv, oc, x, fvec, wout, gn):
    B, T, D = x.shape
    W = ck.shape[1]
    assert W == W_BAND and W + T <= F_LEN - T
    row = lambda r, n: pl.BlockSpec((1, r, n), lambda b: (b, 0, 0))
    return pl.pallas_call(
        _mix_sample_kernel,
        grid=(B,),
        in_specs=[row(T, ATTN_WIDTH)] * 4 + [row(W, ATTN_WIDTH)] * 2
                 + [row(T, CONV_WIDTH), row(T, D)]
                 + [_const_spec(a.shape) for a in (fvec, wout, gn)],
        out_specs=[row(T, D), row(T, D), row(W, ATTN_WIDTH), row(W, ATTN_WIDTH)],
        out_shape=[jax.ShapeDtypeStruct((B, T, D), f32),
                   jax.ShapeDtypeStruct((B, T, D), bf16),
                   jax.ShapeDtypeStruct((B, W, ATTN_WIDTH), f32),
                   jax.ShapeDtypeStruct((B, W, ATTN_WIDTH), f32)],
        scratch_shapes=[pltpu.VMEM((N_PAIRS, 2 * T, F_LEN), f32),
                        pltpu.VMEM((T, ATTN_WIDTH), bf16)],
        compiler_params=pltpu.CompilerParams(
            dimension_semantics=("arbitrary",), vmem_limit_bytes=VMEM_LIMIT),
        name="mix_sample",
    )(qe, qo, kn, vn, ck, cv, oc, x, fvec, wout, gn)


def _ffn_block(x, h, wg_ref, wu_ref, wd_ref, gf_ref):
    acc = x
    for c0 in range(0, D_FF, FFN_COLS):
        cs = slice(c0, min(c0 + FFN_COLS, D_FF))
        a = jax.nn.silu(_dot(h, wg_ref[:, cs])) * _dot(h, wu_ref[:, cs])
        acc = acc + _dot(a.astype(bf16), wd_ref[cs, :])
    return _rmsnorm(acc, gf_ref[...])


def _ffn_kernel(x_ref, h_ref, xs_ref, hs_ref, wg_ref, wu_ref, wd_ref, gf_ref, o_ref, os_ref):
    last = pl.num_programs(0) - 1

    @pl.when(pl.program_id(0) < last)
    def _():
        o_ref[...] = _ffn_block(x_ref[...], h_ref[...], wg_ref, wu_ref, wd_ref, gf_ref)

    @pl.when(pl.program_id(0) == last)
    def _():
        os_ref[...] = _ffn_block(xs_ref[...], hs_ref[...], wg_ref, wu_ref, wd_ref, gf_ref)


def _ffn(x, h, xs, hs, wg, wu, wd, gf):
    N, D = x.shape
    assert N % TM_FFN == 0
    n = N // TM_FFN
    tile = pl.BlockSpec((TM_FFN, D), lambda i: (jnp.minimum(i, n - 1), 0))
    return pl.pallas_call(
        _ffn_kernel,
        grid=(n + 1,),
        in_specs=[tile, tile] + [_const_spec(a.shape) for a in (xs, hs, wg, wu, wd, gf)],
        out_specs=[tile, _const_spec(xs.shape)],
        out_shape=[jax.ShapeDtypeStruct((N, D), f32), jax.ShapeDtypeStruct(xs.shape, f32)],
        compiler_params=pltpu.CompilerParams(
            dimension_semantics=("arbitrary",), vmem_limit_bytes=VMEM_LIMIT),
        name="ffn",
    )(x, h, xs, hs, wg, wu, wd, gf)


def _rel_bias_row(table):
    n_rel = table.shape[1]
    far = W_BAND - REL_CLIP
    assert n_rel == 2 * REL_CLIP + 1 and far + n_rel + SUB <= F_LEN
    rep = lambda col, n: jnp.broadcast_to(table[:, col:col + 1], (table.shape[0], n))
    return jnp.concatenate([rep(n_rel - 1, far), table[:, ::-1],
                            rep(0, F_LEN - SUB - far - n_rel), rep(n_rel - 1, SUB)], axis=1)


def kernel(x_prompt, x_sample, cache_k, cache_v, state_conv, g_mix, w_in, rel_table, w_dw, b_dw,
           ln_g, ln_b, w_out, g_ffn, w_gate, w_up, w_down, g_final):
    assert g_mix.shape[0] == 1, "single-layer trunk: the final RMSNorm is fused into the FFN kernel"
    B, S, D = x_prompt.shape
    Bs, T, _ = x_sample.shape
    W = cache_k.shape[2]
    gm, gff, gf = g_mix[0].reshape(1, D), g_ffn[0].reshape(1, D), g_final.reshape(1, D)
    win = w_in[0]
    conv_params = (w_dw[0],) + tuple(a[0].reshape(1, CONV_WIDTH) for a in (b_dw, ln_g, ln_b))
    heads = lambda a: a.reshape(1, a.shape[0], a.shape[1], N_HEADS, HEAD_DIM)
    fvec = _rel_bias_row(rel_table[0])

    (qe, qo, k, v, oc, kf, vf, ct, qes, qos, kn, vn, ocs, nc, wout, wg, wu, wd) = _inproj(
        x_prompt, x_sample.reshape(Bs * T, D), state_conv[0], gm, win, *conv_params,
        (w_out[0], w_gate[0], w_up[0], w_down[0]))
    x1p, hp = _mix_prompt(qe, qo, k, v, oc, x_prompt, fvec, wout, gff, *conv_params[2:])

    r3 = lambda a: a.reshape(Bs, T, ATTN_WIDTH)
    x1s, hs, nk, nv = _mix_sample(
        r3(qes), r3(qos), r3(kn), r3(vn),
        cache_k[0].reshape(Bs, W, ATTN_WIDTH), cache_v[0].reshape(Bs, W, ATTN_WIDTH),
        ocs, x_sample, fvec, wout, gff)

    flat = lambda a: a.reshape(-1, D)
    y_prompt, y_sample = _ffn(flat(x1p), flat(hp), flat(x1s), flat(hs), wg, wu, wd, gf)
    return (y_prompt.reshape(B, S, D), y_sample.reshape(Bs, T, D), heads(kf), heads(vf), ct[None],
            heads(nk), heads(nv), nc[None])
```

```python
import functools

import jax
import jax.numpy as jnp
from jax import lax
from jax.experimental import pallas as pl
from jax.experimental.pallas import tpu as pltpu

D_MODEL = 1024
CHUNK = 64
LEFT_CHUNKS = 8
W_BAND = LEFT_CHUNKS * CHUNK
ATTN_WIDTH = 512
N_HEADS = 8
HEAD_DIM = 64
CONV_WIDTH = 512
CONV_KERNEL = 31
CONV_PAST = CONV_KERNEL - 1
REL_CLIP = 128
D_FF = 2816
RMS_EPS = 1e-6
LN_EPS = 1e-5
NEG_INF = -1e30
SCALE = HEAD_DIM ** -0.5

LANES = 128
SUBLANES = 8
N_PAIRS = ATTN_WIDTH // LANES
TT_PROJ = 64
Z_STRIDE = TT_PROJ + SUBLANES
PROJ_COLS = 256
TQ = 512
N_KBLK = W_BAND // TQ + 1
SUB = 2 * CHUNK
WIN = W_BAND + SUB
F_LEN = 1024
LOOKAHEAD = 1
TM_FFN = 512
MXU_TILE = 256
FFN_COLS = 4 * MXU_TILE
CONV_ROWS = 64
CONV_STEPS = 16
CONV_LEAD = 32
VMEM_LIMIT = 56 * 1024 * 1024

f32 = jnp.float32
bf16 = jnp.bfloat16


def _rmsnorm(x, g):
    return (x * lax.rsqrt(jnp.mean(x * x, axis=-1, keepdims=True) + RMS_EPS)) * g


def _dot(a, b):
    return jnp.dot(a, b, preferred_element_type=f32)


def _dot_nt(a, b):
    return lax.dot_general(a, b, (((1,), (1,)), ((), ())), preferred_element_type=f32)


def _const_spec(shape):
    nd = len(shape)
    return pl.BlockSpec(shape, lambda *_: (0,) * nd, pipeline_mode=pl.Buffered(1))


def _project_stages(h_ref, w_ref, q_store, k_store, v_store):
    def issue(base, cs):
        return _dot(h_ref[...], w_ref[:, base + cs.start:base + cs.stop])

    def q_finish(cs, z):
        q_store(cs, (z * SCALE).astype(bf16))

    stages = []
    for base, finish in ((0, q_finish), (ATTN_WIDTH, k_store), (2 * ATTN_WIDTH, v_store)):
        for c0 in range(0, ATTN_WIDTH, PROJ_COLS):
            cs = slice(c0, c0 + PROJ_COLS)
            stages.append((functools.partial(issue, base, cs), functools.partial(finish, cs)))
    return stages


def _cast_weight(w32_ref, w_ref):
    for c0 in range(0, w_ref.shape[1], MXU_TILE):
        w_ref[:, c0:c0 + MXU_TILE] = w32_ref[:, c0:c0 + MXU_TILE].astype(bf16)


def _glu(h_ref, w_ref):
    base = 3 * ATTN_WIDTH
    a = _dot(h_ref[...], w_ref[:, base:base + CONV_WIDTH])
    g = _dot(h_ref[...], w_ref[:, base + CONV_WIDTH:base + 2 * CONV_WIDTH])
    return a * jax.nn.sigmoid(g)


def _conv_stages(cbuf_ref, sh_ref, y_ref, rows, wdw_ref, bdw_ref):
    lead = CONV_LEAD - CONV_PAST
    span = rows + CONV_LEAD - SUBLANES
    rb = min(CONV_ROWS, rows)

    def shift_stage(m):
        sh_ref[m - 1, 0:span, :] = cbuf_ref[m:m + span, :]

    def tap_stage(cs, r0):
        acc = jnp.zeros((rb, LANES), f32)
        for j in range(CONV_KERNEL):
            a, m = divmod(j + lead, SUBLANES)
            src = cbuf_ref if m == 0 else sh_ref.at[m - 1]
            lo = r0 + SUBLANES * a
            acc = acc + wdw_ref[j:j + 1, cs] * src[lo:lo + rb, cs]
        y_ref[r0:r0 + rb, cs] = acc + bdw_ref[:, cs]

    shifts = [functools.partial(shift_stage, m) for m in range(1, SUBLANES)]
    taps = [functools.partial(tap_stage, slice(c * LANES, (c + 1) * LANES), r0)
            for c in range(CONV_WIDTH // LANES) for r0 in range(0, rows, rb)]
    return shifts, taps


def _conv_time_major(ut_ref, y_ref, steps, wdw_ref, bdw_ref):
    for c in range(CONV_WIDTH // LANES):
        cs = slice(c * LANES, (c + 1) * LANES)
        for t0 in range(0, steps, CONV_STEPS):
            acc = jnp.zeros((CONV_STEPS, ut_ref.shape[2], LANES), f32)
            for j in range(CONV_KERNEL):
                acc = acc + wdw_ref[j:j + 1, cs] * ut_ref[c, t0 + j:t0 + j + CONV_STEPS]
            y_ref[c, t0:t0 + CONV_STEPS] = acc + bdw_ref[:, cs]


def _ln_swish_tiles(y_tiles, lng_ref, lnb_ref):
    n = len(y_tiles) * LANES
    mu = sum(jnp.sum(y, axis=-1, keepdims=True) for y in y_tiles) / n
    yc = [y - mu for y in y_tiles]
    var = sum(jnp.sum(y * y, axis=-1, keepdims=True) for y in yc) / n
    inv = lax.rsqrt(var + LN_EPS)
    out = []
    for c, y in enumerate(yc):
        cs = slice(c * LANES, (c + 1) * LANES)
        yn = y * inv * lng_ref[:, cs] + lnb_ref[:, cs]
        out.append(yn * jax.nn.sigmoid(yn))
    return out


def _ln_swish(y, lng_ref, lnb_ref):
    mu = jnp.mean(y, axis=-1, keepdims=True)
    yc = y - mu
    yn = yc * lax.rsqrt(jnp.mean(yc * yc, axis=-1, keepdims=True) + LN_EPS)
    yn = yn * lng_ref[...] + lnb_ref[...]
    return yn * jax.nn.sigmoid(yn)


def _pipeline(mxu_stages, valu_stages):
    pending = mxu_stages[0][0]()
    done = 0
    for n, (_, consume) in enumerate(mxu_stages):
        cur = pending
        if n + 1 < len(mxu_stages):
            pending = mxu_stages[n + 1][0]()
        upto = (n + 1) * len(valu_stages) // len(mxu_stages)
        for stage in valu_stages[done:upto]:
            stage()
        done = upto
        consume(cur)


def _inproj_kernel(x_ref, xs_ref, st_ref, g_ref, w32_ref, wdw_ref, bdw_ref, lng_ref, lnb_ref,
                   *refs, n_cast):
    cast_in, refs = refs[:n_cast], refs[n_cast:]
    prompt_out, sample_out = refs[:7], refs[7:12]
    cast_out, scratch = refs[12:12 + n_cast], refs[12 + n_cast:]
    w_ref, prompt_scratch, sample_scratch = scratch[0], scratch[1:5], scratch[5:]
    for src, dst in zip(cast_in, cast_out):
        dst[...] = src[...].astype(bf16)
    last = pl.num_programs(0) - 1

    @pl.when(pl.program_id(0) == 0)
    def _():
        _cast_weight(w32_ref, w_ref)
        ut_ref = prompt_scratch[2]
        ut_ref[:, 0:CONV_PAST] = jnp.zeros((ut_ref.shape[0], CONV_PAST) + ut_ref.shape[2:], f32)

    conv_refs = (wdw_ref, bdw_ref, lng_ref, lnb_ref)
    pl.when(pl.program_id(0) < last)(functools.partial(
        _inproj_prompt_tile, x_ref, g_ref, w_ref, conv_refs, prompt_out, prompt_scratch))
    pl.when(pl.program_id(0) == last)(functools.partial(
        _inproj_sample_tokens, xs_ref, st_ref, g_ref, w_ref, conv_refs, sample_out, sample_scratch))


def _inproj_prompt_tile(x_ref, g_ref, w_ref, conv_refs, out_refs, scratch):
    wdw_ref, bdw_ref, lng_ref, lnb_ref = conv_refs
    q_ref, k_ref, v_ref, oc_ref, kf_ref, vf_ref, ct_ref = out_refs
    h_ref, z_ref, ut_ref, y_ref = scratch
    nb = x_ref.shape[0]
    rows = nb * TT_PROJ
    h_ref[...] = _rmsnorm(x_ref[...].reshape(rows, D_MODEL), g_ref[...]).astype(bf16)

    def store(ref):
        def put(cs, val):
            ref[:, :, cs] = val.reshape(nb, TT_PROJ, val.shape[-1])
        return put

    def kv_store(ref, full_ref):
        def put(cs, val):
            val = val.reshape(nb, TT_PROJ, val.shape[-1])
            ref[:, :, cs] = val.astype(bf16)
            full_ref[:, :, cs] = val
        return put

    u = _glu(h_ref, w_ref)
    for issue, consume in _project_stages(h_ref, w_ref, store(q_ref),
                                          kv_store(k_ref, kf_ref), kv_store(v_ref, vf_ref)):
        consume(issue())
    lane_tiles = [slice(c * LANES, (c + 1) * LANES) for c in range(CONV_WIDTH // LANES)]
    for c, cs in enumerate(lane_tiles):
        for b in range(nb):
            z_ref[c, Z_STRIDE * b:Z_STRIDE * b + TT_PROJ, :] = u[b * TT_PROJ:(b + 1) * TT_PROJ, cs]
            ct_ref[b, :, cs] = z_ref[c, Z_STRIDE * b + TT_PROJ - CONV_PAST:Z_STRIDE * b + TT_PROJ, :]
        for t in range(TT_PROJ):
            ut_ref[c, CONV_PAST + t] = z_ref[c, pl.ds(t, nb, stride=Z_STRIDE), :]
    _conv_time_major(ut_ref, y_ref, TT_PROJ, wdw_ref, bdw_ref)
    for c, cs in enumerate(lane_tiles):
        for t in range(TT_PROJ):
            z_ref[c, pl.ds(t, nb, stride=Z_STRIDE), :] = y_ref[c, t]
        for b in range(nb):
            oc_ref[b, :, cs] = z_ref[c, Z_STRIDE * b:Z_STRIDE * b + TT_PROJ, :]

    ut_ref[:, 0:CONV_PAST] = ut_ref[:, TT_PROJ:TT_PROJ + CONV_PAST]


def _inproj(x, xs, st, g, w, wdw, bdw, lng, lnb, cast_weights):
    B, S, D = x.shape
    Ns, Bs = xs.shape[0], st.shape[0]
    T = Ns // Bs
    assert S % TT_PROJ == 0 and W_BAND % TT_PROJ == 0 and TT_PROJ >= CONV_PAST
    assert T >= CONV_PAST and T % SUBLANES == 0
    steps = S // TT_PROJ
    first_kept = (S - W_BAND) // TT_PROJ
    tile = lambda n: pl.BlockSpec((B, TT_PROJ, n), lambda i: (0, jnp.minimum(i, steps - 1), 0))
    kept = pl.BlockSpec((B, TT_PROJ, ATTN_WIDTH),
                        lambda i: (0, jnp.clip(i - first_kept, 0, steps - 1 - first_kept), 0))
    act = jax.ShapeDtypeStruct((B, S, ATTN_WIDTH), bf16)
    rows = B * TT_PROJ
    sample_shapes = ([jax.ShapeDtypeStruct((Ns, ATTN_WIDTH), bf16)]
                     + [jax.ShapeDtypeStruct((Ns, ATTN_WIDTH), f32)] * 2
                     + [jax.ShapeDtypeStruct((Bs, T, CONV_WIDTH), bf16),
                        jax.ShapeDtypeStruct((Bs, CONV_PAST, CONV_WIDTH), f32)])

    def cast_spec(a):
        blk = next(r for r in (32, 64, 128, 256) if a.shape[0] % r == 0 and a.shape[0] // r <= steps)
        last = a.shape[0] // blk - 1
        return pl.BlockSpec((blk, a.shape[1]), lambda i: (jnp.minimum(i, last), 0))

    cast_specs = [cast_spec(a) for a in cast_weights]
    return pl.pallas_call(
        functools.partial(_inproj_kernel, n_cast=len(cast_weights)),
        grid=(steps + 1,),
        in_specs=[tile(D)] + [_const_spec(a.shape) for a in (xs, st, g, w, wdw, bdw, lng, lnb)]
                 + cast_specs,
        out_specs=[tile(ATTN_WIDTH)] * 4 + [kept] * 2 + [_const_spec((B, CONV_PAST, CONV_WIDTH))]
                  + [_const_spec(a.shape) for a in sample_shapes] + cast_specs,
        out_shape=[act] * 3 + [jax.ShapeDtypeStruct((B, S, CONV_WIDTH), f32)]
                  + [jax.ShapeDtypeStruct((B, W_BAND, ATTN_WIDTH), f32)] * 2
                  + [jax.ShapeDtypeStruct((B, CONV_PAST, CONV_WIDTH), f32)] + sample_shapes
                  + [jax.ShapeDtypeStruct(a.shape, bf16) for a in cast_weights],
        scratch_shapes=[pltpu.VMEM(w.shape, bf16),
                        pltpu.VMEM((rows, D), bf16),
                        pltpu.VMEM((CONV_WIDTH // LANES, B * Z_STRIDE, LANES), f32),
                        pltpu.VMEM((CONV_WIDTH // LANES, CONV_PAST + TT_PROJ, B, LANES), f32),
                        pltpu.VMEM((CONV_WIDTH // LANES, TT_PROJ, B, LANES), f32),
                        pltpu.VMEM((Ns, D), bf16),
                        pltpu.VMEM((CONV_LEAD + T, CONV_WIDTH), f32),
                        pltpu.VMEM((SUBLANES - 1, CONV_LEAD + T - SUBLANES, CONV_WIDTH), f32),
                        pltpu.VMEM((T, CONV_WIDTH), f32)],
        compiler_params=pltpu.CompilerParams(
            dimension_semantics=("arbitrary",), vmem_limit_bytes=VMEM_LIMIT),
        name="inproj",
    )(x, xs, st, g, w, wdw, bdw, lng, lnb, *cast_weights)


def _inproj_sample_tokens(x_ref, st_ref, g_ref, w_ref, conv_refs, out_refs, scratch):
    wdw_ref, bdw_ref, lng_ref, lnb_ref = conv_refs
    q_ref, k_ref, v_ref, oc_ref, nc_ref = out_refs
    h_ref, cbuf_ref, sh_ref, y_ref = scratch
    B, T, _ = oc_ref.shape
    h_ref[...] = _rmsnorm(x_ref[...], g_ref[...]).astype(bf16)
    u = _glu(h_ref, w_ref)

    def store(ref):
        def put(cs, val):
            ref[:, cs] = val
        return put

    proj = _project_stages(h_ref, w_ref, store(q_ref), store(k_ref), store(v_ref))
    lead = CONV_LEAD - CONV_PAST

    def conv_stage(b):
        cbuf_ref[0:SUBLANES, :] = jnp.zeros((SUBLANES, CONV_WIDTH), f32)
        cbuf_ref[lead:CONV_LEAD, :] = st_ref[b]
        cbuf_ref[CONV_LEAD:CONV_LEAD + T, :] = u[b * T:(b + 1) * T]
        nc_ref[b] = cbuf_ref[CONV_LEAD + T - CONV_PAST:CONV_LEAD + T, :]
        shifts, taps = _conv_stages(cbuf_ref, sh_ref, y_ref, T, wdw_ref, bdw_ref)
        for stage in shifts + taps:
            stage()
        oc_ref[b] = _ln_swish(y_ref[...], lng_ref, lnb_ref).astype(bf16)

    _pipeline(proj, [functools.partial(conv_stage, b) for b in range(B)])


def _toeplitz(frow, rows):
    return pltpu.roll(jnp.broadcast_to(frow, (rows, F_LEN)), 0, 1, stride=1, stride_axis=0)


def _softmax_parts(s_parts):
    m = functools.reduce(jnp.maximum, [jnp.max(s, axis=-1, keepdims=True) for s in s_parts])
    e_parts = [jnp.exp(s - m) for s in s_parts]
    l = functools.reduce(jnp.add, [jnp.sum(e, axis=-1, keepdims=True) for e in e_parts])
    return e_parts, l


def _stack_heads(q):
    lane = lax.broadcasted_iota(jnp.int32, q.shape, 1)
    qf = q.astype(f32)
    even = jnp.where(lane < HEAD_DIM, qf, 0.0)
    odd = jnp.where(lane < HEAD_DIM, 0.0, qf)
    return jnp.concatenate([even, odd], axis=0).astype(bf16)


def _residual_out(x_ref, cat_ref, oc, wout_ref, gn_ref, o_ref, hn_ref):
    x1 = (x_ref[0] + _dot(cat_ref[...], wout_ref[0:ATTN_WIDTH, :])
          + _dot(oc, wout_ref[ATTN_WIDTH:, :]))
    o_ref[0] = x1
    hn_ref[0] = _rmsnorm(x1, gn_ref[...]).astype(bf16)


def _mix_prompt_kernel(q_ref, *refs):
    k_refs, v_refs = refs[:N_KBLK], refs[N_KBLK:2 * N_KBLK]
    (oc_ref, x_ref, f_ref, wout_ref, gn_ref, lng_ref, lnb_ref,
     o_ref, hn_ref, bias_ref, cat_ref) = refs[2 * N_KBLK:]
    i = pl.program_id(1)

    @pl.when((pl.program_id(0) == 0) & (i == 0))
    def _():
        for h in range(N_HEADS):
            p, par = divmod(h, 2)
            bias_ref[p, par * SUB:(par + 1) * SUB, :] = _toeplitz(f_ref[h:h + 1, :], SUB)[:, :WIN]

    row = lax.broadcasted_iota(jnp.int32, (2 * SUB, LANES), 0)
    lane = lax.broadcasted_iota(jnp.int32, (2 * SUB, LANES), 1)
    odd_chunk = (row & CHUNK) != 0
    edge_ok = {0: jnp.logical_not(odd_chunk & (lane < CHUNK)),
               WIN // LANES - 1: odd_chunk | (lane < CHUNK)}
    blk_ok = [i >= N_KBLK - 1 - b for b in range(N_KBLK - 1)]
    even_head = lax.broadcasted_iota(jnp.int32, (SUB, LANES), 1) < HEAD_DIM

    def pieces(r):
        out = []
        for b in range(N_KBLK):
            lo, hi = max(SUB * r, TQ * b), min(SUB * r + WIN, TQ * (b + 1))
            if lo < hi:
                out.append((b, lo - TQ * b, hi - TQ * b))
        return out

    def scores(r, p):
        rs, ps = slice(r * SUB, (r + 1) * SUB), slice(p * LANES, (p + 1) * LANES)
        q2 = _stack_heads(q_ref[0, rs, ps])
        return jnp.concatenate(
            [_dot_nt(q2, k_refs[b][0, a:z, ps]) for b, a, z in pieces(r)], axis=1)

    def finish(r, p, s):
        rs, ps = slice(r * SUB, (r + 1) * SUB), slice(p * LANES, (p + 1) * LANES)
        s = s + bias_ref[p]
        tiles = []
        for t in range(WIN // LANES):
            st = s[:, t * LANES:(t + 1) * LANES]
            b = (SUB * r + LANES * t) // TQ
            ok = edge_ok.get(t)
            if b < N_KBLK - 1:
                ok = blk_ok[b] if ok is None else ok & blk_ok[b]
            tiles.append(st if ok is None else jnp.where(ok, st, NEG_INF))
        (e,), l = _softmax_parts([jnp.concatenate(tiles, axis=1)])
        pb = e.astype(bf16)
        o, c0 = None, 0
        for b, a, z in pieces(r):
            ob = _dot(pb[:, c0:c0 + z - a], v_refs[b][0, a:z, ps])
            o = ob if o is None else o + ob
            c0 += z - a
        o = o / l
        cat_ref[rs, ps] = jnp.where(even_head, o[:SUB], o[SUB:]).astype(bf16)

    units = [(r, p) for r in range(TQ // SUB) for p in range(N_PAIRS)]
    pending = [scores(*u) for u in units[:LOOKAHEAD]]
    for n, unit in enumerate(units):
        if n + LOOKAHEAD < len(units):
            pending.append(scores(*units[n + LOOKAHEAD]))
        finish(*unit, pending.pop(0))

    oc = _ln_swish(oc_ref[0], lng_ref, lnb_ref).astype(bf16)
    _residual_out(x_ref, cat_ref, oc, wout_ref, gn_ref, o_ref, hn_ref)


def _mix_prompt(q, k, v, oc, x, fvec, wout, gn, lng, lnb):
    B, S, D = x.shape
    assert S % TQ == 0 and W_BAND % TQ == 0 and TQ % SUB == 0 and WIN % LANES == 0
    qspec = pl.BlockSpec((1, TQ, ATTN_WIDTH), lambda b, i: (b, i, 0))
    kspec = lambda back: pl.BlockSpec(
        (1, TQ, ATTN_WIDTH), lambda b, i: (b, jnp.maximum(i - back, 0), 0))
    kspecs = [kspec(N_KBLK - 1 - n) for n in range(N_KBLK)]
    xspec = pl.BlockSpec((1, TQ, D), lambda b, i: (b, i, 0))
    return pl.pallas_call(
        _mix_prompt_kernel,
        grid=(B, S // TQ),
        in_specs=[qspec] + kspecs + kspecs + [qspec, xspec,
                  _const_spec(fvec.shape), _const_spec(wout.shape), _const_spec(gn.shape),
                  _const_spec(lng.shape), _const_spec(lnb.shape)],
        out_specs=[xspec, xspec],
        out_shape=[jax.ShapeDtypeStruct((B, S, D), f32), jax.ShapeDtypeStruct((B, S, D), bf16)],
        scratch_shapes=[pltpu.VMEM((N_PAIRS, 2 * SUB, WIN), f32),
                        pltpu.VMEM((TQ, ATTN_WIDTH), bf16)],
        compiler_params=pltpu.CompilerParams(
            dimension_semantics=("arbitrary", "arbitrary"), vmem_limit_bytes=VMEM_LIMIT),
        name="mix_prompt",
    )(q, *([k] * N_KBLK), *([v] * N_KBLK), oc, x, fvec, wout, gn, lng, lnb)


def _mix_sample_kernel(q_ref, kn_ref, vn_ref, ck_ref, cv_ref, oc_ref, x_ref,
                       f_ref, wout_ref, gn_ref, o_ref, hn_ref, nk_ref, nv_ref, bias_ref, cat_ref):
    T = oc_ref.shape[1]
    W = ck_ref.shape[1]

    @pl.when(pl.program_id(0) == 0)
    def _():
        for h in range(N_HEADS):
            p, par = divmod(h, 2)
            bias_ref[p, par * T:(par + 1) * T, :] = _toeplitz(f_ref[h:h + 1, :], T)

    nk_ref[0, 0:W - T, :] = ck_ref[0, T:W, :]
    nk_ref[0, W - T:W, :] = kn_ref[0]
    nv_ref[0, 0:W - T, :] = cv_ref[0, T:W, :]
    nv_ref[0, W - T:W, :] = vn_ref[0]

    lane = lax.broadcasted_iota(jnp.int32, (T, LANES), 1)
    even_head = lane < HEAD_DIM
    for p in range(N_PAIRS):
        ps = slice(p * LANES, (p + 1) * LANES)
        kc = ck_ref[0, :, ps].astype(bf16)
        vc = cv_ref[0, :, ps].astype(bf16)
        kn = kn_ref[0, :, ps].astype(bf16)
        vn = vn_ref[0, :, ps].astype(bf16)
        q2 = _stack_heads(q_ref[0, :, ps])
        (ec, en), l = _softmax_parts([_dot_nt(q2, kc) + bias_ref[p, :, 0:W],
                                      _dot_nt(q2, kn) + bias_ref[p, :, W:W + T]])
        o = (_dot(ec.astype(bf16), vc) + _dot(en.astype(bf16), vn)) / l
        cat_ref[:, ps] = jnp.where(even_head, o[:T], o[T:]).astype(bf16)

    _residual_out(x_ref, cat_ref, oc_ref[0], wout_ref, gn_ref, o_ref, hn_ref)


def _mix_sample(q, kn, vn, ck, cv, oc, x, fvec, wout, gn):
    B, T, D = x.shape
    W = ck.shape[1]
    assert W == W_BAND and W + T <= F_LEN - T
    row = lambda r, n: pl.BlockSpec((1, r, n), lambda b: (b, 0, 0))
    return pl.pallas_call(
        _mix_sample_kernel,
        grid=(B,),
        in_specs=[row(T, ATTN_WIDTH)] * 3 + [row(W, ATTN_WIDTH)] * 2
                 + [row(T, CONV_WIDTH), row(T, D)]
                 + [_const_spec(a.shape) for a in (fvec, wout, gn)],
        out_specs=[row(T, D), row(T, D), row(W, ATTN_WIDTH), row(W, ATTN_WIDTH)],
        out_shape=[jax.ShapeDtypeStruct((B, T, D), f32),
                   jax.ShapeDtypeStruct((B, T, D), bf16),
                   jax.ShapeDtypeStruct((B, W, ATTN_WIDTH), f32),
                   jax.ShapeDtypeStruct((B, W, ATTN_WIDTH), f32)],
        scratch_shapes=[pltpu.VMEM((N_PAIRS, 2 * T, F_LEN), f32),
                        pltpu.VMEM((T, ATTN_WIDTH), bf16)],
        compiler_params=pltpu.CompilerParams(
            dimension_semantics=("arbitrary",), vmem_limit_bytes=VMEM_LIMIT),
        name="mix_sample",
    )(q, kn, vn, ck, cv, oc, x, fvec, wout, gn)


def _ffn_block(x, h, wg_ref, wu_ref, wd_ref, gf_ref):
    acc = x
    for c0 in range(0, D_FF, FFN_COLS):
        cs = slice(c0, min(c0 + FFN_COLS, D_FF))
        a = jax.nn.silu(_dot(h, wg_ref[:, cs])) * _dot(h, wu_ref[:, cs])
        acc = acc + _dot(a.astype(bf16), wd_ref[cs, :])
    return _rmsnorm(acc, gf_ref[...])


def _ffn_kernel(x_ref, h_ref, xs_ref, hs_ref, wg_ref, wu_ref, wd_ref, gf_ref, o_ref, os_ref):
    last = pl.num_programs(0) - 1

    @pl.when(pl.program_id(0) < last)
    def _():
        o_ref[...] = _ffn_block(x_ref[...], h_ref[...], wg_ref, wu_ref, wd_ref, gf_ref)

    @pl.when(pl.program_id(0) == last)
    def _():
        os_ref[...] = _ffn_block(xs_ref[...], hs_ref[...], wg_ref, wu_ref, wd_ref, gf_ref)


def _ffn(x, h, xs, hs, wg, wu, wd, gf):
    N, D = x.shape
    assert N % TM_FFN == 0
    n = N // TM_FFN
    tile = pl.BlockSpec((TM_FFN, D), lambda i: (jnp.minimum(i, n - 1), 0))
    return pl.pallas_call(
        _ffn_kernel,
        grid=(n + 1,),
        in_specs=[tile, tile] + [_const_spec(a.shape) for a in (xs, hs, wg, wu, wd, gf)],
        out_specs=[tile, _const_spec(xs.shape)],
        out_shape=[jax.ShapeDtypeStruct((N, D), f32), jax.ShapeDtypeStruct(xs.shape, f32)],
        compiler_params=pltpu.CompilerParams(
            dimension_semantics=("arbitrary",), vmem_limit_bytes=VMEM_LIMIT),
        name="ffn",
    )(x, h, xs, hs, wg, wu, wd, gf)


def _rel_bias_row(table):
    n_rel = table.shape[1]
    far = W_BAND - REL_CLIP
    assert n_rel == 2 * REL_CLIP + 1 and far + n_rel + SUB <= F_LEN
    rep = lambda col, n: jnp.broadcast_to(table[:, col:col + 1], (table.shape[0], n))
    return jnp.concatenate([rep(n_rel - 1, far), table[:, ::-1],
                            rep(0, F_LEN - SUB - far - n_rel), rep(n_rel - 1, SUB)], axis=1)


def kernel(x_prompt, x_sample, cache_k, cache_v, state_conv, g_mix, w_in, rel_table, w_dw, b_dw,
           ln_g, ln_b, w_out, g_ffn, w_gate, w_up, w_down, g_final):
    assert g_mix.shape[0] == 1, "single-layer trunk: the final RMSNorm is fused into the FFN kernel"
    B, S, D = x_prompt.shape
    Bs, T, _ = x_sample.shape
    W = cache_k.shape[2]
    gm, gff, gf = g_mix[0].reshape(1, D), g_ffn[0].reshape(1, D), g_final.reshape(1, D)
    win = w_in[0]
    conv_params = (w_dw[0],) + tuple(a[0].reshape(1, CONV_WIDTH) for a in (b_dw, ln_g, ln_b))
    heads = lambda a: a.reshape(1, a.shape[0], a.shape[1], N_HEADS, HEAD_DIM)
    fvec = _rel_bias_row(rel_table[0])

    (q, k, v, oc, kf, vf, ct, qs, kn, vn, ocs, nc, wout, wg, wu, wd) = _inproj(
        x_prompt, x_sample.reshape(Bs * T, D), state_conv[0], gm, win, *conv_params,
        (w_out[0], w_gate[0], w_up[0], w_down[0]))
    x1p, hp = _mix_prompt(q, k, v, oc, x_prompt, fvec, wout, gff, *conv_params[2:])

    r3 = lambda a: a.reshape(Bs, T, ATTN_WIDTH)
    x1s, hs, nk, nv = _mix_sample(
        r3(qs), r3(kn), r3(vn),
        cache_k[0].reshape(Bs, W, ATTN_WIDTH), cache_v[0].reshape(Bs, W, ATTN_WIDTH),
        ocs, x_sample, fvec, wout, gff)

    flat = lambda a: a.reshape(-1, D)
    y_prompt, y_sample = _ffn(flat(x1p), flat(hp), flat(x1s), flat(hs), wg, wu, wd, gf)
    return (y_prompt.reshape(B, S, D), y_sample.reshape(Bs, T, D), heads(kf), heads(vf), ct[None],
            heads(nk), heads(nv), nc[None])
```

```python
import functools

import jax
import jax.numpy as jnp
from jax import lax
from jax.experimental import pallas as pl
from jax.experimental.pallas import tpu as pltpu

D_MODEL = 1024
CHUNK = 64
LEFT_CHUNKS = 8
W_BAND = LEFT_CHUNKS * CHUNK
ATTN_WIDTH = 512
N_HEADS = 8
HEAD_DIM = 64
CONV_WIDTH = 512
CONV_KERNEL = 31
CONV_PAST = CONV_KERNEL - 1
REL_CLIP = 128
D_FF = 2816
RMS_EPS = 1e-6
LN_EPS = 1e-5
NEG_INF = -1e30
SCALE = HEAD_DIM ** -0.5

LANES = 128
SUBLANES = 8
N_PAIRS = ATTN_WIDTH // LANES
TT_PROJ = 64
Z_STRIDE = TT_PROJ + SUBLANES
PROJ_COLS = 256
TQ = 512
N_KBLK = W_BAND // TQ + 1
SUB = 2 * CHUNK
WIN = W_BAND + SUB
F_LEN = 1024
LOOKAHEAD = 1
TM_FFN = 512
MXU_TILE = 256
FFN_COLS = 4 * MXU_TILE
CONV_ROWS = 64
CONV_STEPS = 16
CONV_LEAD = 32
VMEM_LIMIT = 56 * 1024 * 1024

f32 = jnp.float32
bf16 = jnp.bfloat16


def _rmsnorm(x, g):
    return (x * lax.rsqrt(jnp.mean(x * x, axis=-1, keepdims=True) + RMS_EPS)) * g


def _dot(a, b):
    return jnp.dot(a, b, preferred_element_type=f32)


def _dot_nt(a, b):
    return lax.dot_general(a, b, (((1,), (1,)), ((), ())), preferred_element_type=f32)


def _const_spec(shape):
    nd = len(shape)
    return pl.BlockSpec(shape, lambda *_: (0,) * nd, pipeline_mode=pl.Buffered(1))


def _project_stages(h_ref, w_ref, q_store, k_store, v_store):
    def issue(base, cs):
        return _dot(h_ref[...], w_ref[:, base + cs.start:base + cs.stop])

    def q_finish(cs, z):
        q_store(cs, (z * SCALE).astype(bf16))

    stages = []
    for base, finish in ((0, q_finish), (ATTN_WIDTH, k_store), (2 * ATTN_WIDTH, v_store)):
        for c0 in range(0, ATTN_WIDTH, PROJ_COLS):
            cs = slice(c0, c0 + PROJ_COLS)
            stages.append((functools.partial(issue, base, cs), functools.partial(finish, cs)))
    return stages


def _cast_weight(w32_ref, w_ref):
    for c0 in range(0, w_ref.shape[1], MXU_TILE):
        w_ref[:, c0:c0 + MXU_TILE] = w32_ref[:, c0:c0 + MXU_TILE].astype(bf16)


def _glu(h_ref, w_ref):
    base = 3 * ATTN_WIDTH
    a = _dot(h_ref[...], w_ref[:, base:base + CONV_WIDTH])
    g = _dot(h_ref[...], w_ref[:, base + CONV_WIDTH:base + 2 * CONV_WIDTH])
    return a * jax.nn.sigmoid(g)


def _conv_stages(cbuf_ref, sh_ref, y_ref, rows, wdw_ref, bdw_ref):
    lead = CONV_LEAD - CONV_PAST
    span = rows + CONV_LEAD - SUBLANES
    rb = min(CONV_ROWS, rows)

    def shift_stage(m):
        sh_ref[m - 1, 0:span, :] = cbuf_ref[m:m + span, :]

    def tap_stage(cs, r0):
        acc = jnp.zeros((rb, LANES), f32)
        for j in range(CONV_KERNEL):
            a, m = divmod(j + lead, SUBLANES)
            src = cbuf_ref if m == 0 else sh_ref.at[m - 1]
            lo = r0 + SUBLANES * a
            acc = acc + wdw_ref[j:j + 1, cs] * src[lo:lo + rb, cs]
        y_ref[r0:r0 + rb, cs] = acc + bdw_ref[:, cs]

    shifts = [functools.partial(shift_stage, m) for m in range(1, SUBLANES)]
    taps = [functools.partial(tap_stage, slice(c * LANES, (c + 1) * LANES), r0)
            for c in range(CONV_WIDTH // LANES) for r0 in range(0, rows, rb)]
    return shifts, taps


def _conv_time_major(ut_ref, y_ref, steps, wdw_ref, bdw_ref):
    for c in range(CONV_WIDTH // LANES):
        cs = slice(c * LANES, (c + 1) * LANES)
        for t0 in range(0, steps, CONV_STEPS):
            acc = jnp.zeros((CONV_STEPS, ut_ref.shape[2], LANES), f32)
            for j in range(CONV_KERNEL):
                acc = acc + wdw_ref[j:j + 1, cs] * ut_ref[c, t0 + j:t0 + j + CONV_STEPS]
            y_ref[c, t0:t0 + CONV_STEPS] = acc + bdw_ref[:, cs]


def _ln_swish_tiles(y_tiles, lng_ref, lnb_ref):
    n = len(y_tiles) * LANES
    mu = sum(jnp.sum(y, axis=-1, keepdims=True) for y in y_tiles) / n
    yc = [y - mu for y in y_tiles]
    var = sum(jnp.sum(y * y, axis=-1, keepdims=True) for y in yc) / n
    inv = lax.rsqrt(var + LN_EPS)
    out = []
    for c, y in enumerate(yc):
        cs = slice(c * LANES, (c + 1) * LANES)
        yn = y * inv * lng_ref[:, cs] + lnb_ref[:, cs]
        out.append(yn * jax.nn.sigmoid(yn))
    return out


def _ln_swish(y, lng_ref, lnb_ref):
    mu = jnp.mean(y, axis=-1, keepdims=True)
    yc = y - mu
    yn = yc * lax.rsqrt(jnp.mean(yc * yc, axis=-1, keepdims=True) + LN_EPS)
    yn = yn * lng_ref[...] + lnb_ref[...]
    return yn * jax.nn.sigmoid(yn)


def _pipeline(mxu_stages, valu_stages):
    pending = mxu_stages[0][0]()
    done = 0
    for n, (_, consume) in enumerate(mxu_stages):
        cur = pending
        if n + 1 < len(mxu_stages):
            pending = mxu_stages[n + 1][0]()
        upto = (n + 1) * len(valu_stages) // len(mxu_stages)
        for stage in valu_stages[done:upto]:
            stage()
        done = upto
        consume(cur)


def _inproj_kernel(x_ref, xs_ref, st_ref, g_ref, w32_ref, wdw_ref, bdw_ref, lng_ref, lnb_ref,
                   *refs, n_cast):
    cast_in, refs = refs[:n_cast], refs[n_cast:]
    prompt_out, sample_out = refs[:7], refs[7:12]
    cast_out, scratch = refs[12:12 + n_cast], refs[12 + n_cast:]
    w_ref, prompt_scratch, sample_scratch = scratch[0], scratch[1:5], scratch[5:]
    for src, dst in zip(cast_in, cast_out):
        dst[...] = src[...].astype(bf16)
    last = pl.num_programs(0) - 1

    @pl.when(pl.program_id(0) == 0)
    def _():
        _cast_weight(w32_ref, w_ref)
        ut_ref = prompt_scratch[2]
        ut_ref[:, 0:CONV_PAST] = jnp.zeros((ut_ref.shape[0], CONV_PAST) + ut_ref.shape[2:], f32)

    conv_refs = (wdw_ref, bdw_ref, lng_ref, lnb_ref)
    pl.when(pl.program_id(0) < last)(functools.partial(
        _inproj_prompt_tile, x_ref, g_ref, w_ref, conv_refs, prompt_out, prompt_scratch))
    pl.when(pl.program_id(0) == last)(functools.partial(
        _inproj_sample_tokens, xs_ref, st_ref, g_ref, w_ref, conv_refs, sample_out, sample_scratch))


def _inproj_prompt_tile(x_ref, g_ref, w_ref, conv_refs, out_refs, scratch):
    wdw_ref, bdw_ref, lng_ref, lnb_ref = conv_refs
    q_ref, k_ref, v_ref, oc_ref, kf_ref, vf_ref, ct_ref = out_refs
    h_ref, z_ref, ut_ref, y_ref = scratch
    nb = x_ref.shape[0]
    rows = nb * TT_PROJ
    h_ref[...] = _rmsnorm(x_ref[...].reshape(rows, D_MODEL), g_ref[...]).astype(bf16)

    def store(ref):
        def put(cs, val):
            ref[:, :, cs] = val.reshape(nb, TT_PROJ, val.shape[-1])
        return put

    def kv_store(ref, full_ref):
        def put(cs, val):
            val = val.reshape(nb, TT_PROJ, val.shape[-1])
            ref[:, :, cs] = val.astype(bf16)
            full_ref[:, :, cs] = val
        return put

    u = _glu(h_ref, w_ref)
    for issue, consume in _project_stages(h_ref, w_ref, store(q_ref),
                                          kv_store(k_ref, kf_ref), kv_store(v_ref, vf_ref)):
        consume(issue())
    lane_tiles = [slice(c * LANES, (c + 1) * LANES) for c in range(CONV_WIDTH // LANES)]
    for c, cs in enumerate(lane_tiles):
        for b in range(nb):
            z_ref[c, Z_STRIDE * b:Z_STRIDE * b + TT_PROJ, :] = u[b * TT_PROJ:(b + 1) * TT_PROJ, cs]
            ct_ref[b, :, cs] = z_ref[c, Z_STRIDE * b + TT_PROJ - CONV_PAST:Z_STRIDE * b + TT_PROJ, :]
        for t in range(TT_PROJ):
            ut_ref[c, CONV_PAST + t] = z_ref[c, pl.ds(t, nb, stride=Z_STRIDE), :]
    _conv_time_major(ut_ref, y_ref, TT_PROJ, wdw_ref, bdw_ref)
    for c, cs in enumerate(lane_tiles):
        for t in range(TT_PROJ):
            z_ref[c, pl.ds(t, nb, stride=Z_STRIDE), :] = y_ref[c, t]
        for b in range(nb):
            oc_ref[b, :, cs] = z_ref[c, Z_STRIDE * b:Z_STRIDE * b + TT_PROJ, :]

    ut_ref[:, 0:CONV_PAST] = ut_ref[:, TT_PROJ:TT_PROJ + CONV_PAST]


def _inproj(x, xs, st, g, w, wdw, bdw, lng, lnb, cast_weights):
    B, S, D = x.shape
    Ns, Bs = xs.shape[0], st.shape[0]
    T = Ns // Bs
    assert S % TT_PROJ == 0 and W_BAND % TT_PROJ == 0 and TT_PROJ >= CONV_PAST
    assert T >= CONV_PAST and T % SUBLANES == 0
    steps = S // TT_PROJ
    first_kept = (S - W_BAND) // TT_PROJ
    tile = lambda n: pl.BlockSpec((B, TT_PROJ, n), lambda i: (0, jnp.minimum(i, steps - 1), 0))
    kept = pl.BlockSpec((B, TT_PROJ, ATTN_WIDTH),
                        lambda i: (0, jnp.clip(i - first_kept, 0, steps - 1 - first_kept), 0))
    act = jax.ShapeDtypeStruct((B, S, ATTN_WIDTH), bf16)
    rows = B * TT_PROJ
    sample_shapes = ([jax.ShapeDtypeStruct((Ns, ATTN_WIDTH), bf16)]
                     + [jax.ShapeDtypeStruct((Ns, ATTN_WIDTH), f32)] * 2
                     + [jax.ShapeDtypeStruct((Bs, T, CONV_WIDTH), bf16),
                        jax.ShapeDtypeStruct((Bs, CONV_PAST, CONV_WIDTH), f32)])

    def cast_spec(a):
        blk = next(r for r in (32, 64, 128, 256) if a.shape[0] % r == 0 and a.shape[0] // r <= steps)
        last = a.shape[0] // blk - 1
        return pl.BlockSpec((blk, a.shape[1]), lambda i: (jnp.minimum(i, last), 0))

    cast_specs = [cast_spec(a) for a in cast_weights]
    return pl.pallas_call(
        functools.partial(_inproj_kernel, n_cast=len(cast_weights)),
        grid=(steps + 1,),
        in_specs=[tile(D)] + [_const_spec(a.shape) for a in (xs, st, g, w, wdw, bdw, lng, lnb)]
                 + cast_specs,
        out_specs=[tile(ATTN_WIDTH)] * 4 + [kept] * 2 + [_const_spec((B, CONV_PAST, CONV_WIDTH))]
                  + [_const_spec(a.shape) for a in sample_shapes] + cast_specs,
        out_shape=[act] * 3 + [jax.ShapeDtypeStruct((B, S, CONV_WIDTH), f32)]
                  + [jax.ShapeDtypeStruct((B, W_BAND, ATTN_WIDTH), f32)] * 2
                  + [jax.ShapeDtypeStruct((B, CONV_PAST, CONV_WIDTH), f32)] + sample_shapes
                  + [jax.ShapeDtypeStruct(a.shape, bf16) for a in cast_weights],
        scratch_shapes=[pltpu.VMEM(w.shape, bf16),
                        pltpu.VMEM((rows, D), bf16),
                        pltpu.VMEM((CONV_WIDTH // LANES, B * Z_STRIDE, LANES), f32),
                        pltpu.VMEM((CONV_WIDTH // LANES, CONV_PAST + TT_PROJ, B, LANES), f32),
                        pltpu.VMEM((CONV_WIDTH // LANES, TT_PROJ, B, LANES), f32),
                        pltpu.VMEM((Ns, D), bf16),
                        pltpu.VMEM((CONV_LEAD + T, CONV_WIDTH), f32),
                        pltpu.VMEM((SUBLANES - 1, CONV_LEAD + T - SUBLANES, CONV_WIDTH), f32),
                        pltpu.VMEM((T, CONV_WIDTH), f32)],
        compiler_params=pltpu.CompilerParams(
            dimension_semantics=("arbitrary",), vmem_limit_bytes=VMEM_LIMIT),
        name="inproj",
    )(x, xs, st, g, w, wdw, bdw, lng, lnb, *cast_weights)


def _inproj_sample_tokens(x_ref, st_ref, g_ref, w_ref, conv_refs, out_refs, scratch):
    wdw_ref, bdw_ref, lng_ref, lnb_ref = conv_refs
    q_ref, k_ref, v_ref, oc_ref, nc_ref = out_refs
    h_ref, cbuf_ref, sh_ref, y_ref = scratch
    B, T, _ = oc_ref.shape
    h_ref[...] = _rmsnorm(x_ref[...], g_ref[...]).astype(bf16)
    u = _glu(h_ref, w_ref)

    def store(ref):
        def put(cs, val):
            ref[:, cs] = val
        return put

    proj = _project_stages(h_ref, w_ref, store(q_ref), store(k_ref), store(v_ref))
    lead = CONV_LEAD - CONV_PAST

    def conv_stage(b):
        cbuf_ref[0:SUBLANES, :] = jnp.zeros((SUBLANES, CONV_WIDTH), f32)
        cbuf_ref[lead:CONV_LEAD, :] = st_ref[b]
        cbuf_ref[CONV_LEAD:CONV_LEAD + T, :] = u[b * T:(b + 1) * T]
        nc_ref[b] = cbuf_ref[CONV_LEAD + T - CONV_PAST:CONV_LEAD + T, :]
        shifts, taps = _conv_stages(cbuf_ref, sh_ref, y_ref, T, wdw_ref, bdw_ref)
        for stage in shifts + taps:
            stage()
        oc_ref[b] = _ln_swish(y_ref[...], lng_ref, lnb_ref).astype(bf16)

    _pipeline(proj, [functools.partial(conv_stage, b) for b in range(B)])


def _toeplitz(frow, rows):
    return pltpu.roll(jnp.broadcast_to(frow, (rows, F_LEN)), 0, 1, stride=1, stride_axis=0)


def _softmax_parts(s_parts):
    m = functools.reduce(jnp.maximum, [jnp.max(s, axis=-1, keepdims=True) for s in s_parts])
    e_parts = [jnp.exp(s - m) for s in s_parts]
    l = functools.reduce(jnp.add, [jnp.sum(e, axis=-1, keepdims=True) for e in e_parts])
    return e_parts, l


def _stack_heads(q):
    lane = lax.broadcasted_iota(jnp.int32, q.shape, 1)
    qf = q.astype(f32)
    even = jnp.where(lane < HEAD_DIM, qf, 0.0)
    odd = jnp.where(lane < HEAD_DIM, 0.0, qf)
    return jnp.concatenate([even, odd], axis=0).astype(bf16)


def _residual_out(x_ref, cat_ref, oc, wout_ref, gn_ref, o_ref, hn_ref):
    x1 = (x_ref[0] + _dot(cat_ref[...], wout_ref[0:ATTN_WIDTH, :])
          + _dot(oc, wout_ref[ATTN_WIDTH:, :]))
    o_ref[0] = x1
    hn_ref[0] = _rmsnorm(x1, gn_ref[...]).astype(bf16)


def _mix_prompt_kernel(q_ref, *refs):
    k_refs, v_refs = refs[:N_KBLK], refs[N_KBLK:2 * N_KBLK]
    (oc_ref, x_ref, f_ref, wout_ref, gn_ref, lng_ref, lnb_ref,
     o_ref, hn_ref, bias_ref, cat_ref) = refs[2 * N_KBLK:]
    i = pl.program_id(1)

    @pl.when((pl.program_id(0) == 0) & (i == 0))
    def _():
        for h in range(N_HEADS):
            p, par = divmod(h, 2)
            bias_ref[p, par * SUB:(par + 1) * SUB, :] = _toeplitz(f_ref[h:h + 1, :], SUB)[:, :WIN]

    row = lax.broadcasted_iota(jnp.int32, (2 * SUB, LANES), 0)
    lane = lax.broadcasted_iota(jnp.int32, (2 * SUB, LANES), 1)
    odd_chunk = (row & CHUNK) != 0
    edge_ok = {0: jnp.logical_not(odd_chunk & (lane < CHUNK)),
               WIN // LANES - 1: odd_chunk | (lane < CHUNK)}
    blk_ok = [i >= N_KBLK - 1 - b for b in range(N_KBLK - 1)]
    even_head = lax.broadcasted_iota(jnp.int32, (SUB, LANES), 1) < HEAD_DIM

    def pieces(r):
        out = []
        for b in range(N_KBLK):
            lo, hi = max(SUB * r, TQ * b), min(SUB * r + WIN, TQ * (b + 1))
            if lo < hi:
                out.append((b, lo - TQ * b, hi - TQ * b))
        return out

    def scores(r, p):
        rs, ps = slice(r * SUB, (r + 1) * SUB), slice(p * LANES, (p + 1) * LANES)
        q2 = _stack_heads(q_ref[0, rs, ps])
        return jnp.concatenate(
            [_dot_nt(q2, k_refs[b][0, a:z, ps]) for b, a, z in pieces(r)], axis=1)

    def finish(r, p, s):
        rs, ps = slice(r * SUB, (r + 1) * SUB), slice(p * LANES, (p + 1) * LANES)
        s = s + bias_ref[p]
        tiles = []
        for t in range(WIN // LANES):
            st = s[:, t * LANES:(t + 1) * LANES]
            b = (SUB * r + LANES * t) // TQ
            ok = edge_ok.get(t)
            if b < N_KBLK - 1:
                ok = blk_ok[b] if ok is None else ok & blk_ok[b]
            tiles.append(st if ok is None else jnp.where(ok, st, NEG_INF))
        (e,), l = _softmax_parts([jnp.concatenate(tiles, axis=1)])
        pb = e.astype(bf16)
        o, c0 = None, 0
        for b, a, z in pieces(r):
            ob = _dot(pb[:, c0:c0 + z - a], v_refs[b][0, a:z, ps])
            o = ob if o is None else o + ob
            c0 += z - a
        o = o / l
        cat_ref[rs, ps] = jnp.where(even_head, o[:SUB], o[SUB:]).astype(bf16)

    units = [(r, p) for r in range(TQ // SUB) for p in range(N_PAIRS)]
    pending = [scores(*u) for u in units[:LOOKAHEAD]]
    for n, unit in enumerate(units):
        if n + LOOKAHEAD < len(units):
            pending.append(scores(*units[n + LOOKAHEAD]))
        finish(*unit, pending.pop(0))

    oc = _ln_swish(oc_ref[0], lng_ref, lnb_ref).astype(bf16)
    _residual_out(x_ref, cat_ref, oc, wout_ref, gn_ref, o_ref, hn_ref)


def _mix_prompt(q, k, v, oc, x, fvec, wout, gn, lng, lnb):
    B, S, D = x.shape
    assert S % TQ == 0 and W_BAND % TQ == 0 and TQ % SUB == 0 and WIN % LANES == 0
    qspec = pl.BlockSpec((1, TQ, ATTN_WIDTH), lambda b, i: (b, i, 0))
    kspec = lambda back: pl.BlockSpec(
        (1, TQ, ATTN_WIDTH), lambda b, i: (b, jnp.maximum(i - back, 0), 0))
    kspecs = [kspec(N_KBLK - 1 - n) for n in range(N_KBLK)]
    xspec = pl.BlockSpec((1, TQ, D), lambda b, i: (b, i, 0))
    return pl.pallas_call(
        _mix_prompt_kernel,
        grid=(B, S // TQ),
        in_specs=[qspec] + kspecs + kspecs + [qspec, xspec,
                  _const_spec(fvec.shape), _const_spec(wout.shape), _const_spec(gn.shape),
                  _const_spec(lng.shape), _const_spec(lnb.shape)],
        out_specs=[xspec, xspec],
        out_shape=[jax.ShapeDtypeStruct((B, S, D), f32), jax.ShapeDtypeStruct((B, S, D), bf16)],
        scratch_shapes=[pltpu.VMEM((N_PAIRS, 2 * SUB, WIN), f32),
                        pltpu.VMEM((TQ, ATTN_WIDTH), bf16)],
        compiler_params=pltpu.CompilerParams(
            dimension_semantics=("arbitrary", "arbitrary"), vmem_limit_bytes=VMEM_LIMIT),
        name="mix_prompt",
    )(q, *([k] * N_KBLK), *([v] * N_KBLK), oc, x, fvec, wout, gn, lng, lnb)


def _mix_sample_kernel(q_ref, kn_ref, vn_ref, ck_ref, cv_ref, oc_ref, x_ref,
                       f_ref, wout_ref, gn_ref, o_ref, hn_ref, nk_ref, nv_ref, bias_ref, cat_ref):
    T = oc_ref.shape[1]
    W = ck_ref.shape[1]

    @pl.when(pl.program_id(0) == 0)
    def _():
        for h in range(N_HEADS):
            p, par = divmod(h, 2)
            bias_ref[p, par * T:(par + 1) * T, :] = _toeplitz(f_ref[h:h + 1, :], T)

    nk_ref[0, 0:W - T, :] = ck_ref[0, T:W, :]
    nk_ref[0, W - T:W, :] = kn_ref[0]
    nv_ref[0, 0:W - T, :] = cv_ref[0, T:W, :]
    nv_ref[0, W - T:W, :] = vn_ref[0]

    lane = lax.broadcasted_iota(jnp.int32, (T, LANES), 1)
    even_head = lane < HEAD_DIM
    for p in range(N_PAIRS):
        ps = slice(p * LANES, (p + 1) * LANES)
        kc = ck_ref[0, :, ps].astype(bf16)
        vc = cv_ref[0, :, ps].astype(bf16)
        kn = kn_ref[0, :, ps].astype(bf16)
        vn = vn_ref[0, :, ps].astype(bf16)
        q2 = _stack_heads(q_ref[0, :, ps])
        (ec, en), l = _softmax_parts([_dot_nt(q2, kc) + bias_ref[p, :, 0:W],
                                      _dot_nt(q2, kn) + bias_ref[p, :, W:W + T]])
        o = (_dot(ec.astype(bf16), vc) + _dot(en.astype(bf16), vn)) / l
        cat_ref[:, ps] = jnp.where(even_head, o[:T], o[T:]).astype(bf16)

    _residual_out(x_ref, cat_ref, oc_ref[0], wout_ref, gn_ref, o_ref, hn_ref)


def _mix_sample(q, kn, vn, ck, cv, oc, x, fvec, wout, gn):
    B, T, D = x.shape
    W = ck.shape[1]
    assert W == W_BAND and W + T <= F_LEN - T
    row = lambda r, n: pl.BlockSpec((1, r, n), lambda b: (b, 0, 0))
    return pl.pallas_call(
        _mix_sample_kernel,
        grid=(B,),
        in_specs=[row(T, ATTN_WIDTH)] * 3 + [row(W, ATTN_WIDTH)] * 2
                 + [row(T, CONV_WIDTH), row(T, D)]
                 + [_const_spec(a.shape) for a in (fvec, wout, gn)],
        out_specs=[row(T, D), row(T, D), row(W, ATTN_WIDTH), row(W, ATTN_WIDTH)],
        out_shape=[jax.ShapeDtypeStruct((B, T, D), f32),
                   jax.ShapeDtypeStruct((B, T, D), bf16),
                   jax.ShapeDtypeStruct((B, W, ATTN_WIDTH), f32),
                   jax.ShapeDtypeStruct((B, W, ATTN_WIDTH), f32)],
        scratch_shapes=[pltpu.VMEM((N_PAIRS, 2 * T, F_LEN), f32),
                        pltpu.VMEM((T, ATTN_WIDTH), bf16)],
        compiler_params=pltpu.CompilerParams(
            dimension_semantics=("arbitrary",), vmem_limit_bytes=VMEM_LIMIT),
        name="mix_sample",
    )(q, kn, vn, ck, cv, oc, x, fvec, wout, gn)


def _ffn_block(x, h, wg_ref, wu_ref, wd_ref, gf_ref):
    acc = x
    for c0 in range(0, D_FF, FFN_COLS):
        cs = slice(c0, min(c0 + FFN_COLS, D_FF))
        a = jax.nn.silu(_dot(h, wg_ref[:, cs])) * _dot(h, wu_ref[:, cs])
        acc = acc + _dot(a.astype(bf16), wd_ref[cs, :])
    return _rmsnorm(acc, gf_ref[...])


def _ffn_kernel(x_ref, h_ref, xs_ref, hs_ref, wg_ref, wu_ref, wd_ref, gf_ref, o_ref, os_ref):
    last = pl.num_programs(0) - 1

    @pl.when(pl.program_id(0) < last)
    def _():
        half = x_ref.shape[0] // 2
        for r in range(2):
            rs = slice(r * half, (r + 1) * half)
            o_ref[rs, :] = _ffn_block(x_ref[rs, :], h_ref[rs, :], wg_ref, wu_ref, wd_ref, gf_ref)

    @pl.when(pl.program_id(0) == last)
    def _():
        os_ref[...] = _ffn_block(xs_ref[...], hs_ref[...], wg_ref, wu_ref, wd_ref, gf_ref)


def _ffn(x, h, xs, hs, wg, wu, wd, gf):
    N, D = x.shape
    assert N % TM_FFN == 0
    n = N // TM_FFN
    tile = pl.BlockSpec((TM_FFN, D), lambda i: (jnp.minimum(i, n - 1), 0))
    return pl.pallas_call(
        _ffn_kernel,
        grid=(n + 1,),
        in_specs=[tile, tile] + [_const_spec(a.shape) for a in (xs, hs, wg, wu, wd, gf)],
        out_specs=[tile, _const_spec(xs.shape)],
        out_shape=[jax.ShapeDtypeStruct((N, D), f32), jax.ShapeDtypeStruct(xs.shape, f32)],
        compiler_params=pltpu.CompilerParams(
            dimension_semantics=("arbitrary",), vmem_limit_bytes=VMEM_LIMIT),
        name="ffn",
    )(x, h, xs, hs, wg, wu, wd, gf)


def _rel_bias_row(table):
    n_rel = table.shape[1]
    far = W_BAND - REL_CLIP
    assert n_rel == 2 * REL_CLIP + 1 and far + n_rel + SUB <= F_LEN
    rep = lambda col, n: jnp.broadcast_to(table[:, col:col + 1], (table.shape[0], n))
    return jnp.concatenate([rep(n_rel - 1, far), table[:, ::-1],
                            rep(0, F_LEN - SUB - far - n_rel), rep(n_rel - 1, SUB)], axis=1)


def kernel(x_prompt, x_sample, cache_k, cache_v, state_conv, g_mix, w_in, rel_table, w_dw, b_dw,
           ln_g, ln_b, w_out, g_ffn, w_gate, w_up, w_down, g_final):
    assert g_mix.shape[0] == 1, "single-layer trunk: the final RMSNorm is fused into the FFN kernel"
    B, S, D = x_prompt.shape
    Bs, T, _ = x_sample.shape
    W = cache_k.shape[2]
    gm, gff, gf = g_mix[0].reshape(1, D), g_ffn[0].reshape(1, D), g_final.reshape(1, D)
    win = w_in[0]
    conv_params = (w_dw[0],) + tuple(a[0].reshape(1, CONV_WIDTH) for a in (b_dw, ln_g, ln_b))
    heads = lambda a: a.reshape(1, a.shape[0], a.shape[1], N_HEADS, HEAD_DIM)
    fvec = _rel_bias_row(rel_table[0])

    (q, k, v, oc, kf, vf, ct, qs, kn, vn, ocs, nc, wout, wg, wu, wd) = _inproj(
        x_prompt, x_sample.reshape(Bs * T, D), state_conv[0], gm, win, *conv_params,
        (w_out[0], w_gate[0], w_up[0], w_down[0]))
    x1p, hp = _mix_prompt(q, k, v, oc, x_prompt, fvec, wout, gff, *conv_params[2:])

    r3 = lambda a: a.reshape(Bs, T, ATTN_WIDTH)
    x1s, hs, nk, nv = _mix_sample(
        r3(qs), r3(kn), r3(vn),
        cache_k[0].reshape(Bs, W, ATTN_WIDTH), cache_v[0].reshape(Bs, W, ATTN_WIDTH),
        ocs, x_sample, fvec, wout, gff)

    flat = lambda a: a.reshape(-1, D)
    y_prompt, y_sample = _ffn(flat(x1p), flat(hp), flat(x1s), flat(hs), wg, wu, wd, gf)
    return (y_prompt.reshape(B, S, D), y_sample.reshape(Bs, T, D), heads(kf), heads(vf), ct[None],
            heads(nk), heads(nv), nc[None])
```

```python
import functools

import jax
import jax.numpy as jnp
from jax import lax
from jax.experimental import pallas as pl
from jax.experimental.pallas import tpu as pltpu

D_MODEL = 1024
CHUNK = 64
LEFT_CHUNKS = 8
W_BAND = LEFT_CHUNKS * CHUNK
ATTN_WIDTH = 512
N_HEADS = 8
HEAD_DIM = 64
CONV_WIDTH = 512
CONV_KERNEL = 31
CONV_PAST = CONV_KERNEL - 1
REL_CLIP = 128
D_FF = 2816
RMS_EPS = 1e-6
LN_EPS = 1e-5
NEG_INF = -1e30
SCALE = HEAD_DIM ** -0.5

LANES = 128
SUBLANES = 8
N_PAIRS = ATTN_WIDTH // LANES
TT_PROJ = 64
Z_STRIDE = TT_PROJ + SUBLANES
PROJ_COLS = 256
TQ = 512
N_KBLK = W_BAND // TQ + 1
SUB = 2 * CHUNK
WIN = W_BAND + SUB
F_LEN = 1024
LOOKAHEAD = 1
TM_FFN = 512
MXU_TILE = 256
FFN_COLS = 4 * MXU_TILE
CONV_ROWS = 64
CONV_STEPS = 16
CONV_LEAD = 32
VMEM_LIMIT = 56 * 1024 * 1024

f32 = jnp.float32
bf16 = jnp.bfloat16


def _rmsnorm(x, g):
    return (x * lax.rsqrt(jnp.mean(x * x, axis=-1, keepdims=True) + RMS_EPS)) * g


def _dot(a, b):
    return jnp.dot(a, b, preferred_element_type=f32)


def _dot_nt(a, b):
    return lax.dot_general(a, b, (((1,), (1,)), ((), ())), preferred_element_type=f32)


def _const_spec(shape):
    nd = len(shape)
    return pl.BlockSpec(shape, lambda *_: (0,) * nd, pipeline_mode=pl.Buffered(1))


def _project_stages(h_ref, w_ref, q_store, k_store, v_store):
    def issue(base, cs):
        return _dot(h_ref[...], w_ref[:, base + cs.start:base + cs.stop])

    def q_finish(cs, z):
        q_store(cs, (z * SCALE).astype(bf16))

    stages = []
    for base, finish in ((0, q_finish), (ATTN_WIDTH, k_store), (2 * ATTN_WIDTH, v_store)):
        for c0 in range(0, ATTN_WIDTH, PROJ_COLS):
            cs = slice(c0, c0 + PROJ_COLS)
            stages.append((functools.partial(issue, base, cs), functools.partial(finish, cs)))
    return stages


def _cast_weight(w32_ref, w_ref):
    for c0 in range(0, w_ref.shape[1], MXU_TILE):
        w_ref[:, c0:c0 + MXU_TILE] = w32_ref[:, c0:c0 + MXU_TILE].astype(bf16)


def _glu(h_ref, w_ref):
    base = 3 * ATTN_WIDTH
    a = _dot(h_ref[...], w_ref[:, base:base + CONV_WIDTH])
    g = _dot(h_ref[...], w_ref[:, base + CONV_WIDTH:base + 2 * CONV_WIDTH])
    return a * jax.nn.sigmoid(g)


def _conv_stages(cbuf_ref, sh_ref, y_ref, rows, wdw_ref, bdw_ref):
    lead = CONV_LEAD - CONV_PAST
    span = rows + CONV_LEAD - SUBLANES
    rb = min(CONV_ROWS, rows)

    def shift_stage(m):
        sh_ref[m - 1, 0:span, :] = cbuf_ref[m:m + span, :]

    def tap_stage(cs, r0):
        acc = jnp.zeros((rb, LANES), f32)
        for j in range(CONV_KERNEL):
            a, m = divmod(j + lead, SUBLANES)
            src = cbuf_ref if m == 0 else sh_ref.at[m - 1]
            lo = r0 + SUBLANES * a
            acc = acc + wdw_ref[j:j + 1, cs] * src[lo:lo + rb, cs]
        y_ref[r0:r0 + rb, cs] = acc + bdw_ref[:, cs]

    shifts = [functools.partial(shift_stage, m) for m in range(1, SUBLANES)]
    taps = [functools.partial(tap_stage, slice(c * LANES, (c + 1) * LANES), r0)
            for c in range(CONV_WIDTH // LANES) for r0 in range(0, rows, rb)]
    return shifts, taps


def _conv_time_major(ut_ref, y_ref, steps, wdw_ref, bdw_ref):
    for c in range(CONV_WIDTH // LANES):
        cs = slice(c * LANES, (c + 1) * LANES)
        for t0 in range(0, steps, CONV_STEPS):
            acc = jnp.zeros((CONV_STEPS, ut_ref.shape[2], LANES), f32)
            for j in range(CONV_KERNEL):
                acc = acc + wdw_ref[j:j + 1, cs] * ut_ref[c, t0 + j:t0 + j + CONV_STEPS]
            y_ref[c, t0:t0 + CONV_STEPS] = acc + bdw_ref[:, cs]


def _ln_swish_tiles(y_tiles, lng_ref, lnb_ref):
    n = len(y_tiles) * LANES
    mu = sum(jnp.sum(y, axis=-1, keepdims=True) for y in y_tiles) / n
    yc = [y - mu for y in y_tiles]
    var = sum(jnp.sum(y * y, axis=-1, keepdims=True) for y in yc) / n
    inv = lax.rsqrt(var + LN_EPS)
    out = []
    for c, y in enumerate(yc):
        cs = slice(c * LANES, (c + 1) * LANES)
        yn = y * inv * lng_ref[:, cs] + lnb_ref[:, cs]
        out.append(yn * jax.nn.sigmoid(yn))
    return out


def _ln_swish(y, lng_ref, lnb_ref):
    mu = jnp.mean(y, axis=-1, keepdims=True)
    yc = y - mu
    yn = yc * lax.rsqrt(jnp.mean(yc * yc, axis=-1, keepdims=True) + LN_EPS)
    yn = yn * lng_ref[...] + lnb_ref[...]
    return yn * jax.nn.sigmoid(yn)


def _pipeline(mxu_stages, valu_stages):
    pending = mxu_stages[0][0]()
    done = 0
    for n, (_, consume) in enumerate(mxu_stages):
        cur = pending
        if n + 1 < len(mxu_stages):
            pending = mxu_stages[n + 1][0]()
        upto = (n + 1) * len(valu_stages) // len(mxu_stages)
        for stage in valu_stages[done:upto]:
            stage()
        done = upto
        consume(cur)


def _inproj_kernel(x_ref, xs_ref, st_ref, g_ref, w32_ref, wdw_ref, bdw_ref, lng_ref, lnb_ref,
                   *refs, n_cast):
    cast_in, refs = refs[:n_cast], refs[n_cast:]
    prompt_out, sample_out = refs[:7], refs[7:12]
    cast_out, scratch = refs[12:12 + n_cast], refs[12 + n_cast:]
    w_ref, prompt_scratch, sample_scratch = scratch[0], scratch[1:5], scratch[5:]
    for src, dst in zip(cast_in, cast_out):
        dst[...] = src[...].astype(bf16)
    last = pl.num_programs(0) - 1

    @pl.when(pl.program_id(0) == 0)
    def _():
        _cast_weight(w32_ref, w_ref)
        ut_ref = prompt_scratch[2]
        ut_ref[:, 0:CONV_PAST] = jnp.zeros((ut_ref.shape[0], CONV_PAST) + ut_ref.shape[2:], f32)

    conv_refs = (wdw_ref, bdw_ref, lng_ref, lnb_ref)
    pl.when(pl.program_id(0) < last)(functools.partial(
        _inproj_prompt_tile, x_ref, g_ref, w_ref, conv_refs, prompt_out, prompt_scratch))
    pl.when(pl.program_id(0) == last)(functools.partial(
        _inproj_sample_tokens, xs_ref, st_ref, g_ref, w_ref, conv_refs, sample_out, sample_scratch))


def _inproj_prompt_tile(x_ref, g_ref, w_ref, conv_refs, out_refs, scratch):
    wdw_ref, bdw_ref, lng_ref, lnb_ref = conv_refs
    q_ref, k_ref, v_ref, oc_ref, kf_ref, vf_ref, ct_ref = out_refs
    h_ref, z_ref, ut_ref, y_ref = scratch
    nb = x_ref.shape[0]
    rows = nb * TT_PROJ
    h_ref[...] = _rmsnorm(x_ref[...].reshape(rows, D_MODEL), g_ref[...]).astype(bf16)

    def store(ref):
        def put(cs, val):
            ref[:, :, cs] = val.reshape(nb, TT_PROJ, val.shape[-1])
        return put

    def kv_store(ref, full_ref):
        def put(cs, val):
            val = val.reshape(nb, TT_PROJ, val.shape[-1])
            ref[:, :, cs] = val.astype(bf16)
            full_ref[:, :, cs] = val
        return put

    u = _glu(h_ref, w_ref)
    for issue, consume in _project_stages(h_ref, w_ref, store(q_ref),
                                          kv_store(k_ref, kf_ref), kv_store(v_ref, vf_ref)):
        consume(issue())
    lane_tiles = [slice(c * LANES, (c + 1) * LANES) for c in range(CONV_WIDTH // LANES)]
    for c, cs in enumerate(lane_tiles):
        for b in range(nb):
            z_ref[c, Z_STRIDE * b:Z_STRIDE * b + TT_PROJ, :] = u[b * TT_PROJ:(b + 1) * TT_PROJ, cs]
            ct_ref[b, :, cs] = z_ref[c, Z_STRIDE * b + TT_PROJ - CONV_PAST:Z_STRIDE * b + TT_PROJ, :]
        for t in range(TT_PROJ):
            ut_ref[c, CONV_PAST + t] = z_ref[c, pl.ds(t, nb, stride=Z_STRIDE), :]
    _conv_time_major(ut_ref, y_ref, TT_PROJ, wdw_ref, bdw_ref)
    for c, cs in enumerate(lane_tiles):
        for t in range(TT_PROJ):
            z_ref[c, pl.ds(t, nb, stride=Z_STRIDE), :] = y_ref[c, t]
        for b in range(nb):
            oc_ref[b, :, cs] = z_ref[c, Z_STRIDE * b:Z_STRIDE * b + TT_PROJ, :]

    ut_ref[:, 0:CONV_PAST] = ut_ref[:, TT_PROJ:TT_PROJ + CONV_PAST]


def _inproj(x, xs, st, g, w, wdw, bdw, lng, lnb, cast_weights):
    B, S, D = x.shape
    Ns, Bs = xs.shape[0], st.shape[0]
    T = Ns // Bs
    assert S % TT_PROJ == 0 and W_BAND % TT_PROJ == 0 and TT_PROJ >= CONV_PAST
    assert T >= CONV_PAST and T % SUBLANES == 0
    steps = S // TT_PROJ
    first_kept = (S - W_BAND) // TT_PROJ
    tile = lambda n: pl.BlockSpec((B, TT_PROJ, n), lambda i: (0, jnp.minimum(i, steps - 1), 0))
    kept = pl.BlockSpec((B, TT_PROJ, ATTN_WIDTH),
                        lambda i: (0, jnp.clip(i - first_kept, 0, steps - 1 - first_kept), 0))
    act = jax.ShapeDtypeStruct((B, S, ATTN_WIDTH), bf16)
    rows = B * TT_PROJ
    sample_shapes = ([jax.ShapeDtypeStruct((Ns, ATTN_WIDTH), bf16)]
                     + [jax.ShapeDtypeStruct((Ns, ATTN_WIDTH), f32)] * 2
                     + [jax.ShapeDtypeStruct((Bs, T, CONV_WIDTH), bf16),
                        jax.ShapeDtypeStruct((Bs, CONV_PAST, CONV_WIDTH), f32)])

    def cast_spec(a):
        blk = next(r for r in (32, 64, 128, 256) if a.shape[0] % r == 0 and a.shape[0] // r <= steps)
        last = a.shape[0] // blk - 1
        return pl.BlockSpec((blk, a.shape[1]), lambda i: (jnp.minimum(i, last), 0))

    cast_specs = [cast_spec(a) for a in cast_weights]
    return pl.pallas_call(
        functools.partial(_inproj_kernel, n_cast=len(cast_weights)),
        grid=(steps + 1,),
        in_specs=[tile(D)] + [_const_spec(a.shape) for a in (xs, st, g, w, wdw, bdw, lng, lnb)]
                 + cast_specs,
        out_specs=[tile(ATTN_WIDTH)] * 4 + [kept] * 2 + [_const_spec((B, CONV_PAST, CONV_WIDTH))]
                  + [_const_spec(a.shape) for a in sample_shapes] + cast_specs,
        out_shape=[act] * 3 + [jax.ShapeDtypeStruct((B, S, CONV_WIDTH), f32)]
                  + [jax.ShapeDtypeStruct((B, W_BAND, ATTN_WIDTH), f32)] * 2
                  + [jax.ShapeDtypeStruct((B, CONV_PAST, CONV_WIDTH), f32)] + sample_shapes
                  + [jax.ShapeDtypeStruct(a.shape, bf16) for a in cast_weights],
        scratch_shapes=[pltpu.VMEM(w.shape, bf16),
                        pltpu.VMEM((rows, D), bf16),
                        pltpu.VMEM((CONV_WIDTH // LANES, B * Z_STRIDE, LANES), f32),
                        pltpu.VMEM((CONV_WIDTH // LANES, CONV_PAST + TT_PROJ, B, LANES), f32),
                        pltpu.VMEM((CONV_WIDTH // LANES, TT_PROJ, B, LANES), f32),
                        pltpu.VMEM((Ns, D), bf16),
                        pltpu.VMEM((CONV_LEAD + T, CONV_WIDTH), f32),
                        pltpu.VMEM((SUBLANES - 1, CONV_LEAD + T - SUBLANES, CONV_WIDTH), f32),
                        pltpu.VMEM((T, CONV_WIDTH), f32)],
        compiler_params=pltpu.CompilerParams(
            dimension_semantics=("arbitrary",), vmem_limit_bytes=VMEM_LIMIT),
        name="inproj",
    )(x, xs, st, g, w, wdw, bdw, lng, lnb, *cast_weights)


def _inproj_sample_tokens(x_ref, st_ref, g_ref, w_ref, conv_refs, out_refs, scratch):
    wdw_ref, bdw_ref, lng_ref, lnb_ref = conv_refs
    q_ref, k_ref, v_ref, oc_ref, nc_ref = out_refs
    h_ref, cbuf_ref, sh_ref, y_ref = scratch
    B, T, _ = oc_ref.shape
    h_ref[...] = _rmsnorm(x_ref[...], g_ref[...]).astype(bf16)
    u = _glu(h_ref, w_ref)

    def store(ref):
        def put(cs, val):
            ref[:, cs] = val
        return put

    proj = _project_stages(h_ref, w_ref, store(q_ref), store(k_ref), store(v_ref))
    lead = CONV_LEAD - CONV_PAST

    def conv_stage(b):
        cbuf_ref[0:SUBLANES, :] = jnp.zeros((SUBLANES, CONV_WIDTH), f32)
        cbuf_ref[lead:CONV_LEAD, :] = st_ref[b]
        cbuf_ref[CONV_LEAD:CONV_LEAD + T, :] = u[b * T:(b + 1) * T]
        nc_ref[b] = cbuf_ref[CONV_LEAD + T - CONV_PAST:CONV_LEAD + T, :]
        shifts, taps = _conv_stages(cbuf_ref, sh_ref, y_ref, T, wdw_ref, bdw_ref)
        for stage in shifts + taps:
            stage()
        oc_ref[b] = _ln_swish(y_ref[...], lng_ref, lnb_ref).astype(bf16)

    _pipeline(proj, [functools.partial(conv_stage, b) for b in range(B)])


def _toeplitz(frow, rows):
    return pltpu.roll(jnp.broadcast_to(frow, (rows, F_LEN)), 0, 1, stride=1, stride_axis=0)


def _softmax_parts(s_parts):
    m = functools.reduce(jnp.maximum, [jnp.max(s, axis=-1, keepdims=True) for s in s_parts])
    e_parts = [jnp.exp(s - m) for s in s_parts]
    l = functools.reduce(jnp.add, [jnp.sum(e, axis=-1, keepdims=True) for e in e_parts])
    return e_parts, l


def _stack_heads(q):
    lane = lax.broadcasted_iota(jnp.int32, q.shape, 1)
    qf = q.astype(f32)
    even = jnp.where(lane < HEAD_DIM, qf, 0.0)
    odd = jnp.where(lane < HEAD_DIM, 0.0, qf)
    return jnp.concatenate([even, odd], axis=0).astype(bf16)


def _residual_out(x_ref, cat_ref, oc, wout_ref, gn_ref, o_ref, hn_ref):
    x1 = (x_ref[0] + _dot(cat_ref[...], wout_ref[0:ATTN_WIDTH, :])
          + _dot(oc, wout_ref[ATTN_WIDTH:, :]))
    o_ref[0] = x1
    hn_ref[0] = _rmsnorm(x1, gn_ref[...]).astype(bf16)


def _mix_prompt_kernel(q_ref, *refs):
    k_refs, v_refs = refs[:N_KBLK], refs[N_KBLK:2 * N_KBLK]
    (oc_ref, x_ref, f_ref, wout_ref, gn_ref, lng_ref, lnb_ref,
     o_ref, hn_ref, bias_ref, cat_ref) = refs[2 * N_KBLK:]
    i = pl.program_id(1)

    @pl.when((pl.program_id(0) == 0) & (i == 0))
    def _():
        for h in range(N_HEADS):
            p, par = divmod(h, 2)
            bias_ref[p, par * SUB:(par + 1) * SUB, :] = _toeplitz(f_ref[h:h + 1, :], SUB)[:, :WIN]

    row = lax.broadcasted_iota(jnp.int32, (2 * SUB, LANES), 0)
    lane = lax.broadcasted_iota(jnp.int32, (2 * SUB, LANES), 1)
    odd_chunk = (row & CHUNK) != 0
    edge_ok = {0: jnp.logical_not(odd_chunk & (lane < CHUNK)),
               WIN // LANES - 1: odd_chunk | (lane < CHUNK)}
    blk_ok = [i >= N_KBLK - 1 - b for b in range(N_KBLK - 1)]
    even_head = lax.broadcasted_iota(jnp.int32, (SUB, LANES), 1) < HEAD_DIM

    def pieces(r):
        out = []
        for b in range(N_KBLK):
            lo, hi = max(SUB * r, TQ * b), min(SUB * r + WIN, TQ * (b + 1))
            if lo < hi:
                out.append((b, lo - TQ * b, hi - TQ * b))
        return out

    def scores(r, p):
        rs, ps = slice(r * SUB, (r + 1) * SUB), slice(p * LANES, (p + 1) * LANES)
        q2 = _stack_heads(q_ref[0, rs, ps])
        return jnp.concatenate(
            [_dot_nt(q2, k_refs[b][0, a:z, ps]) for b, a, z in pieces(r)], axis=1)

    def finish(r, p, s):
        rs, ps = slice(r * SUB, (r + 1) * SUB), slice(p * LANES, (p + 1) * LANES)
        s = s + bias_ref[p]
        tiles = []
        for t in range(WIN // LANES):
            st = s[:, t * LANES:(t + 1) * LANES]
            b = (SUB * r + LANES * t) // TQ
            ok = edge_ok.get(t)
            if b < N_KBLK - 1:
                ok = blk_ok[b] if ok is None else ok & blk_ok[b]
            tiles.append(st if ok is None else jnp.where(ok, st, NEG_INF))
        (e,), l = _softmax_parts([jnp.concatenate(tiles, axis=1)])
        pb = e.astype(bf16)
        o, c0 = None, 0
        for b, a, z in pieces(r):
            ob = _dot(pb[:, c0:c0 + z - a], v_refs[b][0, a:z, ps])
            o = ob if o is None else o + ob
            c0 += z - a
        o = o / l
        cat_ref[rs, ps] = jnp.where(even_head, o[:SUB], o[SUB:]).astype(bf16)

    units = [(r, p) for r in range(TQ // SUB) for p in range(N_PAIRS)]
    pending = [scores(*u) for u in units[:LOOKAHEAD]]
    for n, unit in enumerate(units):
        if n + LOOKAHEAD < len(units):
            pending.append(scores(*units[n + LOOKAHEAD]))
        finish(*unit, pending.pop(0))

    oc = _ln_swish(oc_ref[0], lng_ref, lnb_ref).astype(bf16)
    _residual_out(x_ref, cat_ref, oc, wout_ref, gn_ref, o_ref, hn_ref)


def _mix_prompt(q, k, v, oc, x, fvec, wout, gn, lng, lnb):
    B, S, D = x.shape
    assert S % TQ == 0 and W_BAND % TQ == 0 and TQ % SUB == 0 and WIN % LANES == 0
    qspec = pl.BlockSpec((1, TQ, ATTN_WIDTH), lambda b, i: (b, i, 0))
    kspec = lambda back: pl.BlockSpec(
        (1, TQ, ATTN_WIDTH), lambda b, i: (b, jnp.maximum(i - back, 0), 0))
    kspecs = [kspec(N_KBLK - 1 - n) for n in range(N_KBLK)]
    xspec = pl.BlockSpec((1, TQ, D), lambda b, i: (b, i, 0))
    return pl.pallas_call(
        _mix_prompt_kernel,
        grid=(B, S // TQ),
        in_specs=[qspec] + kspecs + kspecs + [qspec, xspec,
                  _const_spec(fvec.shape), _const_spec(wout.shape), _const_spec(gn.shape),
                  _const_spec(lng.shape), _const_spec(lnb.shape)],
        out_specs=[xspec, xspec],
        out_shape=[jax.ShapeDtypeStruct((B, S, D), f32), jax.ShapeDtypeStruct((B, S, D), bf16)],
        scratch_shapes=[pltpu.VMEM((N_PAIRS, 2 * SUB, WIN), f32),
                        pltpu.VMEM((TQ, ATTN_WIDTH), bf16)],
        compiler_params=pltpu.CompilerParams(
            dimension_semantics=("arbitrary", "arbitrary"), vmem_limit_bytes=VMEM_LIMIT),
        name="mix_prompt",
    )(q, *([k] * N_KBLK), *([v] * N_KBLK), oc, x, fvec, wout, gn, lng, lnb)


def _mix_sample_kernel(q_ref, kn_ref, vn_ref, ck_ref, cv_ref, oc_ref, x_ref,
                       f_ref, wout_ref, gn_ref, o_ref, hn_ref, nk_ref, nv_ref, bias_ref, cat_ref):
    T = oc_ref.shape[1]
    W = ck_ref.shape[1]

    @pl.when(pl.program_id(0) == 0)
    def _():
        for h in range(N_HEADS):
            p, par = divmod(h, 2)
            bias_ref[p, par * T:(par + 1) * T, :] = _toeplitz(f_ref[h:h + 1, :], T)

    ck = ck_ref[0].reshape(W, ATTN_WIDTH)
    cv = cv_ref[0].reshape(W, ATTN_WIDTH)
    nk_ref[0, 0:W - T, :] = ck[T:W]
    nk_ref[0, W - T:W, :] = kn_ref[0]
    nv_ref[0, 0:W - T, :] = cv[T:W]
    nv_ref[0, W - T:W, :] = vn_ref[0]

    lane = lax.broadcasted_iota(jnp.int32, (T, LANES), 1)
    even_head = lane < HEAD_DIM
    for p in range(N_PAIRS):
        ps = slice(p * LANES, (p + 1) * LANES)
        kc = ck[:, ps].astype(bf16)
        vc = cv[:, ps].astype(bf16)
        kn = kn_ref[0, :, ps].astype(bf16)
        vn = vn_ref[0, :, ps].astype(bf16)
        q2 = _stack_heads(q_ref[0, :, ps])
        (ec, en), l = _softmax_parts([_dot_nt(q2, kc) + bias_ref[p, :, 0:W],
                                      _dot_nt(q2, kn) + bias_ref[p, :, W:W + T]])
        o = (_dot(ec.astype(bf16), vc) + _dot(en.astype(bf16), vn)) / l
        cat_ref[:, ps] = jnp.where(even_head, o[:T], o[T:]).astype(bf16)

    _residual_out(x_ref, cat_ref, oc_ref[0], wout_ref, gn_ref, o_ref, hn_ref)


def _mix_sample(q, kn, vn, ck, cv, oc, x, fvec, wout, gn):
    B, T, D = x.shape
    W = ck.shape[1]
    assert W == W_BAND and W + T <= F_LEN - T
    row = lambda r, n: pl.BlockSpec((1, r, n), lambda b: (b, 0, 0))
    return pl.pallas_call(
        _mix_sample_kernel,
        grid=(B,),
        in_specs=[row(T, ATTN_WIDTH)] * 3
                 + [pl.BlockSpec((1, W, N_HEADS, HEAD_DIM), lambda b: (b, 0, 0, 0))] * 2
                 + [row(T, CONV_WIDTH), row(T, D)]
                 + [_const_spec(a.shape) for a in (fvec, wout, gn)],
        out_specs=[row(T, D), row(T, D), row(W, ATTN_WIDTH), row(W, ATTN_WIDTH)],
        out_shape=[jax.ShapeDtypeStruct((B, T, D), f32),
                   jax.ShapeDtypeStruct((B, T, D), bf16),
                   jax.ShapeDtypeStruct((B, W, ATTN_WIDTH), f32),
                   jax.ShapeDtypeStruct((B, W, ATTN_WIDTH), f32)],
        scratch_shapes=[pltpu.VMEM((N_PAIRS, 2 * T, F_LEN), f32),
                        pltpu.VMEM((T, ATTN_WIDTH), bf16)],
        compiler_params=pltpu.CompilerParams(
            dimension_semantics=("arbitrary",), vmem_limit_bytes=VMEM_LIMIT),
        name="mix_sample",
    )(q, kn, vn, ck, cv, oc, x, fvec, wout, gn)


def _ffn_block(x, h, wg_ref, wu_ref, wd_ref, gf_ref):
    acc = x
    for c0 in range(0, D_FF, FFN_COLS):
        cs = slice(c0, min(c0 + FFN_COLS, D_FF))
        a = jax.nn.silu(_dot(h, wg_ref[:, cs])) * _dot(h, wu_ref[:, cs])
        acc = acc + _dot(a.astype(bf16), wd_ref[cs, :])
    return _rmsnorm(acc, gf_ref[...])


def _ffn_kernel(x_ref, h_ref, xs_ref, hs_ref, wg_ref, wu_ref, wd_ref, gf_ref, o_ref, os_ref):
    last = pl.num_programs(0) - 1

    @pl.when(pl.program_id(0) < last)
    def _():
        half = x_ref.shape[0] // 2
        for r in range(2):
            rs = slice(r * half, (r + 1) * half)
            o_ref[rs, :] = _ffn_block(x_ref[rs, :], h_ref[rs, :], wg_ref, wu_ref, wd_ref, gf_ref)

    @pl.when(pl.program_id(0) == last)
    def _():
        os_ref[...] = _ffn_block(xs_ref[...], hs_ref[...], wg_ref, wu_ref, wd_ref, gf_ref)


def _ffn(x, h, xs, hs, wg, wu, wd, gf):
    N, D = x.shape
    assert N % TM_FFN == 0
    n = N // TM_FFN
    tile = pl.BlockSpec((TM_FFN, D), lambda i: (jnp.minimum(i, n - 1), 0))
    return pl.pallas_call(
        _ffn_kernel,
        grid=(n + 1,),
        in_specs=[tile, tile] + [_const_spec(a.shape) for a in (xs, hs, wg, wu, wd, gf)],
        out_specs=[tile, _const_spec(xs.shape)],
        out_shape=[jax.ShapeDtypeStruct((N, D), f32), jax.ShapeDtypeStruct(xs.shape, f32)],
        compiler_params=pltpu.CompilerParams(
            dimension_semantics=("arbitrary",), vmem_limit_bytes=VMEM_LIMIT),
        name="ffn",
    )(x, h, xs, hs, wg, wu, wd, gf)


def _rel_bias_row(table):
    n_rel = table.shape[1]
    far = W_BAND - REL_CLIP
    assert n_rel == 2 * REL_CLIP + 1 and far + n_rel + SUB <= F_LEN
    rep = lambda col, n: jnp.broadcast_to(table[:, col:col + 1], (table.shape[0], n))
    return jnp.concatenate([rep(n_rel - 1, far), table[:, ::-1],
                            rep(0, F_LEN - SUB - far - n_rel), rep(n_rel - 1, SUB)], axis=1)


def kernel(x_prompt, x_sample, cache_k, cache_v, state_conv, g_mix, w_in, rel_table, w_dw, b_dw,
           ln_g, ln_b, w_out, g_ffn, w_gate, w_up, w_down, g_final):
    assert g_mix.shape[0] == 1, "single-layer trunk: the final RMSNorm is fused into the FFN kernel"
    B, S, D = x_prompt.shape
    Bs, T, _ = x_sample.shape
    W = cache_k.shape[2]
    gm, gff, gf = g_mix[0].reshape(1, D), g_ffn[0].reshape(1, D), g_final.reshape(1, D)
    win = w_in[0]
    conv_params = (w_dw[0],) + tuple(a[0].reshape(1, CONV_WIDTH) for a in (b_dw, ln_g, ln_b))
    heads = lambda a: a.reshape(1, a.shape[0], a.shape[1], N_HEADS, HEAD_DIM)
    fvec = _rel_bias_row(rel_table[0])

    (q, k, v, oc, kf, vf, ct, qs, kn, vn, ocs, nc, wout, wg, wu, wd) = _inproj(
        x_prompt, x_sample.reshape(Bs * T, D), state_conv[0], gm, win, *conv_params,
        (w_out[0], w_gate[0], w_up[0], w_down[0]))
    x1p, hp = _mix_prompt(q, k, v, oc, x_prompt, fvec, wout, gff, *conv_params[2:])

    r3 = lambda a: a.reshape(Bs, T, ATTN_WIDTH)
    x1s, hs, nk, nv = _mix_sample(
        r3(qs), r3(kn), r3(vn),
        cache_k[0], cache_v[0],
        ocs, x_sample, fvec, wout, gff)

    flat = lambda a: a.reshape(-1, D)
    y_prompt, y_sample = _ffn(flat(x1p), flat(hp), flat(x1s), flat(hs), wg, wu, wd, gf)
    return (y_prompt.reshape(B, S, D), y_sample.reshape(Bs, T, D), heads(kf), heads(vf), ct[None],
            heads(nk), heads(nv), nc[None])
```

```python
import functools

import jax
import jax.numpy as jnp
from jax import lax
from jax.experimental import pallas as pl
from jax.experimental.pallas import tpu as pltpu

D_MODEL = 1024
CHUNK = 64
LEFT_CHUNKS = 8
W_BAND = LEFT_CHUNKS * CHUNK
ATTN_WIDTH = 512
N_HEADS = 8
HEAD_DIM = 64
CONV_WIDTH = 512
CONV_KERNEL = 31
CONV_PAST = CONV_KERNEL - 1
REL_CLIP = 128
D_FF = 2816
RMS_EPS = 1e-6
LN_EPS = 1e-5
NEG_INF = -1e30
SCALE = HEAD_DIM ** -0.5

LANES = 128
SUBLANES = 8
N_PAIRS = ATTN_WIDTH // LANES
TT_PROJ = 64
PROJ_COLS = 256
TQ = 512
N_KBLK = W_BAND // TQ + 1
SUB = 2 * CHUNK
WIN = W_BAND + SUB
F_LEN = 1024
LOOKAHEAD = 1
TM_FFN = 512
MXU_TILE = 256
FFN_COLS = 4 * MXU_TILE
CONV_ROWS = 64
CONV_STEPS = 16
CONV_LEAD = 32
VMEM_LIMIT = 56 * 1024 * 1024

f32 = jnp.float32
bf16 = jnp.bfloat16


def _rmsnorm(x, g):
    return (x * lax.rsqrt(jnp.mean(x * x, axis=-1, keepdims=True) + RMS_EPS)) * g


def _dot(a, b):
    return jnp.dot(a, b, preferred_element_type=f32)


def _dot_nt(a, b):
    return lax.dot_general(a, b, (((1,), (1,)), ((), ())), preferred_element_type=f32)


def _const_spec(shape):
    nd = len(shape)
    return pl.BlockSpec(shape, lambda *_: (0,) * nd, pipeline_mode=pl.Buffered(1))


def _project_stages(h_ref, w_ref, q_store, k_store, v_store):
    def issue(base, cs):
        return _dot(h_ref[...], w_ref[:, base + cs.start:base + cs.stop])

    def q_finish(cs, z):
        q_store(cs, (z * SCALE).astype(bf16))

    stages = []
    for base, finish in ((0, q_finish), (ATTN_WIDTH, k_store), (2 * ATTN_WIDTH, v_store)):
        for c0 in range(0, ATTN_WIDTH, PROJ_COLS):
            cs = slice(c0, c0 + PROJ_COLS)
            stages.append((functools.partial(issue, base, cs), functools.partial(finish, cs)))
    return stages


def _cast_weight(w32_ref, w_ref):
    for c0 in range(0, w_ref.shape[1], MXU_TILE):
        w_ref[:, c0:c0 + MXU_TILE] = w32_ref[:, c0:c0 + MXU_TILE].astype(bf16)


def _glu(h_ref, w_ref):
    base = 3 * ATTN_WIDTH
    a = _dot(h_ref[...], w_ref[:, base:base + CONV_WIDTH])
    g = _dot(h_ref[...], w_ref[:, base + CONV_WIDTH:base + 2 * CONV_WIDTH])
    return a * jax.nn.sigmoid(g)


def _conv_stages(cbuf_ref, sh_ref, y_ref, rows, wdw_ref, bdw_ref):
    lead = CONV_LEAD - CONV_PAST
    span = rows + CONV_LEAD - SUBLANES
    rb = min(CONV_ROWS, rows)

    def shift_stage(m):
        sh_ref[m - 1, 0:span, :] = cbuf_ref[m:m + span, :]

    def tap_stage(cs, r0):
        acc = jnp.zeros((rb, LANES), f32)
        for j in range(CONV_KERNEL):
            a, m = divmod(j + lead, SUBLANES)
            src = cbuf_ref if m == 0 else sh_ref.at[m - 1]
            lo = r0 + SUBLANES * a
            acc = acc + wdw_ref[j:j + 1, cs] * src[lo:lo + rb, cs]
        y_ref[r0:r0 + rb, cs] = acc + bdw_ref[:, cs]

    shifts = [functools.partial(shift_stage, m) for m in range(1, SUBLANES)]
    taps = [functools.partial(tap_stage, slice(c * LANES, (c + 1) * LANES), r0)
            for c in range(CONV_WIDTH // LANES) for r0 in range(0, rows, rb)]
    return shifts, taps


def _conv_time_major(ut_ref, y_ref, steps, nb, wdw_ref, bdw_ref):
    for c in range(CONV_WIDTH // LANES):
        cs = slice(c * LANES, (c + 1) * LANES)
        for t0 in range(0, steps, CONV_STEPS):
            acc = jnp.zeros((CONV_STEPS * nb, LANES), f32)
            for j in range(CONV_KERNEL):
                acc = acc + wdw_ref[j:j + 1, cs] * ut_ref[c, (t0 + j) * nb:(t0 + j + CONV_STEPS) * nb, :]
            y_ref[c, t0 * nb:(t0 + CONV_STEPS) * nb, :] = acc + bdw_ref[:, cs]


def _ln_swish_tiles(y_tiles, lng_ref, lnb_ref):
    n = len(y_tiles) * LANES
    mu = sum(jnp.sum(y, axis=-1, keepdims=True) for y in y_tiles) / n
    yc = [y - mu for y in y_tiles]
    var = sum(jnp.sum(y * y, axis=-1, keepdims=True) for y in yc) / n
    inv = lax.rsqrt(var + LN_EPS)
    out = []
    for c, y in enumerate(yc):
        cs = slice(c * LANES, (c + 1) * LANES)
        yn = y * inv * lng_ref[:, cs] + lnb_ref[:, cs]
        out.append(yn * jax.nn.sigmoid(yn))
    return out


def _ln_swish(y, lng_ref, lnb_ref):
    mu = jnp.mean(y, axis=-1, keepdims=True)
    yc = y - mu
    yn = yc * lax.rsqrt(jnp.mean(yc * yc, axis=-1, keepdims=True) + LN_EPS)
    yn = yn * lng_ref[...] + lnb_ref[...]
    return yn * jax.nn.sigmoid(yn)


def _pipeline(mxu_stages, valu_stages):
    pending = mxu_stages[0][0]()
    done = 0
    for n, (_, consume) in enumerate(mxu_stages):
        cur = pending
        if n + 1 < len(mxu_stages):
            pending = mxu_stages[n + 1][0]()
        upto = (n + 1) * len(valu_stages) // len(mxu_stages)
        for stage in valu_stages[done:upto]:
            stage()
        done = upto
        consume(cur)


def _inproj_kernel(x_ref, xs_ref, st_ref, g_ref, w32_ref, wdw_ref, bdw_ref, lng_ref, lnb_ref,
                   *refs, n_cast):
    cast_in, refs = refs[:n_cast], refs[n_cast:]
    prompt_out, sample_out = refs[:7], refs[7:12]
    cast_out, scratch = refs[12:12 + n_cast], refs[12 + n_cast:]
    w_ref, prompt_scratch, sample_scratch = scratch[0], scratch[1:4], scratch[4:]
    for src, dst in zip(cast_in, cast_out):
        dst[...] = src[...].astype(bf16)
    last = pl.num_programs(0) - 1

    @pl.when(pl.program_id(0) == 0)
    def _():
        _cast_weight(w32_ref, w_ref)
        ut_ref, hist = prompt_scratch[1], CONV_PAST * x_ref.shape[0]
        ut_ref[:, 0:hist, :] = jnp.zeros((ut_ref.shape[0], hist, LANES), f32)

    conv_refs = (wdw_ref, bdw_ref, lng_ref, lnb_ref)
    pl.when(pl.program_id(0) < last)(functools.partial(
        _inproj_prompt_tile, x_ref, g_ref, w_ref, conv_refs, prompt_out, prompt_scratch))
    pl.when(pl.program_id(0) == last)(functools.partial(
        _inproj_sample_tokens, xs_ref, st_ref, g_ref, w_ref, conv_refs, sample_out, sample_scratch))


def _inproj_prompt_tile(x_ref, g_ref, w_ref, conv_refs, out_refs, scratch):
    wdw_ref, bdw_ref, lng_ref, lnb_ref = conv_refs
    q_ref, k_ref, v_ref, oc_ref, kf_ref, vf_ref, ct_ref = out_refs
    h_ref, ut_ref, y_ref = scratch
    nb = x_ref.shape[0]
    us = []
    for part in range(4):
        streams = slice(part * nb // 4, (part + 1) * nb // 4)
        hrows = slice(streams.start * TT_PROJ, streams.stop * TT_PROJ)
        xh = x_ref[streams].reshape(hrows.stop - hrows.start, D_MODEL)
        h_ref[hrows, :] = _rmsnorm(xh, g_ref[...]).astype(bf16)
        us.append(_glu(h_ref.at[hrows], w_ref))
    u = jnp.concatenate(us, axis=0)

    def store(ref):
        def put(cs, val):
            ref[:, :, cs] = val.reshape(nb, TT_PROJ, val.shape[-1])
        return put

    def kv_store(ref, full_ref):
        def put(cs, val):
            val = val.reshape(nb, TT_PROJ, val.shape[-1])
            ref[:, :, cs] = val.astype(bf16)
            full_ref[:, :, cs] = val
        return put

    for issue, consume in _project_stages(h_ref, w_ref, store(q_ref),
                                          kv_store(k_ref, kf_ref), kv_store(v_ref, vf_ref)):
        consume(issue())
    lane_tiles = [slice(c * LANES, (c + 1) * LANES) for c in range(CONV_WIDTH // LANES)]
    tail = 4 * SUBLANES
    for c, cs in enumerate(lane_tiles):
        for b in range(nb):
            ut_ref[c, pl.ds(CONV_PAST * nb + b, TT_PROJ, stride=nb), :] = (
                u[b * TT_PROJ:(b + 1) * TT_PROJ, cs])
            last = ut_ref[c, pl.ds((CONV_PAST + TT_PROJ - tail) * nb + b, tail, stride=nb), :]
            ct_ref[b, :, cs] = last[tail - CONV_PAST:]
    _conv_time_major(ut_ref, y_ref, TT_PROJ, nb, wdw_ref, bdw_ref)
    for c, cs in enumerate(lane_tiles):
        for b in range(nb):
            oc_ref[b, :, cs] = y_ref[c, pl.ds(b, TT_PROJ, stride=nb), :]

    ut_ref[:, 0:CONV_PAST * nb, :] = ut_ref[:, TT_PROJ * nb:(TT_PROJ + CONV_PAST) * nb, :]


def _inproj(x, xs, st, g, w, wdw, bdw, lng, lnb, cast_weights):
    B, S, D = x.shape
    Ns, Bs = xs.shape[0], st.shape[0]
    T = Ns // Bs
    assert S % TT_PROJ == 0 and W_BAND % TT_PROJ == 0 and TT_PROJ >= CONV_PAST
    assert T >= CONV_PAST and T % SUBLANES == 0
    steps = S // TT_PROJ
    first_kept = (S - W_BAND) // TT_PROJ
    tile = lambda n: pl.BlockSpec((B, TT_PROJ, n), lambda i: (0, jnp.minimum(i, steps - 1), 0))
    kept = pl.BlockSpec((B, TT_PROJ, ATTN_WIDTH),
                        lambda i: (0, jnp.clip(i - first_kept, 0, steps - 1 - first_kept), 0))
    act = jax.ShapeDtypeStruct((B, S, ATTN_WIDTH), bf16)
    rows = B * TT_PROJ
    sample_shapes = ([jax.ShapeDtypeStruct((Ns, ATTN_WIDTH), bf16)]
                     + [jax.ShapeDtypeStruct((Ns, ATTN_WIDTH), f32)] * 2
                     + [jax.ShapeDtypeStruct((Bs, T, CONV_WIDTH), bf16),
                        jax.ShapeDtypeStruct((Bs, CONV_PAST, CONV_WIDTH), f32)])

    def cast_spec(a):
        blk = next(r for r in (32, 64, 128, 256) if a.shape[0] % r == 0 and a.shape[0] // r <= steps)
        last = a.shape[0] // blk - 1
        return pl.BlockSpec((blk, a.shape[1]), lambda i: (jnp.minimum(i, last), 0))

    cast_specs = [cast_spec(a) for a in cast_weights]
    return pl.pallas_call(
        functools.partial(_inproj_kernel, n_cast=len(cast_weights)),
        grid=(steps + 1,),
        in_specs=[tile(D)] + [_const_spec(a.shape) for a in (xs, st, g, w, wdw, bdw, lng, lnb)]
                 + cast_specs,
        out_specs=[tile(ATTN_WIDTH)] * 4 + [kept] * 2 + [_const_spec((B, CONV_PAST, CONV_WIDTH))]
                  + [_const_spec(a.shape) for a in sample_shapes] + cast_specs,
        out_shape=[act] * 3 + [jax.ShapeDtypeStruct((B, S, CONV_WIDTH), f32)]
                  + [jax.ShapeDtypeStruct((B, W_BAND, ATTN_WIDTH), f32)] * 2
                  + [jax.ShapeDtypeStruct((B, CONV_PAST, CONV_WIDTH), f32)] + sample_shapes
                  + [jax.ShapeDtypeStruct(a.shape, bf16) for a in cast_weights],
        scratch_shapes=[pltpu.VMEM(w.shape, bf16),
                        pltpu.VMEM((rows, D), bf16),
                        pltpu.VMEM((CONV_WIDTH // LANES, (CONV_PAST + TT_PROJ) * B, LANES), f32),
                        pltpu.VMEM((CONV_WIDTH // LANES, TT_PROJ * B, LANES), f32),
                        pltpu.VMEM((Ns, D), bf16),
                        pltpu.VMEM((CONV_LEAD + T, CONV_WIDTH), f32),
                        pltpu.VMEM((SUBLANES - 1, CONV_LEAD + T - SUBLANES, CONV_WIDTH), f32),
                        pltpu.VMEM((T, CONV_WIDTH), f32)],
        compiler_params=pltpu.CompilerParams(
            dimension_semantics=("arbitrary",), vmem_limit_bytes=VMEM_LIMIT),
        name="inproj",
    )(x, xs, st, g, w, wdw, bdw, lng, lnb, *cast_weights)


def _inproj_sample_tokens(x_ref, st_ref, g_ref, w_ref, conv_refs, out_refs, scratch):
    wdw_ref, bdw_ref, lng_ref, lnb_ref = conv_refs
    q_ref, k_ref, v_ref, oc_ref, nc_ref = out_refs
    h_ref, cbuf_ref, sh_ref, y_ref = scratch
    B, T, _ = oc_ref.shape
    h_ref[...] = _rmsnorm(x_ref[...], g_ref[...]).astype(bf16)
    u = _glu(h_ref, w_ref)

    def store(ref):
        def put(cs, val):
            ref[:, cs] = val
        return put

    proj = _project_stages(h_ref, w_ref, store(q_ref), store(k_ref), store(v_ref))
    lead = CONV_LEAD - CONV_PAST

    def conv_stage(b):
        cbuf_ref[0:SUBLANES, :] = jnp.zeros((SUBLANES, CONV_WIDTH), f32)
        cbuf_ref[lead:CONV_LEAD, :] = st_ref[b]
        cbuf_ref[CONV_LEAD:CONV_LEAD + T, :] = u[b * T:(b + 1) * T]
        nc_ref[b] = cbuf_ref[CONV_LEAD + T - CONV_PAST:CONV_LEAD + T, :]
        shifts, taps = _conv_stages(cbuf_ref, sh_ref, y_ref, T, wdw_ref, bdw_ref)
        for stage in shifts + taps:
            stage()
        oc_ref[b] = _ln_swish(y_ref[...], lng_ref, lnb_ref).astype(bf16)

    _pipeline(proj, [functools.partial(conv_stage, b) for b in range(B)])


def _toeplitz(frow, rows):
    return pltpu.roll(jnp.broadcast_to(frow, (rows, F_LEN)), 0, 1, stride=1, stride_axis=0)


def _softmax_parts(s_parts):
    m = functools.reduce(jnp.maximum, [jnp.max(s, axis=-1, keepdims=True) for s in s_parts])
    e_parts = [jnp.exp(s - m) for s in s_parts]
    l = functools.reduce(jnp.add, [jnp.sum(e, axis=-1, keepdims=True) for e in e_parts])
    return e_parts, l


def _stack_heads(q):
    lane = lax.broadcasted_iota(jnp.int32, q.shape, 1)
    qf = q.astype(f32)
    even = jnp.where(lane < HEAD_DIM, qf, 0.0)
    odd = jnp.where(lane < HEAD_DIM, 0.0, qf)
    return jnp.concatenate([even, odd], axis=0).astype(bf16)


def _residual_out(x_ref, cat_ref, oc, wout_ref, gn_ref, o_ref, hn_ref):
    x1 = (x_ref[0] + _dot(cat_ref[...], wout_ref[0:ATTN_WIDTH, :])
          + _dot(oc, wout_ref[ATTN_WIDTH:, :]))
    o_ref[0] = x1
    hn_ref[0] = _rmsnorm(x1, gn_ref[...]).astype(bf16)


def _mix_prompt_kernel(q_ref, *refs):
    k_refs, v_refs = refs[:N_KBLK], refs[N_KBLK:2 * N_KBLK]
    (oc_ref, x_ref, f_ref, wout_ref, gn_ref, lng_ref, lnb_ref,
     o_ref, hn_ref, bias_ref, cat_ref) = refs[2 * N_KBLK:]
    i = pl.program_id(1)

    @pl.when((pl.program_id(0) == 0) & (i == 0))
    def _():
        for h in range(N_HEADS):
            p, par = divmod(h, 2)
            bias_ref[p, par * SUB:(par + 1) * SUB, :] = _toeplitz(f_ref[h:h + 1, :], SUB)[:, :WIN]

    row = lax.broadcasted_iota(jnp.int32, (2 * SUB, LANES), 0)
    lane = lax.broadcasted_iota(jnp.int32, (2 * SUB, LANES), 1)
    odd_chunk = (row & CHUNK) != 0
    edge_ok = {0: jnp.logical_not(odd_chunk & (lane < CHUNK)),
               WIN // LANES - 1: odd_chunk | (lane < CHUNK)}
    blk_ok = [i >= N_KBLK - 1 - b for b in range(N_KBLK - 1)]
    even_head = lax.broadcasted_iota(jnp.int32, (SUB, LANES), 1) < HEAD_DIM

    def pieces(r):
        out = []
        for b in range(N_KBLK):
            lo, hi = max(SUB * r, TQ * b), min(SUB * r + WIN, TQ * (b + 1))
            if lo < hi:
                out.append((b, lo - TQ * b, hi - TQ * b))
        return out

    def scores(r, p):
        rs, ps = slice(r * SUB, (r + 1) * SUB), slice(p * LANES, (p + 1) * LANES)
        q2 = _stack_heads(q_ref[0, rs, ps])
        return jnp.concatenate(
            [_dot_nt(q2, k_refs[b][0, a:z, ps]) for b, a, z in pieces(r)], axis=1)

    def finish(r, p, s):
        rs, ps = slice(r * SUB, (r + 1) * SUB), slice(p * LANES, (p + 1) * LANES)
        s = s + bias_ref[p]
        tiles = []
        for t in range(WIN // LANES):
            st = s[:, t * LANES:(t + 1) * LANES]
            b = (SUB * r + LANES * t) // TQ
            ok = edge_ok.get(t)
            if b < N_KBLK - 1:
                ok = blk_ok[b] if ok is None else ok & blk_ok[b]
            tiles.append(st if ok is None else jnp.where(ok, st, NEG_INF))
        (e,), l = _softmax_parts([jnp.concatenate(tiles, axis=1)])
        pb = e.astype(bf16)
        o, c0 = None, 0
        for b, a, z in pieces(r):
            ob = _dot(pb[:, c0:c0 + z - a], v_refs[b][0, a:z, ps])
            o = ob if o is None else o + ob
            c0 += z - a
        o = o / l
        cat_ref[rs, ps] = jnp.where(even_head, o[:SUB], o[SUB:]).astype(bf16)

    units = [(r, p) for r in range(TQ // SUB) for p in range(N_PAIRS)]
    pending = [scores(*u) for u in units[:LOOKAHEAD]]
    for n, unit in enumerate(units):
        if n + LOOKAHEAD < len(units):
            pending.append(scores(*units[n + LOOKAHEAD]))
        finish(*unit, pending.pop(0))

    oc = _ln_swish(oc_ref[0], lng_ref, lnb_ref).astype(bf16)
    _residual_out(x_ref, cat_ref, oc, wout_ref, gn_ref, o_ref, hn_ref)


def _mix_prompt(q, k, v, oc, x, fvec, wout, gn, lng, lnb):
    B, S, D = x.shape
    assert S % TQ == 0 and W_BAND % TQ == 0 and TQ % SUB == 0 and WIN % LANES == 0
    qspec = pl.BlockSpec((1, TQ, ATTN_WIDTH), lambda b, i: (b, i, 0))
    kspec = lambda back: pl.BlockSpec(
        (1, TQ, ATTN_WIDTH), lambda b, i: (b, jnp.maximum(i - back, 0), 0))
    kspecs = [kspec(N_KBLK - 1 - n) for n in range(N_KBLK)]
    xspec = pl.BlockSpec((1, TQ, D), lambda b, i: (b, i, 0))
    return pl.pallas_call(
        _mix_prompt_kernel,
        grid=(B, S // TQ),
        in_specs=[qspec] + kspecs + kspecs + [qspec, xspec,
                  _const_spec(fvec.shape), _const_spec(wout.shape), _const_spec(gn.shape),
                  _const_spec(lng.shape), _const_spec(lnb.shape)],
        out_specs=[xspec, xspec],
        out_shape=[jax.ShapeDtypeStruct((B, S, D), f32), jax.ShapeDtypeStruct((B, S, D), bf16)],
        scratch_shapes=[pltpu.VMEM((N_PAIRS, 2 * SUB, WIN), f32),
                        pltpu.VMEM((TQ, ATTN_WIDTH), bf16)],
        compiler_params=pltpu.CompilerParams(
            dimension_semantics=("arbitrary", "arbitrary"), vmem_limit_bytes=VMEM_LIMIT),
        name="mix_prompt",
    )(q, *([k] * N_KBLK), *([v] * N_KBLK), oc, x, fvec, wout, gn, lng, lnb)


def _mix_sample_kernel(q_ref, kn_ref, vn_ref, ck_ref, cv_ref, oc_ref, x_ref,
                       f_ref, wout_ref, gn_ref, o_ref, hn_ref, nk_ref, nv_ref, bias_ref, cat_ref):
    T = oc_ref.shape[1]
    W = ck_ref.shape[1]

    @pl.when(pl.program_id(0) == 0)
    def _():
        for h in range(N_HEADS):
            p, par = divmod(h, 2)
            bias_ref[p, par * T:(par + 1) * T, :] = _toeplitz(f_ref[h:h + 1, :], T)

    nk_ref[0, 0:W - T, :] = ck_ref[0, T:W, :]
    nk_ref[0, W - T:W, :] = kn_ref[0]
    nv_ref[0, 0:W - T, :] = cv_ref[0, T:W, :]
    nv_ref[0, W - T:W, :] = vn_ref[0]

    lane = lax.broadcasted_iota(jnp.int32, (T, LANES), 1)
    even_head = lane < HEAD_DIM
    for p in range(N_PAIRS):
        ps = slice(p * LANES, (p + 1) * LANES)
        kc = ck_ref[0, :, ps].astype(bf16)
        vc = cv_ref[0, :, ps].astype(bf16)
        kn = kn_ref[0, :, ps].astype(bf16)
        vn = vn_ref[0, :, ps].astype(bf16)
        q2 = _stack_heads(q_ref[0, :, ps])
        (ec, en), l = _softmax_parts([_dot_nt(q2, kc) + bias_ref[p, :, 0:W],
                                      _dot_nt(q2, kn) + bias_ref[p, :, W:W + T]])
        o = (_dot(ec.astype(bf16), vc) + _dot(en.astype(bf16), vn)) / l
        cat_ref[:, ps] = jnp.where(even_head, o[:T], o[T:]).astype(bf16)

    _residual_out(x_ref, cat_ref, oc_ref[0], wout_ref, gn_ref, o_ref, hn_ref)


def _mix_sample(q, kn, vn, ck, cv, oc, x, fvec, wout, gn):
    B, T, D = x.shape
    W = ck.shape[1]
    assert W == W_BAND and W + T <= F_LEN - T
    row = lambda r, n: pl.BlockSpec((1, r, n), lambda b: (b, 0, 0))
    return pl.pallas_call(
        _mix_sample_kernel,
        grid=(B,),
        in_specs=[row(T, ATTN_WIDTH)] * 3 + [row(W, ATTN_WIDTH)] * 2
                 + [row(T, CONV_WIDTH), row(T, D)]
                 + [_const_spec(a.shape) for a in (fvec, wout, gn)],
        out_specs=[row(T, D), row(T, D), row(W, ATTN_WIDTH), row(W, ATTN_WIDTH)],
        out_shape=[jax.ShapeDtypeStruct((B, T, D), f32),
                   jax.ShapeDtypeStruct((B, T, D), bf16),
                   jax.ShapeDtypeStruct((B, W, ATTN_WIDTH), f32),
                   jax.ShapeDtypeStruct((B, W, ATTN_WIDTH), f32)],
        scratch_shapes=[pltpu.VMEM((N_PAIRS, 2 * T, F_LEN), f32),
                        pltpu.VMEM((T, ATTN_WIDTH), bf16)],
        compiler_params=pltpu.CompilerParams(
            dimension_semantics=("arbitrary",), vmem_limit_bytes=VMEM_LIMIT),
        name="mix_sample",
    )(q, kn, vn, ck, cv, oc, x, fvec, wout, gn)


def _ffn_block(x, h, wg_ref, wu_ref, wd_ref, gf_ref):
    acc = x
    for c0 in range(0, D_FF, FFN_COLS):
        cs = slice(c0, min(c0 + FFN_COLS, D_FF))
        a = jax.nn.silu(_dot(h, wg_ref[:, cs])) * _dot(h, wu_ref[:, cs])
        acc = acc + _dot(a.astype(bf16), wd_ref[cs, :])
    return _rmsnorm(acc, gf_ref[...])


def _ffn_kernel(x_ref, h_ref, xs_ref, hs_ref, wg_ref, wu_ref, wd_ref, gf_ref, o_ref, os_ref):
    last = pl.num_programs(0) - 1

    @pl.when(pl.program_id(0) < last)
    def _():
        half = x_ref.shape[0] // 2
        for r in range(2):
            rs = slice(r * half, (r + 1) * half)
            o_ref[rs, :] = _ffn_block(x_ref[rs, :], h_ref[rs, :], wg_ref, wu_ref, wd_ref, gf_ref)

    @pl.when(pl.program_id(0) == last)
    def _():
        os_ref[...] = _ffn_block(xs_ref[...], hs_ref[...], wg_ref, wu_ref, wd_ref, gf_ref)


def _ffn(x, h, xs, hs, wg, wu, wd, gf):
    N, D = x.shape
    assert N % TM_FFN == 0
    n = N // TM_FFN
    tile = pl.BlockSpec((TM_FFN, D), lambda i: (jnp.minimum(i, n - 1), 0))
    return pl.pallas_call(
        _ffn_kernel,
        grid=(n + 1,),
        in_specs=[tile, tile] + [_const_spec(a.shape) for a in (xs, hs, wg, wu, wd, gf)],
        out_specs=[tile, _const_spec(xs.shape)],
        out_shape=[jax.ShapeDtypeStruct((N, D), f32), jax.ShapeDtypeStruct(xs.shape, f32)],
        compiler_params=pltpu.CompilerParams(
            dimension_semantics=("arbitrary",), vmem_limit_bytes=VMEM_LIMIT),
        name="ffn",
    )(x, h, xs, hs, wg, wu, wd, gf)


def _rel_bias_row(table):
    n_rel = table.shape[1]
    far = W_BAND - REL_CLIP
    assert n_rel == 2 * REL_CLIP + 1 and far + n_rel + SUB <= F_LEN
    rep = lambda col, n: jnp.broadcast_to(table[:, col:col + 1], (table.shape[0], n))
    return jnp.concatenate([rep(n_rel - 1, far), table[:, ::-1],
                            rep(0, F_LEN - SUB - far - n_rel), rep(n_rel - 1, SUB)], axis=1)


def kernel(x_prompt, x_sample, cache_k, cache_v, state_conv, g_mix, w_in, rel_table, w_dw, b_dw,
           ln_g, ln_b, w_out, g_ffn, w_gate, w_up, w_down, g_final):
    assert g_mix.shape[0] == 1, "single-layer trunk: the final RMSNorm is fused into the FFN kernel"
    B, S, D = x_prompt.shape
    Bs, T, _ = x_sample.shape
    W = cache_k.shape[2]
    gm, gff, gf = g_mix[0].reshape(1, D), g_ffn[0].reshape(1, D), g_final.reshape(1, D)
    win = w_in[0]
    conv_params = (w_dw[0],) + tuple(a[0].reshape(1, CONV_WIDTH) for a in (b_dw, ln_g, ln_b))
    heads = lambda a: a.reshape(1, a.shape[0], a.shape[1], N_HEADS, HEAD_DIM)
    fvec = _rel_bias_row(rel_table[0])

    (q, k, v, oc, kf, vf, ct, qs, kn, vn, ocs, nc, wout, wg, wu, wd) = _inproj(
        x_prompt, x_sample.reshape(Bs * T, D), state_conv[0], gm, win, *conv_params,
        (w_out[0], w_gate[0], w_up[0], w_down[0]))
    x1p, hp = _mix_prompt(q, k, v, oc, x_prompt, fvec, wout, gff, *conv_params[2:])

    r3 = lambda a: a.reshape(Bs, T, ATTN_WIDTH)
    x1s, hs, nk, nv = _mix_sample(
        r3(qs), r3(kn), r3(vn),
        cache_k[0].reshape(Bs, W, ATTN_WIDTH), cache_v[0].reshape(Bs, W, ATTN_WIDTH),
        ocs, x_sample, fvec, wout, gff)

    flat = lambda a: a.reshape(-1, D)
    y_prompt, y_sample = _ffn(flat(x1p), flat(hp), flat(x1s), flat(hs), wg, wu, wd, gf)
    return (y_prompt.reshape(B, S, D), y_sample.reshape(Bs, T, D), heads(kf), heads(vf), ct[None],
            heads(nk), heads(nv), nc[None])
```

```python
import functools

import jax
import jax.numpy as jnp
from jax import lax
from jax.experimental import pallas as pl
from jax.experimental.pallas import tpu as pltpu

D_MODEL = 1024
CHUNK = 64
LEFT_CHUNKS = 8
W_BAND = LEFT_CHUNKS * CHUNK
ATTN_WIDTH = 512
N_HEADS = 8
HEAD_DIM = 64
CONV_WIDTH = 512
CONV_KERNEL = 31
CONV_PAST = CONV_KERNEL - 1
REL_CLIP = 128
D_FF = 2816
RMS_EPS = 1e-6
LN_EPS = 1e-5
NEG_INF = -1e30
SCALE = HEAD_DIM ** -0.5

LANES = 128
SUBLANES = 8
N_PAIRS = ATTN_WIDTH // LANES
TT_PROJ = 64
Z_STRIDE = TT_PROJ + SUBLANES
PROJ_COLS = 256
TQ = 512
N_KBLK = W_BAND // TQ + 1
SUB = 2 * CHUNK
WIN = W_BAND + SUB
F_LEN = 1024
LOOKAHEAD = 1
TM_FFN = 1024
FFN_ROWS = 256
MXU_TILE = 256
FFN_COLS = 4 * MXU_TILE
CONV_ROWS = 64
CONV_STEPS = 16
CONV_LEAD = 32
VMEM_LIMIT = 56 * 1024 * 1024

f32 = jnp.float32
bf16 = jnp.bfloat16


def _rmsnorm(x, g):
    return (x * lax.rsqrt(jnp.mean(x * x, axis=-1, keepdims=True) + RMS_EPS)) * g


def _dot(a, b):
    return jnp.dot(a, b, preferred_element_type=f32)


def _dot_nt(a, b):
    return lax.dot_general(a, b, (((1,), (1,)), ((), ())), preferred_element_type=f32)


def _const_spec(shape):
    nd = len(shape)
    return pl.BlockSpec(shape, lambda *_: (0,) * nd, pipeline_mode=pl.Buffered(1))


def _project_stages(h_ref, w_ref, q_store, k_store, v_store):
    def issue(base, cs):
        return _dot(h_ref[...], w_ref[:, base + cs.start:base + cs.stop])

    def q_finish(cs, z):
        q_store(cs, (z * SCALE).astype(bf16))

    stages = []
    for base, finish in ((0, q_finish), (ATTN_WIDTH, k_store), (2 * ATTN_WIDTH, v_store)):
        for c0 in range(0, ATTN_WIDTH, PROJ_COLS):
            cs = slice(c0, c0 + PROJ_COLS)
            stages.append((functools.partial(issue, base, cs), functools.partial(finish, cs)))
    return stages


def _cast_weight(w32_ref, w_ref):
    for c0 in range(0, w_ref.shape[1], MXU_TILE):
        w_ref[:, c0:c0 + MXU_TILE] = w32_ref[:, c0:c0 + MXU_TILE].astype(bf16)


def _glu(h_ref, w_ref):
    base = 3 * ATTN_WIDTH
    a = _dot(h_ref[...], w_ref[:, base:base + CONV_WIDTH])
    g = _dot(h_ref[...], w_ref[:, base + CONV_WIDTH:base + 2 * CONV_WIDTH])
    return a * jax.nn.sigmoid(g)


def _conv_stages(cbuf_ref, sh_ref, y_ref, rows, wdw_ref, bdw_ref):
    lead = CONV_LEAD - CONV_PAST
    span = rows + CONV_LEAD - SUBLANES
    rb = min(CONV_ROWS, rows)

    def shift_stage(m):
        sh_ref[m - 1, 0:span, :] = cbuf_ref[m:m + span, :]

    def tap_stage(cs, r0):
        acc = jnp.zeros((rb, LANES), f32)
        for j in range(CONV_KERNEL):
            a, m = divmod(j + lead, SUBLANES)
            src = cbuf_ref if m == 0 else sh_ref.at[m - 1]
            lo = r0 + SUBLANES * a
            acc = acc + wdw_ref[j:j + 1, cs] * src[lo:lo + rb, cs]
        y_ref[r0:r0 + rb, cs] = acc + bdw_ref[:, cs]

    shifts = [functools.partial(shift_stage, m) for m in range(1, SUBLANES)]
    taps = [functools.partial(tap_stage, slice(c * LANES, (c + 1) * LANES), r0)
            for c in range(CONV_WIDTH // LANES) for r0 in range(0, rows, rb)]
    return shifts, taps


def _conv_time_major(ut_ref, y_ref, steps, wdw_ref, bdw_ref):
    for c in range(CONV_WIDTH // LANES):
        cs = slice(c * LANES, (c + 1) * LANES)
        for t0 in range(0, steps, CONV_STEPS):
            acc = jnp.zeros((CONV_STEPS, ut_ref.shape[2], LANES), f32)
            for j in range(CONV_KERNEL):
                acc = acc + wdw_ref[j:j + 1, cs] * ut_ref[c, t0 + j:t0 + j + CONV_STEPS]
            y_ref[c, t0:t0 + CONV_STEPS] = acc + bdw_ref[:, cs]


def _ln_swish_tiles(y_tiles, lng_ref, lnb_ref):
    n = len(y_tiles) * LANES
    mu = sum(jnp.sum(y, axis=-1, keepdims=True) for y in y_tiles) / n
    yc = [y - mu for y in y_tiles]
    var = sum(jnp.sum(y * y, axis=-1, keepdims=True) for y in yc) / n
    inv = lax.rsqrt(var + LN_EPS)
    out = []
    for c, y in enumerate(yc):
        cs = slice(c * LANES, (c + 1) * LANES)
        yn = y * inv * lng_ref[:, cs] + lnb_ref[:, cs]
        out.append(yn * jax.nn.sigmoid(yn))
    return out


def _ln_swish(y, lng_ref, lnb_ref):
    mu = jnp.mean(y, axis=-1, keepdims=True)
    yc = y - mu
    yn = yc * lax.rsqrt(jnp.mean(yc * yc, axis=-1, keepdims=True) + LN_EPS)
    yn = yn * lng_ref[...] + lnb_ref[...]
    return yn * jax.nn.sigmoid(yn)


def _pipeline(mxu_stages, valu_stages):
    pending = mxu_stages[0][0]()
    done = 0
    for n, (_, consume) in enumerate(mxu_stages):
        cur = pending
        if n + 1 < len(mxu_stages):
            pending = mxu_stages[n + 1][0]()
        upto = (n + 1) * len(valu_stages) // len(mxu_stages)
        for stage in valu_stages[done:upto]:
            stage()
        done = upto
        consume(cur)


def _inproj_kernel(x_ref, xs_ref, st_ref, g_ref, w32_ref, wdw_ref, bdw_ref, lng_ref, lnb_ref,
                   *refs, n_cast):
    cast_in, refs = refs[:n_cast], refs[n_cast:]
    prompt_out, sample_out = refs[:7], refs[7:12]
    cast_out, scratch = refs[12:12 + n_cast], refs[12 + n_cast:]
    w_ref, prompt_scratch, sample_scratch = scratch[0], scratch[1:5], scratch[5:]
    for src, dst in zip(cast_in, cast_out):
        dst[...] = src[...].astype(bf16)
    last = pl.num_programs(0) - 1

    @pl.when(pl.program_id(0) == 0)
    def _():
        _cast_weight(w32_ref, w_ref)
        ut_ref = prompt_scratch[2]
        ut_ref[:, 0:CONV_PAST] = jnp.zeros((ut_ref.shape[0], CONV_PAST) + ut_ref.shape[2:], f32)

    conv_refs = (wdw_ref, bdw_ref, lng_ref, lnb_ref)
    pl.when(pl.program_id(0) < last)(functools.partial(
        _inproj_prompt_tile, x_ref, g_ref, w_ref, conv_refs, prompt_out, prompt_scratch))
    pl.when(pl.program_id(0) == last)(functools.partial(
        _inproj_sample_tokens, xs_ref, st_ref, g_ref, w_ref, conv_refs, sample_out, sample_scratch))


def _inproj_prompt_tile(x_ref, g_ref, w_ref, conv_refs, out_refs, scratch):
    wdw_ref, bdw_ref, lng_ref, lnb_ref = conv_refs
    q_ref, k_ref, v_ref, oc_ref, kf_ref, vf_ref, ct_ref = out_refs
    h_ref, z_ref, ut_ref, y_ref = scratch
    nb = x_ref.shape[0]
    rows = nb * TT_PROJ
    h_ref[...] = _rmsnorm(x_ref[...].reshape(rows, D_MODEL), g_ref[...]).astype(bf16)

    def store(ref):
        def put(cs, val):
            ref[:, :, cs] = val.reshape(nb, TT_PROJ, val.shape[-1])
        return put

    def kv_store(ref, full_ref):
        def put(cs, val):
            val = val.reshape(nb, TT_PROJ, val.shape[-1])
            ref[:, :, cs] = val.astype(bf16)
            full_ref[:, :, cs] = val
        return put

    u = _glu(h_ref, w_ref)
    for issue, consume in _project_stages(h_ref, w_ref, store(q_ref),
                                          kv_store(k_ref, kf_ref), kv_store(v_ref, vf_ref)):
        consume(issue())
    lane_tiles = [slice(c * LANES, (c + 1) * LANES) for c in range(CONV_WIDTH // LANES)]
    for c, cs in enumerate(lane_tiles):
        for b in range(nb):
            z_ref[c, Z_STRIDE * b:Z_STRIDE * b + TT_PROJ, :] = u[b * TT_PROJ:(b + 1) * TT_PROJ, cs]
            ct_ref[b, :, cs] = z_ref[c, Z_STRIDE * b + TT_PROJ - CONV_PAST:Z_STRIDE * b + TT_PROJ, :]
        for t in range(TT_PROJ):
            ut_ref[c, CONV_PAST + t] = z_ref[c, pl.ds(t, nb, stride=Z_STRIDE), :]
    _conv_time_major(ut_ref, y_ref, TT_PROJ, wdw_ref, bdw_ref)
    for c, cs in enumerate(lane_tiles):
        for t in range(TT_PROJ):
            z_ref[c, pl.ds(t, nb, stride=Z_STRIDE), :] = y_ref[c, t]
        for b in range(nb):
            oc_ref[b, :, cs] = z_ref[c, Z_STRIDE * b:Z_STRIDE * b + TT_PROJ, :]

    ut_ref[:, 0:CONV_PAST] = ut_ref[:, TT_PROJ:TT_PROJ + CONV_PAST]


def _inproj(x, xs, st, g, w, wdw, bdw, lng, lnb, cast_weights):
    B, S, D = x.shape
    Ns, Bs = xs.shape[0], st.shape[0]
    T = Ns // Bs
    assert S % TT_PROJ == 0 and W_BAND % TT_PROJ == 0 and TT_PROJ >= CONV_PAST
    assert T >= CONV_PAST and T % SUBLANES == 0
    steps = S // TT_PROJ
    first_kept = (S - W_BAND) // TT_PROJ
    tile = lambda n: pl.BlockSpec((B, TT_PROJ, n), lambda i: (0, jnp.minimum(i, steps - 1), 0))
    kept = pl.BlockSpec((B, TT_PROJ, ATTN_WIDTH),
                        lambda i: (0, jnp.clip(i - first_kept, 0, steps - 1 - first_kept), 0))
    act = jax.ShapeDtypeStruct((B, S, ATTN_WIDTH), bf16)
    rows = B * TT_PROJ
    sample_shapes = ([jax.ShapeDtypeStruct((Ns, ATTN_WIDTH), bf16)]
                     + [jax.ShapeDtypeStruct((Ns, ATTN_WIDTH), f32)] * 2
                     + [jax.ShapeDtypeStruct((Bs, T, CONV_WIDTH), bf16),
                        jax.ShapeDtypeStruct((Bs, CONV_PAST, CONV_WIDTH), f32)])

    def cast_spec(a):
        blk = next(r for r in (32, 64, 128, 256) if a.shape[0] % r == 0 and a.shape[0] // r <= steps)
        last = a.shape[0] // blk - 1
        return pl.BlockSpec((blk, a.shape[1]), lambda i: (jnp.minimum(i, last), 0))

    cast_specs = [cast_spec(a) for a in cast_weights]
    return pl.pallas_call(
        functools.partial(_inproj_kernel, n_cast=len(cast_weights)),
        grid=(steps + 1,),
        in_specs=[tile(D)] + [_const_spec(a.shape) for a in (xs, st, g, w, wdw, bdw, lng, lnb)]
                 + cast_specs,
        out_specs=[tile(ATTN_WIDTH)] * 4 + [kept] * 2 + [_const_spec((B, CONV_PAST, CONV_WIDTH))]
                  + [_const_spec(a.shape) for a in sample_shapes] + cast_specs,
        out_shape=[act] * 3 + [jax.ShapeDtypeStruct((B, S, CONV_WIDTH), f32)]
                  + [jax.ShapeDtypeStruct((B, W_BAND, ATTN_WIDTH), f32)] * 2
                  + [jax.ShapeDtypeStruct((B, CONV_PAST, CONV_WIDTH), f32)] + sample_shapes
                  + [jax.ShapeDtypeStruct(a.shape, bf16) for a in cast_weights],
        scratch_shapes=[pltpu.VMEM(w.shape, bf16),
                        pltpu.VMEM((rows, D), bf16),
                        pltpu.VMEM((CONV_WIDTH // LANES, B * Z_STRIDE, LANES), f32),
                        pltpu.VMEM((CONV_WIDTH // LANES, CONV_PAST + TT_PROJ, B, LANES), f32),
                        pltpu.VMEM((CONV_WIDTH // LANES, TT_PROJ, B, LANES), f32),
                        pltpu.VMEM((Ns, D), bf16),
                        pltpu.VMEM((CONV_LEAD + T, CONV_WIDTH), f32),
                        pltpu.VMEM((SUBLANES - 1, CONV_LEAD + T - SUBLANES, CONV_WIDTH), f32),
                        pltpu.VMEM((T, CONV_WIDTH), f32)],
        compiler_params=pltpu.CompilerParams(
            dimension_semantics=("arbitrary",), vmem_limit_bytes=VMEM_LIMIT),
        name="inproj",
    )(x, xs, st, g, w, wdw, bdw, lng, lnb, *cast_weights)


def _inproj_sample_tokens(x_ref, st_ref, g_ref, w_ref, conv_refs, out_refs, scratch):
    wdw_ref, bdw_ref, lng_ref, lnb_ref = conv_refs
    q_ref, k_ref, v_ref, oc_ref, nc_ref = out_refs
    h_ref, cbuf_ref, sh_ref, y_ref = scratch
    B, T, _ = oc_ref.shape
    h_ref[...] = _rmsnorm(x_ref[...], g_ref[...]).astype(bf16)
    u = _glu(h_ref, w_ref)

    def store(ref):
        def put(cs, val):
            ref[:, cs] = val
        return put

    proj = _project_stages(h_ref, w_ref, store(q_ref), store(k_ref), store(v_ref))
    lead = CONV_LEAD - CONV_PAST

    def conv_stage(b):
        cbuf_ref[0:SUBLANES, :] = jnp.zeros((SUBLANES, CONV_WIDTH), f32)
        cbuf_ref[lead:CONV_LEAD, :] = st_ref[b]
        cbuf_ref[CONV_LEAD:CONV_LEAD + T, :] = u[b * T:(b + 1) * T]
        nc_ref[b] = cbuf_ref[CONV_LEAD + T - CONV_PAST:CONV_LEAD + T, :]
        shifts, taps = _conv_stages(cbuf_ref, sh_ref, y_ref, T, wdw_ref, bdw_ref)
        for stage in shifts + taps:
            stage()
        oc_ref[b] = _ln_swish(y_ref[...], lng_ref, lnb_ref).astype(bf16)

    _pipeline(proj, [functools.partial(conv_stage, b) for b in range(B)])


def _toeplitz(frow, rows):
    return pltpu.roll(jnp.broadcast_to(frow, (rows, F_LEN)), 0, 1, stride=1, stride_axis=0)


def _softmax_parts(s_parts):
    m = functools.reduce(jnp.maximum, [jnp.max(s, axis=-1, keepdims=True) for s in s_parts])
    e_parts = [jnp.exp(s - m) for s in s_parts]
    l = functools.reduce(jnp.add, [jnp.sum(e, axis=-1, keepdims=True) for e in e_parts])
    return e_parts, l


def _stack_heads(q):
    lane = lax.broadcasted_iota(jnp.int32, q.shape, 1)
    qf = q.astype(f32)
    even = jnp.where(lane < HEAD_DIM, qf, 0.0)
    odd = jnp.where(lane < HEAD_DIM, 0.0, qf)
    return jnp.concatenate([even, odd], axis=0).astype(bf16)


def _residual_out(x_ref, cat_ref, oc, wout_ref, gn_ref, o_ref, hn_ref):
    x1 = (x_ref[0] + _dot(cat_ref[...], wout_ref[0:ATTN_WIDTH, :])
          + _dot(oc, wout_ref[ATTN_WIDTH:, :]))
    o_ref[0] = x1
    hn_ref[0] = _rmsnorm(x1, gn_ref[...]).astype(bf16)


def _mix_prompt_kernel(q_ref, *refs):
    k_refs, v_refs = refs[:N_KBLK], refs[N_KBLK:2 * N_KBLK]
    (oc_ref, x_ref, f_ref, wout_ref, gn_ref, lng_ref, lnb_ref,
     o_ref, hn_ref, bias_ref, cat_ref) = refs[2 * N_KBLK:]
    i = pl.program_id(1)

    @pl.when((pl.program_id(0) == 0) & (i == 0))
    def _():
        for h in range(N_HEADS):
            p, par = divmod(h, 2)
            bias_ref[p, par * SUB:(par + 1) * SUB, :] = _toeplitz(f_ref[h:h + 1, :], SUB)[:, :WIN]

    row = lax.broadcasted_iota(jnp.int32, (2 * SUB, LANES), 0)
    lane = lax.broadcasted_iota(jnp.int32, (2 * SUB, LANES), 1)
    odd_chunk = (row & CHUNK) != 0
    edge_ok = {0: jnp.logical_not(odd_chunk & (lane < CHUNK)),
               WIN // LANES - 1: odd_chunk | (lane < CHUNK)}
    blk_ok = [i >= N_KBLK - 1 - b for b in range(N_KBLK - 1)]
    even_head = lax.broadcasted_iota(jnp.int32, (SUB, LANES), 1) < HEAD_DIM

    def pieces(r):
        out = []
        for b in range(N_KBLK):
            lo, hi = max(SUB * r, TQ * b), min(SUB * r + WIN, TQ * (b + 1))
            if lo < hi:
                out.append((b, lo - TQ * b, hi - TQ * b))
        return out

    def scores(r, p):
        rs, ps = slice(r * SUB, (r + 1) * SUB), slice(p * LANES, (p + 1) * LANES)
        q2 = _stack_heads(q_ref[0, rs, ps])
        return jnp.concatenate(
            [_dot_nt(q2, k_refs[b][0, a:z, ps]) for b, a, z in pieces(r)], axis=1)

    def finish(r, p, s):
        rs, ps = slice(r * SUB, (r + 1) * SUB), slice(p * LANES, (p + 1) * LANES)
        s = s + bias_ref[p]
        tiles = []
        for t in range(WIN // LANES):
            st = s[:, t * LANES:(t + 1) * LANES]
            b = (SUB * r + LANES * t) // TQ
            ok = edge_ok.get(t)
            if b < N_KBLK - 1:
                ok = blk_ok[b] if ok is None else ok & blk_ok[b]
            tiles.append(st if ok is None else jnp.where(ok, st, NEG_INF))
        (e,), l = _softmax_parts([jnp.concatenate(tiles, axis=1)])
        pb = e.astype(bf16)
        o, c0 = None, 0
        for b, a, z in pieces(r):
            ob = _dot(pb[:, c0:c0 + z - a], v_refs[b][0, a:z, ps])
            o = ob if o is None else o + ob
            c0 += z - a
        o = o / l
        cat_ref[rs, ps] = jnp.where(even_head, o[:SUB], o[SUB:]).astype(bf16)

    units = [(r, p) for r in range(TQ // SUB) for p in range(N_PAIRS)]
    pending = [scores(*u) for u in units[:LOOKAHEAD]]
    for n, unit in enumerate(units):
        if n + LOOKAHEAD < len(units):
            pending.append(scores(*units[n + LOOKAHEAD]))
        finish(*unit, pending.pop(0))

    oc = _ln_swish(oc_ref[0], lng_ref, lnb_ref).astype(bf16)
    _residual_out(x_ref, cat_ref, oc, wout_ref, gn_ref, o_ref, hn_ref)


def _mix_prompt(q, k, v, oc, x, fvec, wout, gn, lng, lnb):
    B, S, D = x.shape
    assert S % TQ == 0 and W_BAND % TQ == 0 and TQ % SUB == 0 and WIN % LANES == 0
    qspec = pl.BlockSpec((1, TQ, ATTN_WIDTH), lambda b, i: (b, i, 0))
    kspec = lambda back: pl.BlockSpec(
        (1, TQ, ATTN_WIDTH), lambda b, i: (b, jnp.maximum(i - back, 0), 0))
    kspecs = [kspec(N_KBLK - 1 - n) for n in range(N_KBLK)]
    xspec = pl.BlockSpec((1, TQ, D), lambda b, i: (b, i, 0))
    return pl.pallas_call(
        _mix_prompt_kernel,
        grid=(B, S // TQ),
        in_specs=[qspec] + kspecs + kspecs + [qspec, xspec,
                  _const_spec(fvec.shape), _const_spec(wout.shape), _const_spec(gn.shape),
                  _const_spec(lng.shape), _const_spec(lnb.shape)],
        out_specs=[xspec, xspec],
        out_shape=[jax.ShapeDtypeStruct((B, S, D), f32), jax.ShapeDtypeStruct((B, S, D), bf16)],
        scratch_shapes=[pltpu.VMEM((N_PAIRS, 2 * SUB, WIN), f32),
                        pltpu.VMEM((TQ, ATTN_WIDTH), bf16)],
        compiler_params=pltpu.CompilerParams(
            dimension_semantics=("arbitrary", "arbitrary"), vmem_limit_bytes=VMEM_LIMIT),
        name="mix_prompt",
    )(q, *([k] * N_KBLK), *([v] * N_KBLK), oc, x, fvec, wout, gn, lng, lnb)


def _mix_sample_kernel(q_ref, kn_ref, vn_ref, ck_ref, cv_ref, oc_ref, x_ref,
                       f_ref, wout_ref, gn_ref, o_ref, hn_ref, nk_ref, nv_ref, bias_ref, cat_ref):
    T = oc_ref.shape[1]
    W = ck_ref.shape[1]

    @pl.when(pl.program_id(0) == 0)
    def _():
        for h in range(N_HEADS):
            p, par = divmod(h, 2)
            bias_ref[p, par * T:(par + 1) * T, :] = _toeplitz(f_ref[h:h + 1, :], T)

    nk_ref[0, 0:W - T, :] = ck_ref[0, T:W, :]
    nk_ref[0, W - T:W, :] = kn_ref[0]
    nv_ref[0, 0:W - T, :] = cv_ref[0, T:W, :]
    nv_ref[0, W - T:W, :] = vn_ref[0]

    lane = lax.broadcasted_iota(jnp.int32, (T, LANES), 1)
    even_head = lane < HEAD_DIM
    for p in range(N_PAIRS):
        ps = slice(p * LANES, (p + 1) * LANES)
        kc = ck_ref[0, :, ps].astype(bf16)
        vc = cv_ref[0, :, ps].astype(bf16)
        kn = kn_ref[0, :, ps].astype(bf16)
        vn = vn_ref[0, :, ps].astype(bf16)
        q2 = _stack_heads(q_ref[0, :, ps])
        (ec, en), l = _softmax_parts([_dot_nt(q2, kc) + bias_ref[p, :, 0:W],
                                      _dot_nt(q2, kn) + bias_ref[p, :, W:W + T]])
        o = (_dot(ec.astype(bf16), vc) + _dot(en.astype(bf16), vn)) / l
        cat_ref[:, ps] = jnp.where(even_head, o[:T], o[T:]).astype(bf16)

    _residual_out(x_ref, cat_ref, oc_ref[0], wout_ref, gn_ref, o_ref, hn_ref)


def _mix_sample(q, kn, vn, ck, cv, oc, x, fvec, wout, gn):
    B, T, D = x.shape
    W = ck.shape[1]
    assert W == W_BAND and W + T <= F_LEN - T
    row = lambda r, n: pl.BlockSpec((1, r, n), lambda b: (b, 0, 0))
    return pl.pallas_call(
        _mix_sample_kernel,
        grid=(B,),
        in_specs=[row(T, ATTN_WIDTH)] * 3 + [row(W, ATTN_WIDTH)] * 2
                 + [row(T, CONV_WIDTH), row(T, D)]
                 + [_const_spec(a.shape) for a in (fvec, wout, gn)],
        out_specs=[row(T, D), row(T, D), row(W, ATTN_WIDTH), row(W, ATTN_WIDTH)],
        out_shape=[jax.ShapeDtypeStruct((B, T, D), f32),
                   jax.ShapeDtypeStruct((B, T, D), bf16),
                   jax.ShapeDtypeStruct((B, W, ATTN_WIDTH), f32),
                   jax.ShapeDtypeStruct((B, W, ATTN_WIDTH), f32)],
        scratch_shapes=[pltpu.VMEM((N_PAIRS, 2 * T, F_LEN), f32),
                        pltpu.VMEM((T, ATTN_WIDTH), bf16)],
        compiler_params=pltpu.CompilerParams(
            dimension_semantics=("arbitrary",), vmem_limit_bytes=VMEM_LIMIT),
        name="mix_sample",
    )(q, kn, vn, ck, cv, oc, x, fvec, wout, gn)


def _ffn_block(x, h, wg_ref, wu_ref, wd_ref, gf_ref):
    acc = x
    for c0 in range(0, D_FF, FFN_COLS):
        cs = slice(c0, min(c0 + FFN_COLS, D_FF))
        a = jax.nn.silu(_dot(h, wg_ref[:, cs])) * _dot(h, wu_ref[:, cs])
        acc = acc + _dot(a.astype(bf16), wd_ref[cs, :])
    return _rmsnorm(acc, gf_ref[...])


def _ffn_kernel(x_ref, h_ref, xs_ref, hs_ref, wg_ref, wu_ref, wd_ref, gf_ref, o_ref, os_ref):
    last = pl.num_programs(0) - 1

    @pl.when(pl.program_id(0) < last)
    def _():
        half = FFN_ROWS
        for r in range(x_ref.shape[0] // FFN_ROWS):
            rs = slice(r * half, (r + 1) * half)
            o_ref[rs, :] = _ffn_block(x_ref[rs, :], h_ref[rs, :], wg_ref, wu_ref, wd_ref, gf_ref)

    @pl.when(pl.program_id(0) == last)
    def _():
        os_ref[...] = _ffn_block(xs_ref[...], hs_ref[...], wg_ref, wu_ref, wd_ref, gf_ref)


def _ffn(x, h, xs, hs, wg, wu, wd, gf):
    N, D = x.shape
    assert N % TM_FFN == 0
    n = N // TM_FFN
    tile = pl.BlockSpec((TM_FFN, D), lambda i: (jnp.minimum(i, n - 1), 0))
    return pl.pallas_call(
        _ffn_kernel,
        grid=(n + 1,),
        in_specs=[tile, tile] + [_const_spec(a.shape) for a in (xs, hs, wg, wu, wd, gf)],
        out_specs=[tile, _const_spec(xs.shape)],
        out_shape=[jax.ShapeDtypeStruct((N, D), f32), jax.ShapeDtypeStruct(xs.shape, f32)],
        compiler_params=pltpu.CompilerParams(
            dimension_semantics=("arbitrary",), vmem_limit_bytes=VMEM_LIMIT),
        name="ffn",
    )(x, h, xs, hs, wg, wu, wd, gf)


def _rel_bias_row(table):
    n_rel = table.shape[1]
    far = W_BAND - REL_CLIP
    assert n_rel == 2 * REL_CLIP + 1 and far + n_rel + SUB <= F_LEN
    rep = lambda col, n: jnp.broadcast_to(table[:, col:col + 1], (table.shape[0], n))
    return jnp.concatenate([rep(n_rel - 1, far), table[:, ::-1],
                            rep(0, F_LEN - SUB - far - n_rel), rep(n_rel - 1, SUB)], axis=1)


def kernel(x_prompt, x_sample, cache_k, cache_v, state_conv, g_mix, w_in, rel_table, w_dw, b_dw,
           ln_g, ln_b, w_out, g_ffn, w_gate, w_up, w_down, g_final):
    assert g_mix.shape[0] == 1, "single-layer trunk: the final RMSNorm is fused into the FFN kernel"
    B, S, D = x_prompt.shape
    Bs, T, _ = x_sample.shape
    W = cache_k.shape[2]
    gm, gff, gf = g_mix[0].reshape(1, D), g_ffn[0].reshape(1, D), g_final.reshape(1, D)
    win = w_in[0]
    conv_params = (w_dw[0],) + tuple(a[0].reshape(1, CONV_WIDTH) for a in (b_dw, ln_g, ln_b))
    heads = lambda a: a.reshape(1, a.shape[0], a.shape[1], N_HEADS, HEAD_DIM)
    fvec = _rel_bias_row(rel_table[0])

    (q, k, v, oc, kf, vf, ct, qs, kn, vn, ocs, nc, wout, wg, wu, wd) = _inproj(
        x_prompt, x_sample.reshape(Bs * T, D), state_conv[0], gm, win, *conv_params,
        (w_out[0], w_gate[0], w_up[0], w_down[0]))
    x1p, hp = _mix_prompt(q, k, v, oc, x_prompt, fvec, wout, gff, *conv_params[2:])

    r3 = lambda a: a.reshape(Bs, T, ATTN_WIDTH)
    x1s, hs, nk, nv = _mix_sample(
        r3(qs), r3(kn), r3(vn),
        cache_k[0].reshape(Bs, W, ATTN_WIDTH), cache_v[0].reshape(Bs, W, ATTN_WIDTH),
        ocs, x_sample, fvec, wout, gff)

    flat = lambda a: a.reshape(-1, D)
    y_prompt, y_sample = _ffn(flat(x1p), flat(hp), flat(x1s), flat(hs), wg, wu, wd, gf)
    return (y_prompt.reshape(B, S, D), y_sample.reshape(Bs, T, D), heads(kf), heads(vf), ct[None],
            heads(nk), heads(nv), nc[None])
```

```python
import functools

import jax
import jax.numpy as jnp
from jax import lax
from jax.experimental import pallas as pl
from jax.experimental.pallas import tpu as pltpu

D_MODEL = 1024
CHUNK = 64
LEFT_CHUNKS = 8
W_BAND = LEFT_CHUNKS * CHUNK
ATTN_WIDTH = 512
N_HEADS = 8
HEAD_DIM = 64
CONV_WIDTH = 512
CONV_KERNEL = 31
CONV_PAST = CONV_KERNEL - 1
REL_CLIP = 128
D_FF = 2816
RMS_EPS = 1e-6
LN_EPS = 1e-5
NEG_INF = -1e30
SCALE = HEAD_DIM ** -0.5

LANES = 128
SUBLANES = 8
N_PAIRS = ATTN_WIDTH // LANES
TT_PROJ = 64
Z_STRIDE = TT_PROJ + SUBLANES
PROJ_COLS = 256
TQ = 512
N_KBLK = W_BAND // TQ + 1
SUB = 2 * CHUNK
WIN = W_BAND + SUB
F_LEN = 1024
LOOKAHEAD = 1
TM_FFN = 512
MXU_TILE = 256
FFN_COLS = 4 * MXU_TILE
CONV_ROWS = 64
CONV_STEPS = 16
CONV_LEAD = 32
VMEM_LIMIT = 56 * 1024 * 1024

f32 = jnp.float32
bf16 = jnp.bfloat16


def _rmsnorm(x, g):
    return (x * lax.rsqrt(jnp.mean(x * x, axis=-1, keepdims=True) + RMS_EPS)) * g


def _dot(a, b):
    return jnp.dot(a, b, preferred_element_type=f32)


def _dot_nt(a, b):
    return lax.dot_general(a, b, (((1,), (1,)), ((), ())), preferred_element_type=f32)


def _const_spec(shape):
    nd = len(shape)
    return pl.BlockSpec(shape, lambda *_: (0,) * nd, pipeline_mode=pl.Buffered(1))


def _project_stages(h_ref, w_ref, q_store, k_store, v_store):
    def issue(base, cs):
        return _dot(h_ref[...], w_ref[:, base + cs.start:base + cs.stop])

    def q_finish(cs, z):
        q_store(cs, (z * SCALE).astype(bf16))

    stages = []
    for base, finish in ((0, q_finish), (ATTN_WIDTH, k_store), (2 * ATTN_WIDTH, v_store)):
        for c0 in range(0, ATTN_WIDTH, PROJ_COLS):
            cs = slice(c0, c0 + PROJ_COLS)
            stages.append((functools.partial(issue, base, cs), functools.partial(finish, cs)))
    return stages


def _cast_weight(w32_ref, w_ref):
    for c0 in range(0, w_ref.shape[1], MXU_TILE):
        w_ref[:, c0:c0 + MXU_TILE] = w32_ref[:, c0:c0 + MXU_TILE].astype(bf16)


def _glu(h_ref, w_ref):
    base = 3 * ATTN_WIDTH
    a = _dot(h_ref[...], w_ref[:, base:base + CONV_WIDTH])
    g = _dot(h_ref[...], w_ref[:, base + CONV_WIDTH:base + 2 * CONV_WIDTH])
    return a * jax.nn.sigmoid(g)


def _conv_stages(cbuf_ref, sh_ref, y_ref, rows, wdw_ref, bdw_ref):
    lead = CONV_LEAD - CONV_PAST
    span = rows + CONV_LEAD - SUBLANES
    rb = min(CONV_ROWS, rows)

    def shift_stage(m):
        sh_ref[m - 1, 0:span, :] = cbuf_ref[m:m + span, :]

    def tap_stage(cs, r0):
        acc = jnp.zeros((rb, LANES), f32)
        for j in range(CONV_KERNEL):
            a, m = divmod(j + lead, SUBLANES)
            src = cbuf_ref if m == 0 else sh_ref.at[m - 1]
            lo = r0 + SUBLANES * a
            acc = acc + wdw_ref[j:j + 1, cs] * src[lo:lo + rb, cs]
        y_ref[r0:r0 + rb, cs] = acc + bdw_ref[:, cs]

    shifts = [functools.partial(shift_stage, m) for m in range(1, SUBLANES)]
    taps = [functools.partial(tap_stage, slice(c * LANES, (c + 1) * LANES), r0)
            for c in range(CONV_WIDTH // LANES) for r0 in range(0, rows, rb)]
    return shifts, taps


def _conv_time_major(ut_ref, y_ref, steps, wdw_ref, bdw_ref):
    for c in range(CONV_WIDTH // LANES):
        cs = slice(c * LANES, (c + 1) * LANES)
        for t0 in range(0, steps, CONV_STEPS):
            acc = jnp.zeros((CONV_STEPS, ut_ref.shape[2], LANES), f32)
            for j in range(CONV_KERNEL):
                acc = acc + wdw_ref[j:j + 1, cs] * ut_ref[c, t0 + j:t0 + j + CONV_STEPS]
            y_ref[c, t0:t0 + CONV_STEPS] = acc + bdw_ref[:, cs]


def _ln_swish_tiles(y_tiles, lng_ref, lnb_ref):
    n = len(y_tiles) * LANES
    mu = sum(jnp.sum(y, axis=-1, keepdims=True) for y in y_tiles) / n
    yc = [y - mu for y in y_tiles]
    var = sum(jnp.sum(y * y, axis=-1, keepdims=True) for y in yc) / n
    inv = lax.rsqrt(var + LN_EPS)
    out = []
    for c, y in enumerate(yc):
        cs = slice(c * LANES, (c + 1) * LANES)
        yn = y * inv * lng_ref[:, cs] + lnb_ref[:, cs]
        out.append(yn * jax.nn.sigmoid(yn))
    return out


def _ln_swish(y, lng_ref, lnb_ref):
    mu = jnp.mean(y, axis=-1, keepdims=True)
    yc = y - mu
    yn = yc * lax.rsqrt(jnp.mean(yc * yc, axis=-1, keepdims=True) + LN_EPS)
    yn = yn * lng_ref[...] + lnb_ref[...]
    return yn * jax.nn.sigmoid(yn)


def _pipeline(mxu_stages, valu_stages):
    pending = mxu_stages[0][0]()
    done = 0
    for n, (_, consume) in enumerate(mxu_stages):
        cur = pending
        if n + 1 < len(mxu_stages):
            pending = mxu_stages[n + 1][0]()
        upto = (n + 1) * len(valu_stages) // len(mxu_stages)
        for stage in valu_stages[done:upto]:
            stage()
        done = upto
        consume(cur)


def _inproj_kernel(x_ref, xs_ref, st_ref, g_ref, w32_ref, wdw_ref, bdw_ref, lng_ref, lnb_ref,
                   *refs, n_cast):
    cast_in, refs = refs[:n_cast], refs[n_cast:]
    prompt_out, sample_out = refs[:7], refs[7:12]
    cast_out, scratch = refs[12:12 + n_cast], refs[12 + n_cast:]
    w_ref, prompt_scratch, sample_scratch = scratch[0], scratch[1:5], scratch[5:]
    for src, dst in zip(cast_in, cast_out):
        dst[...] = src[...].astype(bf16)
    last = pl.num_programs(0) - 1

    @pl.when(pl.program_id(0) == 0)
    def _():
        _cast_weight(w32_ref, w_ref)
        ut_ref = prompt_scratch[2]
        ut_ref[:, 0:CONV_PAST] = jnp.zeros((ut_ref.shape[0], CONV_PAST) + ut_ref.shape[2:], f32)

    conv_refs = (wdw_ref, bdw_ref, lng_ref, lnb_ref)
    pl.when(pl.program_id(0) < last)(functools.partial(
        _inproj_prompt_tile, x_ref, g_ref, w_ref, conv_refs, prompt_out, prompt_scratch))
    pl.when(pl.program_id(0) == last)(functools.partial(
        _inproj_sample_tokens, xs_ref, st_ref, g_ref, w_ref, conv_refs, sample_out, sample_scratch))


def _inproj_prompt_tile(x_ref, g_ref, w_ref, conv_refs, out_refs, scratch):
    wdw_ref, bdw_ref, lng_ref, lnb_ref = conv_refs
    q_ref, k_ref, v_ref, oc_ref, kf_ref, vf_ref, ct_ref = out_refs
    h_ref, z_ref, ut_ref, y_ref = scratch
    nb = x_ref.shape[0]
    rows = nb * TT_PROJ
    h_ref[...] = _rmsnorm(x_ref[...].reshape(rows, D_MODEL), g_ref[...]).astype(bf16)

    def store(ref):
        def put(cs, val):
            ref[:, :, cs] = val.reshape(nb, TT_PROJ, val.shape[-1])
        return put

    def kv_store(ref, full_ref):
        def put(cs, val):
            val = val.reshape(nb, TT_PROJ, val.shape[-1])
            ref[:, :, cs] = val.astype(bf16)
            full_ref[:, :, cs] = val
        return put

    u = _glu(h_ref, w_ref)
    for issue, consume in _project_stages(h_ref, w_ref, store(q_ref),
                                          kv_store(k_ref, kf_ref), kv_store(v_ref, vf_ref)):
        consume(issue())
    lane_tiles = [slice(c * LANES, (c + 1) * LANES) for c in range(CONV_WIDTH // LANES)]
    for c, cs in enumerate(lane_tiles):
        for b in range(nb):
            z_ref[c, Z_STRIDE * b:Z_STRIDE * b + TT_PROJ, :] = u[b * TT_PROJ:(b + 1) * TT_PROJ, cs]
            ct_ref[b, :, cs] = z_ref[c, Z_STRIDE * b + TT_PROJ - CONV_PAST:Z_STRIDE * b + TT_PROJ, :]
        for t in range(TT_PROJ):
            ut_ref[c, CONV_PAST + t] = z_ref[c, pl.ds(t, nb, stride=Z_STRIDE), :]
    _conv_time_major(ut_ref, y_ref, TT_PROJ, wdw_ref, bdw_ref)
    for c, cs in enumerate(lane_tiles):
        for t in range(TT_PROJ):
            z_ref[c, pl.ds(t, nb, stride=Z_STRIDE), :] = y_ref[c, t]
        for b in range(nb):
            oc_ref[b, :, cs] = z_ref[c, Z_STRIDE * b:Z_STRIDE * b + TT_PROJ, :]

    ut_ref[:, 0:CONV_PAST] = ut_ref[:, TT_PROJ:TT_PROJ + CONV_PAST]


def _inproj(x, xs, st, g, w, wdw, bdw, lng, lnb, cast_weights):
    B, S, D = x.shape
    Ns, Bs = xs.shape[0], st.shape[0]
    T = Ns // Bs
    assert S % TT_PROJ == 0 and W_BAND % TT_PROJ == 0 and TT_PROJ >= CONV_PAST
    assert T >= CONV_PAST and T % SUBLANES == 0
    steps = S // TT_PROJ
    first_kept = (S - W_BAND) // TT_PROJ
    tile = lambda n: pl.BlockSpec((B, TT_PROJ, n), lambda i: (0, jnp.minimum(i, steps - 1), 0))
    kept = pl.BlockSpec((B, TT_PROJ, ATTN_WIDTH),
                        lambda i: (0, jnp.clip(i - first_kept, 0, steps - 1 - first_kept), 0))
    act = jax.ShapeDtypeStruct((B, S, ATTN_WIDTH), bf16)
    rows = B * TT_PROJ
    sample_shapes = ([jax.ShapeDtypeStruct((Ns, ATTN_WIDTH), bf16)]
                     + [jax.ShapeDtypeStruct((Ns, ATTN_WIDTH), f32)] * 2
                     + [jax.ShapeDtypeStruct((Bs, T, CONV_WIDTH), bf16),
                        jax.ShapeDtypeStruct((Bs, CONV_PAST, CONV_WIDTH), f32)])

    def cast_spec(a):
        blk = next(r for r in (32, 64, 128, 256) if a.shape[0] % r == 0 and a.shape[0] // r <= steps)
        last = a.shape[0] // blk - 1
        return pl.BlockSpec((blk, a.shape[1]), lambda i: (jnp.minimum(i, last), 0))

    cast_specs = [cast_spec(a) for a in cast_weights]
    return pl.pallas_call(
        functools.partial(_inproj_kernel, n_cast=len(cast_weights)),
        grid=(steps + 1,),
        in_specs=[tile(D)] + [_const_spec(a.shape) for a in (xs, st, g, w, wdw, bdw, lng, lnb)]
                 + cast_specs,
        out_specs=[tile(ATTN_WIDTH)] * 4 + [kept] * 2 + [_const_spec((B, CONV_PAST, CONV_WIDTH))]
                  + [_const_spec(a.shape) for a in sample_shapes] + cast_specs,
        out_shape=[act] * 3 + [jax.ShapeDtypeStruct((B, S, CONV_WIDTH), f32)]
                  + [jax.ShapeDtypeStruct((B, W_BAND, ATTN_WIDTH), f32)] * 2
                  + [jax.ShapeDtypeStruct((B, CONV_PAST, CONV_WIDTH), f32)] + sample_shapes
                  + [jax.ShapeDtypeStruct(a.shape, bf16) for a in cast_weights],
        scratch_shapes=[pltpu.VMEM(w.shape, bf16),
                        pltpu.VMEM((rows, D), bf16),
                        pltpu.VMEM((CONV_WIDTH // LANES, B * Z_STRIDE, LANES), f32),
                        pltpu.VMEM((CONV_WIDTH // LANES, CONV_PAST + TT_PROJ, B, LANES), f32),
                        pltpu.VMEM((CONV_WIDTH // LANES, TT_PROJ, B, LANES), f32),
                        pltpu.VMEM((Ns, D), bf16),
                        pltpu.VMEM((CONV_LEAD + T, CONV_WIDTH), f32),
                        pltpu.VMEM((SUBLANES - 1, CONV_LEAD + T - SUBLANES, CONV_WIDTH), f32),
                        pltpu.VMEM((T, CONV_WIDTH), f32)],
        compiler_params=pltpu.CompilerParams(
            dimension_semantics=("arbitrary",), vmem_limit_bytes=VMEM_LIMIT),
        name="inproj",
    )(x, xs, st, g, w, wdw, bdw, lng, lnb, *cast_weights)


def _inproj_sample_tokens(x_ref, st_ref, g_ref, w_ref, conv_refs, out_refs, scratch):
    wdw_ref, bdw_ref, lng_ref, lnb_ref = conv_refs
    q_ref, k_ref, v_ref, oc_ref, nc_ref = out_refs
    h_ref, cbuf_ref, sh_ref, y_ref = scratch
    B, T, _ = oc_ref.shape
    h_ref[...] = _rmsnorm(x_ref[...], g_ref[...]).astype(bf16)
    u = _glu(h_ref, w_ref)

    def store(ref):
        def put(cs, val):
            ref[:, cs] = val
        return put

    proj = _project_stages(h_ref, w_ref, store(q_ref), store(k_ref), store(v_ref))
    lead = CONV_LEAD - CONV_PAST

    def conv_stage(b):
        cbuf_ref[0:SUBLANES, :] = jnp.zeros((SUBLANES, CONV_WIDTH), f32)
        cbuf_ref[lead:CONV_LEAD, :] = st_ref[b]
        cbuf_ref[CONV_LEAD:CONV_LEAD + T, :] = u[b * T:(b + 1) * T]
        nc_ref[b] = cbuf_ref[CONV_LEAD + T - CONV_PAST:CONV_LEAD + T, :]
        shifts, taps = _conv_stages(cbuf_ref, sh_ref, y_ref, T, wdw_ref, bdw_ref)
        for stage in shifts + taps:
            stage()
        oc_ref[b] = _ln_swish(y_ref[...], lng_ref, lnb_ref).astype(bf16)

    _pipeline(proj, [functools.partial(conv_stage, b) for b in range(B)])


def _toeplitz(frow, rows):
    return pltpu.roll(jnp.broadcast_to(frow, (rows, F_LEN)), 0, 1, stride=1, stride_axis=0)


def _softmax_parts(s_parts):
    m = functools.reduce(jnp.maximum, [jnp.max(s, axis=-1, keepdims=True) for s in s_parts])
    e_parts = [jnp.exp(s - m) for s in s_parts]
    l = functools.reduce(jnp.add, [jnp.sum(e, axis=-1, keepdims=True) for e in e_parts])
    return e_parts, l


def _stack_heads(q):
    lane = lax.broadcasted_iota(jnp.int32, q.shape, 1)
    qf = q.astype(f32)
    even = jnp.where(lane < HEAD_DIM, qf, 0.0)
    odd = jnp.where(lane < HEAD_DIM, 0.0, qf)
    return jnp.concatenate([even, odd], axis=0).astype(bf16)


def _residual_out(x_ref, cat_ref, oc, wout_ref, gn_ref, o_ref, hn_ref):
    x1 = (x_ref[0] + _dot(cat_ref[...], wout_ref[0:ATTN_WIDTH, :])
          + _dot(oc, wout_ref[ATTN_WIDTH:, :]))
    o_ref[0] = x1
    hn_ref[0] = _rmsnorm(x1, gn_ref[...]).astype(bf16)


def _mix_prompt_kernel(q_ref, *refs):
    k_refs, v_refs = refs[:N_KBLK], refs[N_KBLK:2 * N_KBLK]
    (oc_ref, x_ref, f_ref, wout_ref, gn_ref, lng_ref, lnb_ref, wg_ref, wu_ref, wd_ref, gf_ref,
     o_ref, bias_ref, cat_ref, x1_ref, hn_ref) = refs[2 * N_KBLK:]
    i = pl.program_id(1)

    @pl.when((pl.program_id(0) == 0) & (i == 0))
    def _():
        for h in range(N_HEADS):
            p, par = divmod(h, 2)
            bias_ref[p, par * SUB:(par + 1) * SUB, :] = _toeplitz(f_ref[h:h + 1, :], SUB)[:, :WIN]

    row = lax.broadcasted_iota(jnp.int32, (2 * SUB, LANES), 0)
    lane = lax.broadcasted_iota(jnp.int32, (2 * SUB, LANES), 1)
    odd_chunk = (row & CHUNK) != 0
    edge_ok = {0: jnp.logical_not(odd_chunk & (lane < CHUNK)),
               WIN // LANES - 1: odd_chunk | (lane < CHUNK)}
    blk_ok = [i >= N_KBLK - 1 - b for b in range(N_KBLK - 1)]
    even_head = lax.broadcasted_iota(jnp.int32, (SUB, LANES), 1) < HEAD_DIM

    def pieces(r):
        out = []
        for b in range(N_KBLK):
            lo, hi = max(SUB * r, TQ * b), min(SUB * r + WIN, TQ * (b + 1))
            if lo < hi:
                out.append((b, lo - TQ * b, hi - TQ * b))
        return out

    def scores(r, p):
        rs, ps = slice(r * SUB, (r + 1) * SUB), slice(p * LANES, (p + 1) * LANES)
        q2 = _stack_heads(q_ref[0, rs, ps])
        return jnp.concatenate(
            [_dot_nt(q2, k_refs[b][0, a:z, ps]) for b, a, z in pieces(r)], axis=1)

    def finish(r, p, s):
        rs, ps = slice(r * SUB, (r + 1) * SUB), slice(p * LANES, (p + 1) * LANES)
        s = s + bias_ref[p]
        tiles = []
        for t in range(WIN // LANES):
            st = s[:, t * LANES:(t + 1) * LANES]
            b = (SUB * r + LANES * t) // TQ
            ok = edge_ok.get(t)
            if b < N_KBLK - 1:
                ok = blk_ok[b] if ok is None else ok & blk_ok[b]
            tiles.append(st if ok is None else jnp.where(ok, st, NEG_INF))
        (e,), l = _softmax_parts([jnp.concatenate(tiles, axis=1)])
        pb = e.astype(bf16)
        o, c0 = None, 0
        for b, a, z in pieces(r):
            ob = _dot(pb[:, c0:c0 + z - a], v_refs[b][0, a:z, ps])
            o = ob if o is None else o + ob
            c0 += z - a
        o = o / l
        cat_ref[rs, ps] = jnp.where(even_head, o[:SUB], o[SUB:]).astype(bf16)

    units = [(r, p) for r in range(TQ // SUB) for p in range(N_PAIRS)]
    pending = [scores(*u) for u in units[:LOOKAHEAD]]
    for n, unit in enumerate(units):
        if n + LOOKAHEAD < len(units):
            pending.append(scores(*units[n + LOOKAHEAD]))
        finish(*unit, pending.pop(0))

    oc = _ln_swish(oc_ref[0], lng_ref, lnb_ref).astype(bf16)
    x1 = (x_ref[0] + _dot(cat_ref[...], wout_ref[0:ATTN_WIDTH, :])
          + _dot(oc, wout_ref[ATTN_WIDTH:, :]))
    x1_ref[...] = x1
    hn_ref[...] = _rmsnorm(x1, gn_ref[...]).astype(bf16)

    @pl.when(pl.program_id(0) >= 0)
    def _():
        half = TQ // 2
        for r in range(2):
            rs = slice(r * half, (r + 1) * half)
            o_ref[0, rs, :] = _ffn_block(x1_ref[rs, :], hn_ref[rs, :],
                                         wg_ref, wu_ref, wd_ref, gf_ref)


def _mix_prompt(q, k, v, oc, x, fvec, wout, gn, lng, lnb, wg, wu, wd, gf):
    B, S, D = x.shape
    assert S % TQ == 0 and W_BAND % TQ == 0 and TQ % SUB == 0 and WIN % LANES == 0
    qspec = pl.BlockSpec((1, TQ, ATTN_WIDTH), lambda b, i: (b, i, 0))
    kspec = lambda back: pl.BlockSpec(
        (1, TQ, ATTN_WIDTH), lambda b, i: (b, jnp.maximum(i - back, 0), 0))
    kspecs = [kspec(N_KBLK - 1 - n) for n in range(N_KBLK)]
    xspec = pl.BlockSpec((1, TQ, D), lambda b, i: (b, i, 0))
    return pl.pallas_call(
        _mix_prompt_kernel,
        grid=(B, S // TQ),
        in_specs=[qspec] + kspecs + kspecs + [qspec, xspec,
                  _const_spec(fvec.shape), _const_spec(wout.shape), _const_spec(gn.shape),
                  _const_spec(lng.shape), _const_spec(lnb.shape)]
                 + [_const_spec(a.shape) for a in (wg, wu, wd, gf)],
        out_specs=xspec,
        out_shape=jax.ShapeDtypeStruct((B, S, D), f32),
        scratch_shapes=[pltpu.VMEM((N_PAIRS, 2 * SUB, WIN), f32),
                        pltpu.VMEM((TQ, ATTN_WIDTH), bf16),
                        pltpu.VMEM((TQ, D), f32),
                        pltpu.VMEM((TQ, D), bf16)],
        compiler_params=pltpu.CompilerParams(
            dimension_semantics=("arbitrary", "arbitrary"), vmem_limit_bytes=VMEM_LIMIT),
        name="mix_prompt",
    )(q, *([k] * N_KBLK), *([v] * N_KBLK), oc, x, fvec, wout, gn, lng, lnb, wg, wu, wd, gf)


def _mix_sample_kernel(q_ref, kn_ref, vn_ref, ck_ref, cv_ref, oc_ref, x_ref,
                       f_ref, wout_ref, gn_ref, o_ref, hn_ref, nk_ref, nv_ref, bias_ref, cat_ref):
    T = oc_ref.shape[1]
    W = ck_ref.shape[1]

    @pl.when(pl.program_id(0) == 0)
    def _():
        for h in range(N_HEADS):
            p, par = divmod(h, 2)
            bias_ref[p, par * T:(par + 1) * T, :] = _toeplitz(f_ref[h:h + 1, :], T)

    nk_ref[0, 0:W - T, :] = ck_ref[0, T:W, :]
    nk_ref[0, W - T:W, :] = kn_ref[0]
    nv_ref[0, 0:W - T, :] = cv_ref[0, T:W, :]
    nv_ref[0, W - T:W, :] = vn_ref[0]

    lane = lax.broadcasted_iota(jnp.int32, (T, LANES), 1)
    even_head = lane < HEAD_DIM
    for p in range(N_PAIRS):
        ps = slice(p * LANES, (p + 1) * LANES)
        kc = ck_ref[0, :, ps].astype(bf16)
        vc = cv_ref[0, :, ps].astype(bf16)
        kn = kn_ref[0, :, ps].astype(bf16)
        vn = vn_ref[0, :, ps].astype(bf16)
        q2 = _stack_heads(q_ref[0, :, ps])
        (ec, en), l = _softmax_parts([_dot_nt(q2, kc) + bias_ref[p, :, 0:W],
                                      _dot_nt(q2, kn) + bias_ref[p, :, W:W + T]])
        o = (_dot(ec.astype(bf16), vc) + _dot(en.astype(bf16), vn)) / l
        cat_ref[:, ps] = jnp.where(even_head, o[:T], o[T:]).astype(bf16)

    _residual_out(x_ref, cat_ref, oc_ref[0], wout_ref, gn_ref, o_ref, hn_ref)


def _mix_sample(q, kn, vn, ck, cv, oc, x, fvec, wout, gn):
    B, T, D = x.shape
    W = ck.shape[1]
    assert W == W_BAND and W + T <= F_LEN - T
    row = lambda r, n: pl.BlockSpec((1, r, n), lambda b: (b, 0, 0))
    return pl.pallas_call(
        _mix_sample_kernel,
        grid=(B,),
        in_specs=[row(T, ATTN_WIDTH)] * 3 + [row(W, ATTN_WIDTH)] * 2
                 + [row(T, CONV_WIDTH), row(T, D)]
                 + [_const_spec(a.shape) for a in (fvec, wout, gn)],
        out_specs=[row(T, D), row(T, D), row(W, ATTN_WIDTH), row(W, ATTN_WIDTH)],
        out_shape=[jax.ShapeDtypeStruct((B, T, D), f32),
                   jax.ShapeDtypeStruct((B, T, D), bf16),
                   jax.ShapeDtypeStruct((B, W, ATTN_WIDTH), f32),
                   jax.ShapeDtypeStruct((B, W, ATTN_WIDTH), f32)],
        scratch_shapes=[pltpu.VMEM((N_PAIRS, 2 * T, F_LEN), f32),
                        pltpu.VMEM((T, ATTN_WIDTH), bf16)],
        compiler_params=pltpu.CompilerParams(
            dimension_semantics=("arbitrary",), vmem_limit_bytes=VMEM_LIMIT),
        name="mix_sample",
    )(q, kn, vn, ck, cv, oc, x, fvec, wout, gn)


def _ffn_block(x, h, wg_ref, wu_ref, wd_ref, gf_ref):
    acc = x
    for c0 in range(0, D_FF, FFN_COLS):
        cs = slice(c0, min(c0 + FFN_COLS, D_FF))
        a = jax.nn.silu(_dot(h, wg_ref[:, cs])) * _dot(h, wu_ref[:, cs])
        acc = acc + _dot(a.astype(bf16), wd_ref[cs, :])
    return _rmsnorm(acc, gf_ref[...])


def _ffn_sample_kernel(xs_ref, hs_ref, wg_ref, wu_ref, wd_ref, gf_ref, os_ref):
    os_ref[...] = _ffn_block(xs_ref[...], hs_ref[...], wg_ref, wu_ref, wd_ref, gf_ref)


def _ffn_sample(xs, hs, wg, wu, wd, gf):
    return pl.pallas_call(
        _ffn_sample_kernel,
        grid=(1,),
        in_specs=[_const_spec(a.shape) for a in (xs, hs, wg, wu, wd, gf)],
        out_specs=_const_spec(xs.shape),
        out_shape=jax.ShapeDtypeStruct(xs.shape, f32),
        compiler_params=pltpu.CompilerParams(
            dimension_semantics=("arbitrary",), vmem_limit_bytes=VMEM_LIMIT),
        name="ffn_sample",
    )(xs, hs, wg, wu, wd, gf)


def _rel_bias_row(table):
    n_rel = table.shape[1]
    far = W_BAND - REL_CLIP
    assert n_rel == 2 * REL_CLIP + 1 and far + n_rel + SUB <= F_LEN
    rep = lambda col, n: jnp.broadcast_to(table[:, col:col + 1], (table.shape[0], n))
    return jnp.concatenate([rep(n_rel - 1, far), table[:, ::-1],
                            rep(0, F_LEN - SUB - far - n_rel), rep(n_rel - 1, SUB)], axis=1)


def kernel(x_prompt, x_sample, cache_k, cache_v, state_conv, g_mix, w_in, rel_table, w_dw, b_dw,
           ln_g, ln_b, w_out, g_ffn, w_gate, w_up, w_down, g_final):
    assert g_mix.shape[0] == 1, "single-layer trunk: the final RMSNorm is fused into the FFN kernel"
    B, S, D = x_prompt.shape
    Bs, T, _ = x_sample.shape
    W = cache_k.shape[2]
    gm, gff, gf = g_mix[0].reshape(1, D), g_ffn[0].reshape(1, D), g_final.reshape(1, D)
    win = w_in[0]
    conv_params = (w_dw[0],) + tuple(a[0].reshape(1, CONV_WIDTH) for a in (b_dw, ln_g, ln_b))
    heads = lambda a: a.reshape(1, a.shape[0], a.shape[1], N_HEADS, HEAD_DIM)
    fvec = _rel_bias_row(rel_table[0])

    (q, k, v, oc, kf, vf, ct, qs, kn, vn, ocs, nc, wout, wg, wu, wd) = _inproj(
        x_prompt, x_sample.reshape(Bs * T, D), state_conv[0], gm, win, *conv_params,
        (w_out[0], w_gate[0], w_up[0], w_down[0]))
    y_prompt = _mix_prompt(q, k, v, oc, x_prompt, fvec, wout, gff, *conv_params[2:],
                           wg, wu, wd, gf)

    r3 = lambda a: a.reshape(Bs, T, ATTN_WIDTH)
    x1s, hs, nk, nv = _mix_sample(
        r3(qs), r3(kn), r3(vn),
        cache_k[0].reshape(Bs, W, ATTN_WIDTH), cache_v[0].reshape(Bs, W, ATTN_WIDTH),
        ocs, x_sample, fvec, wout, gff)

    flat = lambda a: a.reshape(-1, D)
    y_sample = _ffn_sample(flat(x1s), flat(hs), wg, wu, wd, gf)
    return (y_prompt, y_sample.reshape(Bs, T, D), heads(kf), heads(vf), ct[None],
            heads(nk), heads(nv), nc[None])
```
